```python
import math
import jax
import jax.numpy as jnp
from jax import lax

D_MODEL = 2048
BATCH = 8
SEQ = 2048
DEPTH = 2
DEC_BATCH = 16
DEC_SEQ = 64
PAST_LEN = 4096

CHUNK = 64
N_EVEN = (DEPTH + 1) // 2
N_ODD = DEPTH // 2

SB_HEAD_DIM = 64
SB_HEADS = D_MODEL // 128
SB_WIDTH = SB_HEADS * SB_HEAD_DIM
SB_QBLOCK = 128

SSD_HEAD_DIM = 64
SSD_WIDTH = D_MODEL
SSD_HEADS = SSD_WIDTH // SSD_HEAD_DIM
SSD_GROUPS = 4
SSD_STATE = 128
SSD_CONV = 4
SSD_CONV_DIM = SSD_WIDTH + 2 * SSD_GROUPS * SSD_STATE
SSD_CHUNK = 64

CA_HEAD_DIM = 64
CA_WIDTH = D_MODEL
CA_HEADS = CA_WIDTH // CA_HEAD_DIM
CA_LEFT_CHUNKS = 8
CA_REL_CLIP = 128

DEEPNORM_ALPHA = (2 * DEPTH) ** 0.25
DEEPNORM_BETA = (8 * DEPTH) ** -0.25
LN_EPS = 1e-5
RMS_EPS = 1e-5

L0_SIZES = (SB_WIDTH, SB_WIDTH, SB_WIDTH, SB_WIDTH, SSD_WIDTH, SSD_CONV_DIM, SSD_HEADS)
L0_SPLITS = tuple(sum(L0_SIZES[:i + 1]) for i in range(len(L0_SIZES) - 1))
L0_IN = sum(L0_SIZES)
L0_MIX = SB_WIDTH + SSD_WIDTH
L1_IN = 4 * CA_WIDTH

kernel_name = 'hybrid_streaming_encoder_step'


def layer_norm(x, g, b):
    xf = x.astype(jnp.float32)
    mu = jnp.mean(xf, axis=-1, keepdims=True)
    var = jnp.mean(jnp.square(xf - mu), axis=-1, keepdims=True)
    return ((xf - mu) * lax.rsqrt(var + LN_EPS) * g + b).astype(x.dtype)


def _sb_block(q, k, v, q_pos0):
    z = jnp.einsum('bqhd,bkhd->bhqk', q, k).astype(jnp.float32) * (SB_HEAD_DIM ** -0.5)
    q_pos = q_pos0 + jnp.arange(q.shape[1])
    k_pos = jnp.arange(k.shape[1])
    strict = k_pos[None, :] < q_pos[:, None]
    sp = jnp.where(strict, jax.nn.softplus(z), 0.0)
    later = lax.cumsum(sp, axis=3, reverse=True) - sp
    w = jnp.where(strict, jnp.exp(jax.nn.log_sigmoid(z) - later), 0.0)
    return jnp.einsum('bhqk,bkhd->bqhd', w.astype(v.dtype), v)


def stick_breaking(q, k, v, q_pos0):
    tq = q.shape[1]
    outs = []
    for s in range(0, tq, SB_QBLOCK):
        e = min(s + SB_QBLOCK, tq)
        outs.append(_sb_block(q[:, s:e], k[:, :q_pos0 + e], v[:, :q_pos0 + e], q_pos0 + s))
    return jnp.concatenate(outs, axis=1)


def causal_conv(xbc, conv_state, conv_w, conv_b):
    L = xbc.shape[1]
    xp = jnp.concatenate([conv_state.astype(xbc.dtype), xbc], axis=1)
    y = conv_b + conv_w[0] * xp[:, 0:L]
    for t in range(1, SSD_CONV):
        y = y + conv_w[t] * xp[:, t:t + L]
    return jax.nn.silu(y), xp[:, -(SSD_CONV - 1):]


def ssd_scan(x, dt, a, bm, cm, h0):
    b_, L = x.shape[:2]
    q_len = min(SSD_CHUNK, L)
    nc = L // q_len
    r = SSD_HEADS // SSD_GROUPS
    f32 = jnp.float32
    xc = x.astype(f32).reshape(b_, nc, q_len, SSD_GROUPS, r, SSD_HEAD_DIM)
    dtc = dt.reshape(b_, nc, q_len, SSD_GROUPS, r)
    bc = bm.astype(f32).reshape(b_, nc, q_len, SSD_GROUPS, SSD_STATE)
    cc = cm.astype(f32).reshape(b_, nc, q_len, SSD_GROUPS, SSD_STATE)
    cum = jnp.cumsum(dtc * a.reshape(SSD_GROUPS, r), axis=2)
    seg = cum[:, :, :, None] - cum[:, :, None, :]
    tril = jnp.tril(jnp.ones((q_len, q_len), dtype=bool))[:, :, None, None]
    decay = jnp.exp(jnp.where(tril, seg, -jnp.inf))
    cb = jnp.einsum('bctgn,bcsgn->bctsg', cc, bc)
    w = cb[..., None] * decay * dtc[:, :, None]
    y_diag = jnp.einsum('bctsgr,bcsgrp->bctgrp', w, xc)
    to_end = jnp.exp(cum[:, :, -1:] - cum) * dtc
    states = jnp.einsum('bcsgn,bcsgr,bcsgrp->bcgrpn', bc, to_end, xc)
    chunk_decay = jnp.exp(cum[:, :, -1])

    def step(h, inp):
        st, dec = inp
        return dec[..., None, None] * h + st, h

    h_init = h0.astype(f32).reshape(b_, SSD_GROUPS, r, SSD_HEAD_DIM, SSD_STATE)
    h_final, h_in = lax.scan(step, h_init, (jnp.moveaxis(states, 1, 0), jnp.moveaxis(chunk_decay, 1, 0)))
    h_in = jnp.moveaxis(h_in, 0, 1)
    y_off = jnp.einsum('bctgn,bcgrpn,bctgr->bctgrp', cc, h_in, jnp.exp(cum))
    y = (y_diag + y_off).reshape(b_, L, SSD_HEADS, SSD_HEAD_DIM)
    return y, h_final.reshape(b_, SSD_HEADS, SSD_HEAD_DIM, SSD_STATE)


def ssd_branch(z, xbc, dt_raw, conv_state, h0, conv_w, conv_b, dt_bias, a_log, d_skip, norm_w):
    b_, L = z.shape[:2]
    gn = SSD_GROUPS * SSD_STATE
    xbc_act, new_conv = causal_conv(xbc, conv_state, conv_w, conv_b)
    xs = xbc_act[..., :SSD_WIDTH].reshape(b_, L, SSD_HEADS, SSD_HEAD_DIM)
    bm = xbc_act[..., SSD_WIDTH:SSD_WIDTH + gn].reshape(b_, L, SSD_GROUPS, SSD_STATE)
    cm = xbc_act[..., SSD_WIDTH + gn:].reshape(b_, L, SSD_GROUPS, SSD_STATE)
    dt = jax.nn.softplus(dt_raw.astype(jnp.float32) + dt_bias.astype(jnp.float32))
    a = -jnp.exp(a_log.astype(jnp.float32))
    y, new_h = ssd_scan(xs, dt, a, bm, cm, h0)
    y = y + d_skip.astype(jnp.float32)[:, None] * xs.astype(jnp.float32)
    y = y.reshape(b_, L, SSD_WIDTH) * jax.nn.silu(z.astype(jnp.float32))
    yg = y.reshape(b_, L, SSD_GROUPS, SSD_WIDTH // SSD_GROUPS)
    yg = yg * lax.rsqrt(jnp.mean(yg * yg, axis=-1, keepdims=True) + RMS_EPS)
    y = yg.reshape(b_, L, SSD_WIDTH) * norm_w
    return y.astype(z.dtype), new_conv, new_h


def even_layer(x, past_k, past_v, conv_state, h0, w_in, conv_w, conv_b, dt_bias, a_log, d_skip,
               norm_w, w_out, ln_g, ln_b):
    b_, L = x.shape[:2]
    proj = jnp.einsum('bld,de->ble', x, w_in)
    q, k, v, g_a, z, xbc, dt_raw = jnp.split(proj, L0_SPLITS, axis=-1)
    q = q.reshape(b_, L, SB_HEADS, SB_HEAD_DIM)
    k = k.reshape(b_, L, SB_HEADS, SB_HEAD_DIM)
    v = v.reshape(b_, L, SB_HEADS, SB_HEAD_DIM)
    k_all = jnp.concatenate([past_k.astype(k.dtype), k], axis=1)
    v_all = jnp.concatenate([past_v.astype(v.dtype), v], axis=1)
    o_a = stick_breaking(q, k_all, v_all, past_k.shape[1]).reshape(b_, L, SB_WIDTH) * jax.nn.silu(g_a)
    o_b, new_conv, new_h = ssd_branch(z, xbc, dt_raw, conv_state, h0, conv_w, conv_b, dt_bias, a_log,
                                      d_skip, norm_w)
    mix = jnp.einsum('ble,ed->bld', jnp.concatenate([o_a, o_b], axis=-1), w_out)
    y = layer_norm(DEEPNORM_ALPHA * x + mix, ln_g, ln_b)
    return y, k, v, new_conv, new_h


def chunk_band_attention(q, k, v, past_k, past_v, rel_bias):
    b_, L, H, d = q.shape
    P = past_k.shape[1]
    left = CA_LEFT_CHUNKS * CHUNK
    qn = min(CHUNK, L)
    n_blocks = L // qn
    band = left + qn
    pad = jnp.zeros((b_, left - P, H, d), k.dtype)
    kp = jnp.concatenate([pad, past_k.astype(k.dtype), k], axis=1)
    vp = jnp.concatenate([pad, past_v.astype(v.dtype), v], axis=1)
    valid = jnp.arange(left + L) >= left - P
    qi = jnp.arange(qn)[:, None]
    kj = jnp.arange(band)[None, :]
    rel = jnp.clip(qi - kj + left, -CA_REL_CLIP, CA_REL_CLIP) + CA_REL_CLIP
    bias = rel_bias[:, rel].astype(jnp.float32)

    def one_block(c):
        start = c * qn
        qc = lax.dynamic_slice_in_dim(q, start, qn, axis=1)
        kc = lax.dynamic_slice_in_dim(kp, start, band, axis=1)
        vc = lax.dynamic_slice_in_dim(vp, start, band, axis=1)
        mc = lax.dynamic_slice_in_dim(valid, start, band, axis=0)
        s = jnp.einsum('bqhd,bkhd->bhqk', qc, kc).astype(jnp.float32) * (CA_HEAD_DIM ** -0.5) + bias
        s = jnp.where(mc, s, -jnp.inf)
        p = jax.nn.softmax(s, axis=-1)
        return jnp.einsum('bhqk,bkhd->bqhd', p.astype(vc.dtype), vc)

    out = lax.map(one_block, jnp.arange(n_blocks))
    return jnp.moveaxis(out, 0, 1).reshape(b_, L, H, d)


def odd_layer(x, past_k, past_v, w_in, rel_bias, w_out, ln_g, ln_b):
    b_, L = x.shape[:2]
    proj = jnp.einsum('bld,de->ble', x, w_in)
    q, k, v, g = jnp.split(proj, 4, axis=-1)
    q = q.reshape(b_, L, CA_HEADS, CA_HEAD_DIM)
    k = k.reshape(b_, L, CA_HEADS, CA_HEAD_DIM)
    v = v.reshape(b_, L, CA_HEADS, CA_HEAD_DIM)
    o = chunk_band_attention(q, k, v, past_k, past_v, rel_bias).reshape(b_, L, CA_WIDTH) * jax.nn.silu(g)
    mix = jnp.einsum('ble,ed->bld', o, w_out)
    y = layer_norm(DEEPNORM_ALPHA * x + mix, ln_g, ln_b)
    k_cat = jnp.concatenate([past_k.astype(k.dtype), k], axis=1)
    v_cat = jnp.concatenate([past_v.astype(v.dtype), v], axis=1)
    keep = min(CA_LEFT_CHUNKS * CHUNK, k_cat.shape[1])
    return y, k_cat[:, -keep:], v_cat[:, -keep:]


def setup_inputs(seed: int = 0) -> dict:
    key = jax.random.key(seed)
    ks = jax.random.split(key, 24)
    f32 = jnp.float32
    nrm = jax.random.normal
    band_rows = min(CA_LEFT_CHUNKS * CHUNK, PAST_LEN)
    x_prompt = nrm(ks[0], (BATCH, SEQ, D_MODEL), f32)
    x_sample = nrm(ks[1], (DEC_BATCH, DEC_SEQ, D_MODEL), f32)
    cache_sb_k = nrm(ks[2], (N_EVEN, DEC_BATCH, PAST_LEN, SB_HEADS, SB_HEAD_DIM), f32)
    cache_sb_v = 0.5 * nrm(ks[3], (N_EVEN, DEC_BATCH, PAST_LEN, SB_HEADS, SB_HEAD_DIM), f32)
    state_ssm = 0.1 * nrm(ks[4], (N_EVEN, DEC_BATCH, SSD_HEADS, SSD_HEAD_DIM, SSD_STATE), f32)
    state_conv = nrm(ks[5], (N_EVEN, DEC_BATCH, SSD_CONV - 1, SSD_CONV_DIM), f32)
    cache_band_k = nrm(ks[6], (N_ODD, DEC_BATCH, band_rows, CA_HEADS, CA_HEAD_DIM), f32)
    cache_band_v = 0.5 * nrm(ks[7], (N_ODD, DEC_BATCH, band_rows, CA_HEADS, CA_HEAD_DIM), f32)
    col0 = jnp.concatenate([jnp.ones((2 * SB_WIDTH,), f32), jnp.full((SB_WIDTH,), DEEPNORM_BETA, f32),
                            jnp.ones((L0_IN - 3 * SB_WIDTH,), f32)])
    even_w_in = nrm(ks[8], (N_EVEN, D_MODEL, L0_IN), f32) * (D_MODEL ** -0.5) * col0
    even_conv_w = 0.5 * nrm(ks[9], (N_EVEN, SSD_CONV, SSD_CONV_DIM), f32)
    even_conv_b = 0.1 * nrm(ks[10], (N_EVEN, SSD_CONV_DIM), f32)
    u = jax.random.uniform(ks[11], (N_EVEN, SSD_HEADS), f32)
    dt0 = jnp.exp(u * (math.log(0.1) - math.log(0.001)) + math.log(0.001))
    even_dt_bias = dt0 + jnp.log(-jnp.expm1(-dt0))
    even_a_log = jnp.log(jax.random.uniform(ks[12], (N_EVEN, SSD_HEADS), f32, 1.0, 16.0))
    even_d_skip = 1.0 + 0.1 * nrm(ks[13], (N_EVEN, SSD_HEADS), f32)
    even_norm_w = 1.0 + 0.1 * nrm(ks[14], (N_EVEN, SSD_WIDTH), f32)
    even_w_out = nrm(ks[15], (N_EVEN, L0_MIX, D_MODEL), f32) * (L0_MIX ** -0.5) * DEEPNORM_BETA
    even_ln_g = 1.0 + 0.1 * nrm(ks[16], (N_EVEN, D_MODEL), f32)
    even_ln_b = 0.1 * nrm(ks[17], (N_EVEN, D_MODEL), f32)
    col1 = jnp.concatenate([jnp.ones((2 * CA_WIDTH,), f32), jnp.full((CA_WIDTH,), DEEPNORM_BETA, f32),
                            jnp.ones((CA_WIDTH,), f32)])
    odd_w_in = nrm(ks[18], (N_ODD, D_MODEL, L1_IN), f32) * (D_MODEL ** -0.5) * col1
    odd_rel_bias = 0.5 * nrm(ks[19], (N_ODD, CA_HEADS, 2 * CA_REL_CLIP + 1), f32)
    odd_w_out = nrm(ks[20], (N_ODD, CA_WIDTH, D_MODEL), f32) * (CA_WIDTH ** -0.5) * DEEPNORM_BETA
    odd_ln_g = 1.0 + 0.1 * nrm(ks[21], (N_ODD, D_MODEL), f32)
    odd_ln_b = 0.1 * nrm(ks[22], (N_ODD, D_MODEL), f32)
    return {'x_prompt': x_prompt, 'x_sample': x_sample,
            'cache_sb_k': cache_sb_k, 'cache_sb_v': cache_sb_v,
            'state_ssm': state_ssm, 'state_conv': state_conv,
            'cache_band_k': cache_band_k, 'cache_band_v': cache_band_v,
            'even_w_in': even_w_in, 'even_conv_w': even_conv_w, 'even_conv_b': even_conv_b,
            'even_dt_bias': even_dt_bias, 'even_a_log': even_a_log, 'even_d_skip': even_d_skip,
            'even_norm_w': even_norm_w, 'even_w_out': even_w_out, 'even_ln_g': even_ln_g,
            'even_ln_b': even_ln_b,
            'odd_w_in': odd_w_in, 'odd_rel_bias': odd_rel_bias, 'odd_w_out': odd_w_out,
            'odd_ln_g': odd_ln_g, 'odd_ln_b': odd_ln_b}


def reference(x_prompt, x_sample, cache_sb_k, cache_sb_v, state_ssm, state_conv, cache_band_k,
              cache_band_v, even_w_in, even_conv_w, even_conv_b, even_dt_bias, even_a_log, even_d_skip,
              even_norm_w, even_w_out, even_ln_g, even_ln_b, odd_w_in, odd_rel_bias, odd_w_out,
              odd_ln_g, odd_ln_b):
    bp = x_prompt.shape[0]
    yp, ys = x_prompt, x_sample
    p_sb_k, p_sb_v, p_ssm, p_conv, p_band_k, p_band_v = [], [], [], [], [], []
    s_sb_k, s_sb_v, s_ssm, s_conv, s_band_k, s_band_v = [], [], [], [], [], []
    for layer in range(DEPTH):
        i = layer // 2
        if layer % 2 == 0:
            ew = (even_w_in[i], even_conv_w[i], even_conv_b[i], even_dt_bias[i], even_a_log[i],
                  even_d_skip[i], even_norm_w[i], even_w_out[i], even_ln_g[i], even_ln_b[i])
            no_past = jnp.zeros((bp, 0, SB_HEADS, SB_HEAD_DIM), yp.dtype)
            zero_conv = jnp.zeros((bp, SSD_CONV - 1, SSD_CONV_DIM), yp.dtype)
            zero_h = jnp.zeros((bp, SSD_HEADS, SSD_HEAD_DIM, SSD_STATE), jnp.float32)
            yp, k_new, v_new, conv_new, h_new = even_layer(yp, no_past, no_past, zero_conv, zero_h, *ew)
            p_sb_k.append(k_new)
            p_sb_v.append(v_new)
            p_conv.append(conv_new)
            p_ssm.append(h_new)
            ys, k_new, v_new, conv_new, h_new = even_layer(ys, cache_sb_k[i], cache_sb_v[i], state_conv[i],
                                                           state_ssm[i], *ew)
            s_sb_k.append(k_new)
            s_sb_v.append(v_new)
            s_conv.append(conv_new)
            s_ssm.append(h_new)
        else:
            ow = (odd_w_in[i], odd_rel_bias[i], odd_w_out[i], odd_ln_g[i], odd_ln_b[i])
            no_past = jnp.zeros((bp, 0, CA_HEADS, CA_HEAD_DIM), yp.dtype)
            yp, bk, bv = odd_layer(yp, no_past, no_past, *ow)
            p_band_k.append(bk)
            p_band_v.append(bv)
            ys, bk, bv = odd_layer(ys, cache_band_k[i], cache_band_v[i], *ow)
            s_band_k.append(bk)
            s_band_v.append(bv)
    return (yp, ys,
            jnp.stack(p_sb_k), jnp.stack(p_sb_v), jnp.stack(p_ssm), jnp.stack(p_conv),
            jnp.stack(p_band_k), jnp.stack(p_band_v),
            jnp.stack(s_sb_k), jnp.stack(s_sb_v), jnp.stack(s_ssm), jnp.stack(s_conv),
            jnp.stack(s_band_k), jnp.stack(s_band_v))
```

```python
import functools
import math

import jax
import jax.numpy as jnp
from jax import lax
from jax.experimental import pallas as pl
from jax.experimental.pallas import tpu as pltpu

F32 = jnp.float32
BF16 = jnp.bfloat16

D_MODEL = 2048
CHUNK = 64
HEAD_DIM = 64
LANES = 128
SB_WIDTH = 1024
SSD_WIDTH = 2048
SSD_HEADS = 32
SSD_GROUPS = 4
SSD_STATE = 128
SSD_CONV = 4
SSD_CONV_DIM = SSD_WIDTH + 2 * SSD_GROUPS * SSD_STATE
CA_WIDTH = 2048
CA_HEADS = 32
CA_LEFT = 8 * CHUNK
CA_REL_CLIP = 128
DEPTH = 2
DEEPNORM_ALPHA = (2 * DEPTH) ** 0.25
LN_EPS = 1e-5
RMS_EPS = 1e-5
ATTN_SCALE = HEAD_DIM ** -0.5
NEG_BIG = -1e30
L0_Q, L0_K, L0_V, L0_G, L0_Z, L0_XBC, L0_DT = 0, 1024, 2048, 3072, 4096, 6144, 9216
VMEM_LIMIT = 56 * 1024 * 1024

NT_DIMS = (((1,), (1,)), ((), ()))
TN_DIMS = (((0,), (0,)), ((), ()))


def _params(*sem):
    return pltpu.CompilerParams(dimension_semantics=sem, vmem_limit_bytes=VMEM_LIMIT)


def _silu(x):
    return x * (1.0 / (1.0 + jnp.exp(-x)))


def _split3(x):
    p0 = x.astype(BF16)
    r = x - p0.astype(F32)
    p1 = r.astype(BF16)
    p2 = (r - p1.astype(F32)).astype(BF16)
    return p0, p1, p2


def _mm_kernel(x_ref, w_ref, *o_refs):
    acc = jnp.dot(x_ref[...].astype(BF16), w_ref[...], preferred_element_type=F32)
    for o_ref in o_refs:
        o_ref[...] = acc.astype(o_ref.dtype)


def _matmul(x, w, col0, ncols, out_dtypes, tm, tn):
    m, k = x.shape
    tm = math.gcd(m, tm)
    assert ncols % tn == 0 and col0 % tn == 0
    cb0 = col0 // tn
    outs = pl.pallas_call(
        _mm_kernel,
        grid=(ncols // tn, m // tm),
        in_specs=[pl.BlockSpec((tm, k), lambda j, i: (i, 0)),
                  pl.BlockSpec((k, tn), lambda j, i: (0, cb0 + j))],
        out_specs=[pl.BlockSpec((tm, tn), lambda j, i: (i, j)) for _ in out_dtypes],
        out_shape=[jax.ShapeDtypeStruct((m, ncols), dt) for dt in out_dtypes],
        compiler_params=_params("parallel", "arbitrary"),
        name=f"mm_c{col0}_n{ncols}_m{m}",
    )(x, w)
    return outs


def _proj_ln_kernel(*refs, widths):
    n = len(widths)
    o_refs = refs[:n]
    x_ref, w_ref, g_ref, b_ref, y_ref, yb_ref = refs[n:]
    mix = None
    row = 0
    for o_ref, wd in zip(o_refs, widths):
        part = jnp.dot(o_ref[...], w_ref[row:row + wd, :], preferred_element_type=F32)
        mix = part if mix is None else mix + part
        row += wd
    h = DEEPNORM_ALPHA * x_ref[...] + mix
    mu = jnp.mean(h, axis=-1, keepdims=True)
    hc = h - mu
    var = jnp.mean(hc * hc, axis=-1, keepdims=True)
    y = hc * lax.rsqrt(var + LN_EPS) * g_ref[...] + b_ref[...]
    y_ref[...] = y
    yb_ref[...] = y.astype(BF16)


def _proj_ln(o_parts, x, w_out, ln_g, ln_b, tm):
    m = x.shape[0]
    widths = tuple(o.shape[1] for o in o_parts)
    ktot = sum(widths)
    tm = math.gcd(m, tm)
    assert w_out.shape == (ktot, D_MODEL)
    in_specs = [pl.BlockSpec((tm, wd), lambda i: (i, 0)) for wd in widths]
    in_specs += [pl.BlockSpec((tm, D_MODEL), lambda i: (i, 0)),
                 pl.BlockSpec((ktot, D_MODEL), lambda i: (0, 0)),
                 pl.BlockSpec((1, D_MODEL), lambda i: (0, 0)),
                 pl.BlockSpec((1, D_MODEL), lambda i: (0, 0))]
    y, yb = pl.pallas_call(
        functools.partial(_proj_ln_kernel, widths=widths),
        grid=(m // tm,),
        in_specs=in_specs,
        out_specs=[pl.BlockSpec((tm, D_MODEL), lambda i: (i, 0)),
                   pl.BlockSpec((tm, D_MODEL), lambda i: (i, 0))],
        out_shape=[jax.ShapeDtypeStruct((m, D_MODEL), F32),
                   jax.ShapeDtypeStruct((m, D_MODEL), BF16)],
        compiler_params=_params("parallel"),
        name=f"proj_ln_k{ktot}_m{m}",
    )(*o_parts, x, w_out, ln_g.reshape(1, D_MODEL), ln_b.reshape(1, D_MODEL))
    return y, yb


def _head_masks(rows):
    lane = lax.broadcasted_iota(jnp.int32, (rows, LANES), 1)
    return lane < HEAD_DIM


def _sb_block(qh, k2, v2, carry, acc, upper, diag_mask):
    z = lax.dot_general(qh, k2, NT_DIMS, preferred_element_type=F32) * ATTN_SCALE
    lg = jnp.log1p(jnp.exp(-jnp.abs(z)))
    sp = jnp.maximum(z, 0.0) + lg
    logsig = jnp.minimum(z, 0.0) - lg
    if diag_mask is not None:
        sp = jnp.where(diag_mask, sp, 0.0)
    sp_hi = sp.astype(BF16)
    sp_lo = (sp - sp_hi.astype(F32)).astype(BF16)
    later = (jnp.dot(sp_hi, upper, preferred_element_type=F32)
             + jnp.dot(sp_lo, upper, preferred_element_type=F32)) + carry
    w = jnp.exp(logsig - later)
    if diag_mask is not None:
        w = jnp.where(diag_mask, w, 0.0)
    acc = acc + jnp.dot(w.astype(BF16), v2, preferred_element_type=F32)
    carry = carry + jnp.sum(sp, axis=-1, keepdims=True)
    return carry, acc


def _sb_upper():
    r = lax.broadcasted_iota(jnp.int32, (LANES, LANES), 0)
    c = lax.broadcasted_iota(jnp.int32, (LANES, LANES), 1)
    return (r > c).astype(BF16)


def _sb_finish(acc0, acc1, g, o_ref):
    head0 = _head_masks(acc0.shape[0])
    o = jnp.where(head0, acc0, acc1) * _silu(g)
    o_ref[0] = o.astype(o_ref.dtype)


def _sb_prompt_kernel(q_ref, k_ref, v_ref, g_ref, o_ref):
    qi = pl.program_id(2)
    tq = q_ref.shape[1]
    q2 = q_ref[0]
    head0 = _head_masks(tq)
    zero = jnp.zeros_like(q2)
    qh = (jnp.where(head0, q2, zero), jnp.where(head0, zero, q2))
    upper = _sb_upper()
    r = lax.broadcasted_iota(jnp.int32, (tq, LANES), 0)
    c = lax.broadcasted_iota(jnp.int32, (tq, LANES), 1)
    strict = c < r

    def visit(kb, state, mask):
        start = pl.multiple_of(kb * LANES, LANES)
        k2 = k_ref[0, pl.ds(start, LANES), :]
        v2 = v_ref[0, pl.ds(start, LANES), :]
        c0, a0, c1, a1 = state
        c0, a0 = _sb_block(qh[0], k2, v2, c0, a0, upper, mask)
        c1, a1 = _sb_block(qh[1], k2, v2, c1, a1, upper, mask)
        return c0, a0, c1, a1

    zc = jnp.zeros((tq, 1), F32)
    za = jnp.zeros((tq, LANES), F32)
    state = visit(qi, (zc, za, zc, za), strict)
    state = lax.fori_loop(0, qi, lambda t, s: visit(qi - 1 - t, s, None), state)
    _sb_finish(state[1], state[3], g_ref[0], o_ref)


def _sb_prompt(q, k, v, g):
    b, l, wdt = q.shape
    tq = LANES
    blk = pl.BlockSpec((1, tq, LANES), lambda bi, hp, qi: (bi, qi, hp))
    full = pl.BlockSpec((1, l, LANES), lambda bi, hp, qi: (bi, 0, hp))
    return pl.pallas_call(
        _sb_prompt_kernel,
        grid=(b, wdt // LANES, l // tq),
        in_specs=[blk, full, full, blk],
        out_specs=blk,
        out_shape=jax.ShapeDtypeStruct((b, l, wdt), BF16),
        compiler_params=_params("parallel", "parallel", "arbitrary"),
        name="sb_prompt",
    )(q, k, v, g)


def _sb_sample_kernel(q_ref, kn_ref, vn_ref, kc_ref, vc_ref, g_ref, o_ref):
    tq = q_ref.shape[1]
    n_cached = kc_ref.shape[1] // LANES
    q2 = q_ref[0]
    head0 = _head_masks(tq)
    zero = jnp.zeros_like(q2)
    qh = (jnp.where(head0, q2, zero), jnp.where(head0, zero, q2))
    upper = _sb_upper()
    r = lax.broadcasted_iota(jnp.int32, (tq, LANES), 0)
    c = lax.broadcasted_iota(jnp.int32, (tq, LANES), 1)
    strict = c < r
    pad = jnp.zeros((LANES - tq, LANES), BF16)
    kn = jnp.concatenate([kn_ref[0], pad], axis=0)
    vn = jnp.concatenate([vn_ref[0], pad], axis=0)
    zc = jnp.zeros((tq, 1), F32)
    za = jnp.zeros((tq, LANES), F32)
    c0, a0 = _sb_block(qh[0], kn, vn, zc, za, upper, strict)
    c1, a1 = _sb_block(qh[1], kn, vn, zc, za, upper, strict)

    def visit(t, state):
        start = pl.multiple_of((n_cached - 1 - t) * LANES, LANES)
        k2 = kc_ref[0, pl.ds(start, LANES), :].astype(BF16)
        v2 = vc_ref[0, pl.ds(start, LANES), :].astype(BF16)
        c0, a0, c1, a1 = state
        c0, a0 = _sb_block(qh[0], k2, v2, c0, a0, upper, None)
        c1, a1 = _sb_block(qh[1], k2, v2, c1, a1, upper, None)
        return c0, a0, c1, a1

    state = lax.fori_loop(0, n_cached, visit, (c0, a0, c1, a1))
    _sb_finish(state[1], state[3], g_ref[0], o_ref)


def _sb_sample(q, kn, vn, kc, vc, g):
    b, t, wdt = q.shape
    p = kc.shape[1]
    assert p % LANES == 0 and t <= LANES
    new = pl.BlockSpec((1, t, LANES), lambda bi, hp: (bi, 0, hp))
    old = pl.BlockSpec((1, p, LANES), lambda bi, hp: (bi, 0, hp))
    return pl.pallas_call(
        _sb_sample_kernel,
        grid=(b, wdt // LANES),
        in_specs=[new, new, new, old, old, new],
        out_specs=new,
        out_shape=jax.ShapeDtypeStruct((b, t, wdt), BF16),
        compiler_params=_params("parallel", "parallel"),
        name="sb_sample",
    )(q, kn, vn, kc, vc, g)


CONV_PAD = 8
CONV_SLAB = 512


def _ssd_kernel(xbc_ref, z_ref, dt_ref, cs_ref, h0_ref, cw_ref, cb_ref, dtb_ref, alog_ref, drow_ref,
                nw_ref, o_ref, h_ref, xp_ref, xa_ref, y_ref):
    ci = pl.program_id(1)
    q = xbc_ref.shape[1]
    taps = SSD_CONV - 1

    @pl.when(ci == 0)
    def _():
        xp_ref[CONV_PAD - taps:CONV_PAD, :] = cs_ref[0]
        h_ref[0] = h0_ref[0]

    @pl.when(ci > 0)
    def _():
        xp_ref[CONV_PAD - taps:CONV_PAD, :] = xp_ref[CONV_PAD + q - taps:CONV_PAD + q, :]

    xp_ref[CONV_PAD:CONV_PAD + q, :] = xbc_ref[0]
    for cs in range(0, SSD_CONV_DIM, CONV_SLAB):
        acc = cb_ref[:, cs:cs + CONV_SLAB] + cw_ref[0:1, cs:cs + CONV_SLAB] * xp_ref[
            CONV_PAD - taps:CONV_PAD - taps + q, cs:cs + CONV_SLAB]
        for t in range(1, SSD_CONV):
            acc = acc + cw_ref[t:t + 1, cs:cs + CONV_SLAB] * xp_ref[
                CONV_PAD - taps + t:CONV_PAD - taps + t + q, cs:cs + CONV_SLAB]
        xa_ref[:, cs:cs + CONV_SLAB] = _silu(acc)

    dtv = dt_ref[0] + dtb_ref[...]
    dtv = jnp.maximum(dtv, 0.0) + jnp.log1p(jnp.exp(-jnp.abs(dtv)))
    da = dtv * (-jnp.exp(alog_ref[...]))
    r = lax.broadcasted_iota(jnp.int32, (q, q), 0)
    c = lax.broadcasted_iota(jnp.int32, (q, q), 1)
    causal = c <= r
    tril = causal.astype(BF16)
    cum = sum(jnp.dot(tril, p, preferred_element_type=F32) for p in _split3(da))
    er = lax.broadcasted_iota(jnp.int32, (LANES, LANES), 0)
    ec = lax.broadcasted_iota(jnp.int32, (LANES, LANES), 1)
    eye = (er == ec).astype(BF16)

    def transpose(x):
        return sum(lax.dot_general(eye, p, NT_DIMS, preferred_element_type=F32) for p in _split3(x))

    cum_t = transpose(cum)
    dt_t = transpose(dtv)
    cum_last = cum[q - 1:q, :]
    to_end = jnp.exp(cum_last - cum) * dtv
    exp_cum = jnp.exp(cum)
    chunk_decay = jnp.broadcast_to(jnp.exp(cum_last), (LANES, LANES))
    head0 = _head_masks(q)
    top = lax.broadcasted_iota(jnp.int32, (LANES, LANES), 0) < HEAD_DIM

    for g in range(SSD_GROUPS):
        b0 = SSD_WIDTH + g * SSD_STATE
        c0 = SSD_WIDTH + SSD_GROUPS * SSD_STATE + g * SSD_STATE
        bm = xa_ref[:, b0:b0 + SSD_STATE].astype(BF16)
        cm = xa_ref[:, c0:c0 + SSD_STATE].astype(BF16)
        cb = lax.dot_general(cm, bm, NT_DIMS, preferred_element_type=F32)
        for j in range(SSD_HEADS // SSD_GROUPS // 2):
            pr = g * (SSD_HEADS // SSD_GROUPS // 2) + j
            h0, h1 = 2 * pr, 2 * pr + 1
            lo = pr * LANES
            x2 = xa_ref[:, lo:lo + LANES]
            x2b = x2.astype(BF16)
            yd = []
            for h in (h0, h1):
                seg = cum[:, h:h + 1] - cum_t[h:h + 1, :]
                wgt = cb * jnp.exp(jnp.where(causal, seg, -jnp.inf)) * dt_t[h:h + 1, :]
                yd.append(jnp.dot(wgt.astype(BF16), x2b, preferred_element_type=F32))
            y = jnp.where(head0, yd[0], yd[1])
            hs = h_ref[0, lo:lo + LANES, :]
            yoff = lax.dot_general(cm, hs.astype(BF16), NT_DIMS, preferred_element_type=F32)
            y = y + yoff * jnp.where(head0, exp_cum[:, h0:h0 + 1], exp_cum[:, h1:h1 + 1])
            y_ref[:, lo:lo + LANES] = y + drow_ref[:, lo:lo + LANES] * x2
            xw = x2 * jnp.where(head0, to_end[:, h0:h0 + 1], to_end[:, h1:h1 + 1])
            st = lax.dot_general(xw.astype(BF16), bm, TN_DIMS, preferred_element_type=F32)
            dec = jnp.where(top, chunk_decay[:, h0:h0 + 1], chunk_decay[:, h1:h1 + 1])
            h_ref[0, lo:lo + LANES, :] = dec * hs + st

    gw = SSD_WIDTH // SSD_GROUPS
    for g in range(SSD_GROUPS):
        yg = y_ref[:, g * gw:(g + 1) * gw] * _silu(z_ref[0, :, g * gw:(g + 1) * gw])
        ms = jnp.mean(yg * yg, axis=-1, keepdims=True)
        o_ref[0, :, g * gw:(g + 1) * gw] = (yg * lax.rsqrt(ms + RMS_EPS)
                                            * nw_ref[:, g * gw:(g + 1) * gw]).astype(o_ref.dtype)


def _ssd(xbc, z, dt_raw, conv_state, h0, conv_w, conv_b, dt_bias, a_log, d_skip, norm_w, q):
    b, l, _ = xbc.shape
    assert l % q == 0
    pad = LANES - SSD_HEADS
    dtb = jnp.pad(dt_bias, (0, pad)).reshape(1, LANES)
    alog = jnp.pad(a_log, (0, pad)).reshape(1, LANES)
    drow = jnp.repeat(d_skip, HEAD_DIM).reshape(1, SSD_WIDTH)

    def tok(wd):
        return pl.BlockSpec((1, q, wd), lambda bi, ci: (bi, ci, 0))

    def per_seq(shape):
        return pl.BlockSpec((1,) + shape, lambda bi, ci: (bi, 0, 0))

    def const(shape):
        return pl.BlockSpec(shape, lambda bi, ci: (0, 0))

    o, h = pl.pallas_call(
        _ssd_kernel,
        grid=(b, l // q),
        in_specs=[tok(SSD_CONV_DIM), tok(SSD_WIDTH), tok(LANES),
                  per_seq((SSD_CONV - 1, SSD_CONV_DIM)), per_seq((SSD_WIDTH, SSD_STATE)),
                  const((SSD_CONV, SSD_CONV_DIM)), const((1, SSD_CONV_DIM)), const((1, LANES)),
                  const((1, LANES)), const((1, SSD_WIDTH)), const((1, SSD_WIDTH))],
        out_specs=[tok(SSD_WIDTH), per_seq((SSD_WIDTH, SSD_STATE))],
        out_shape=[jax.ShapeDtypeStruct((b, l, SSD_WIDTH), BF16),
                   jax.ShapeDtypeStruct((b, SSD_WIDTH, SSD_STATE), F32)],
        scratch_shapes=[pltpu.VMEM((CONV_PAD + q, SSD_CONV_DIM), F32),
                        pltpu.VMEM((q, SSD_CONV_DIM), F32),
                        pltpu.VMEM((q, SSD_WIDTH), F32)],
        compiler_params=_params("parallel", "arbitrary"),
        name=f"ssd_q{q}",
    )(xbc, z, dt_raw, conv_state, h0, conv_w, conv_b.reshape(1, SSD_CONV_DIM), dtb, alog, drow,
      norm_w.reshape(1, SSD_WIDTH))
    return o, h


BAND_BLOCKS = CA_LEFT // LANES + 1


def _band_softmax_pv(scores, values, g, o_ref):
    outs = []
    for s_list in scores:
        m = functools.reduce(jnp.maximum, [jnp.max(s, axis=-1, keepdims=True) for s in s_list])
        es = [jnp.exp(s - m) for s in s_list]
        den = functools.reduce(lambda a, b: a + b, [jnp.sum(e, axis=-1, keepdims=True) for e in es])
        num = functools.reduce(lambda a, b: a + b,
                               [jnp.dot(e.astype(BF16), v, preferred_element_type=F32)
                                for e, v in zip(es, values)])
        outs.append(num / den)
    head0 = _head_masks(outs[0].shape[0])
    o_ref[0] = (jnp.where(head0, outs[0], outs[1]) * _silu(g)).astype(o_ref.dtype)


def _band_prompt_kernel(q_ref, k_ref, v_ref, g_ref, bias_ref, o_ref):
    i = pl.program_id(2)
    q2 = q_ref[0]
    head0 = _head_masks(q2.shape[0])
    zero = jnp.zeros_like(q2)
    qh = (jnp.where(head0, q2, zero), jnp.where(head0, zero, q2))
    scores = ([], [])
    values = []
    for d in range(BAND_BLOCKS):
        jb = i - d
        ok = jb >= 0
        start = pl.multiple_of(jnp.maximum(jb, 0) * LANES, LANES)
        k2 = k_ref[0, pl.ds(start, LANES), :]
        values.append(v_ref[0, pl.ds(start, LANES), :])
        tile = min(d, 2) if d < BAND_BLOCKS - 1 else 3
        for h in range(2):
            s = lax.dot_general(qh[h], k2, NT_DIMS, preferred_element_type=F32) * ATTN_SCALE
            s = s + bias_ref[h, tile]
            scores[h].append(jnp.where(ok, s, NEG_BIG))
    _band_softmax_pv(scores, values, g_ref[0], o_ref)


def _band_bias_tiles(rel_bias):
    r = jnp.arange(LANES)[:, None]
    u = jnp.arange(LANES)[None, :]

    def tile(dist):
        rel = jnp.clip(dist * LANES + r - u, -CA_REL_CLIP, CA_REL_CLIP) + CA_REL_CLIP
        return rel_bias[:, rel]

    t0 = jnp.where((r < CHUNK) & (u >= CHUNK), NEG_BIG, tile(0))
    t1 = tile(1)
    t2 = tile(2)
    t4 = jnp.where((r >= CHUNK) & (u < CHUNK), NEG_BIG, tile(4))
    return jnp.stack([t0, t1, t2, t4], axis=1).astype(F32)


def _band_prompt(q, k, v, g, bias_tiles):
    b, l, wdt = q.shape
    blk = pl.BlockSpec((1, LANES, LANES), lambda bi, hp, i: (bi, i, hp))
    full = pl.BlockSpec((1, l, LANES), lambda bi, hp, i: (bi, 0, hp))
    bias = pl.BlockSpec((2, 4, LANES, LANES), lambda bi, hp, i: (hp, 0, 0, 0))
    return pl.pallas_call(
        _band_prompt_kernel,
        grid=(b, wdt // LANES, l // LANES),
        in_specs=[blk, full, full, blk, bias],
        out_specs=blk,
        out_shape=jax.ShapeDtypeStruct((b, l, wdt), BF16),
        compiler_params=_params("parallel", "parallel", "arbitrary"),
        name="band_prompt",
    )(q, k, v, g, bias_tiles)


def _band_sample_kernel(q_ref, kn_ref, vn_ref, kc_ref, vc_ref, g_ref, bias_ref, o_ref):
    tq = q_ref.shape[1]
    q2 = q_ref[0]
    head0 = _head_masks(tq)
    zero = jnp.zeros_like(q2)
    qh = (jnp.where(head0, q2, zero), jnp.where(head0, zero, q2))
    pad = jnp.zeros((LANES - tq, LANES), BF16)
    n_cached = kc_ref.shape[1] // LANES
    scores = ([], [])
    values = []
    for d in range(n_cached + 1):
        if d < n_cached:
            k2 = kc_ref[0, d * LANES:(d + 1) * LANES, :].astype(BF16)
            values.append(vc_ref[0, d * LANES:(d + 1) * LANES, :].astype(BF16))
        else:
            k2 = jnp.concatenate([kn_ref[0], pad], axis=0)
            values.append(jnp.concatenate([vn_ref[0], pad], axis=0))
        for h in range(2):
            s = lax.dot_general(qh[h], k2, NT_DIMS, preferred_element_type=F32) * ATTN_SCALE
            scores[h].append(s + bias_ref[h, :, d * LANES:(d + 1) * LANES])
    _band_softmax_pv(scores, values, g_ref[0], o_ref)


def _band_sample_bias(rel_bias, tq):
    qi = jnp.arange(tq)[:, None]
    kj = jnp.arange(CA_LEFT + LANES)[None, :]
    rel = jnp.clip(qi - kj + CA_LEFT, -CA_REL_CLIP, CA_REL_CLIP) + CA_REL_CLIP
    return jnp.where(kj < CA_LEFT + tq, rel_bias[:, rel], NEG_BIG).astype(F32)


def _band_sample(q, kn, vn, kc, vc, g, bias):
    b, t, wdt = q.shape
    p = kc.shape[1]
    assert p == CA_LEFT and t == CHUNK
    new = pl.BlockSpec((1, t, LANES), lambda bi, hp: (bi, 0, hp))
    old = pl.BlockSpec((1, p, LANES), lambda bi, hp: (bi, 0, hp))
    bspec = pl.BlockSpec((2, t, CA_LEFT + LANES), lambda bi, hp: (hp, 0, 0))
    return pl.pallas_call(
        _band_sample_kernel,
        grid=(b, wdt // LANES),
        in_specs=[new, new, new, old, old, new, bspec],
        out_specs=new,
        out_shape=jax.ShapeDtypeStruct((b, t, wdt), BF16),
        compiler_params=_params("parallel", "parallel"),
        name="band_sample",
    )(q, kn, vn, kc, vc, g, bias)


def _even_layer(x, xb, past_k, past_v, conv_state, h0, wts, ssd_chunk, tm):
    w_in, w_dt, conv_w, conv_b, dt_bias, a_log, d_skip, norm_w, w_out, ln_g, ln_b = wts
    b, l, _ = x.shape
    m = b * l
    x2 = x.reshape(m, D_MODEL)
    xb2 = xb.reshape(m, D_MODEL)
    tn = 1024
    (qb,) = _matmul(xb2, w_in, L0_Q, SB_WIDTH, (BF16,), tm, tn)
    k, kb = _matmul(xb2, w_in, L0_K, SB_WIDTH, (F32, BF16), tm, tn)
    v, vb = _matmul(xb2, w_in, L0_V, SB_WIDTH, (F32, BF16), tm, tn)
    (ga,) = _matmul(xb2, w_in, L0_G, SB_WIDTH, (F32,), tm, tn)
    (z,) = _matmul(xb2, w_in, L0_Z, SSD_WIDTH, (F32,), tm, tn)
    (xbc,) = _matmul(xb2, w_in, L0_XBC, SSD_CONV_DIM, (F32,), tm, tn)
    (dt_raw,) = _matmul(xb2, w_dt, 0, LANES, (F32,), tm, LANES)

    def seq(a):
        return a.reshape(b, l, a.shape[-1])

    if past_k is None:
        o_a = _sb_prompt(seq(qb), seq(kb), seq(vb), seq(ga))
    else:
        o_a = _sb_sample(seq(qb), seq(kb), seq(vb), past_k, past_v, seq(ga))
    xbc3 = seq(xbc)
    o_b, h_new = _ssd(xbc3, seq(z), seq(dt_raw), conv_state, h0, conv_w, conv_b, dt_bias, a_log,
                      d_skip, norm_w, ssd_chunk)
    y, yb = _proj_ln([o_a.reshape(m, SB_WIDTH), o_b.reshape(m, SSD_WIDTH)], x2, w_out, ln_g, ln_b,
                     min(tm, 256))
    tail = jnp.concatenate([conv_state, xbc3], axis=1)[:, -(SSD_CONV - 1):]
    return (seq(y), seq(yb), k.reshape(b, l, 16, HEAD_DIM), v.reshape(b, l, 16, HEAD_DIM), tail,
            h_new.reshape(b, SSD_HEADS, HEAD_DIM, SSD_STATE))


def _odd_layer(x, xb, past_k, past_v, wts, tm):
    w_in, bias, w_out, ln_g, ln_b = wts
    b, l, _ = x.shape
    m = b * l
    x2 = x.reshape(m, D_MODEL)
    xb2 = xb.reshape(m, D_MODEL)
    tn = 1024
    (qb,) = _matmul(xb2, w_in, 0, CA_WIDTH, (BF16,), tm, tn)
    k, kb = _matmul(xb2, w_in, CA_WIDTH, CA_WIDTH, (F32, BF16), tm, tn)
    v, vb = _matmul(xb2, w_in, 2 * CA_WIDTH, CA_WIDTH, (F32, BF16), tm, tn)
    (g,) = _matmul(xb2, w_in, 3 * CA_WIDTH, CA_WIDTH, (F32,), tm, tn)

    def seq(a):
        return a.reshape(b, l, a.shape[-1])

    if past_k is None:
        o = _band_prompt(seq(qb), seq(kb), seq(vb), seq(g), bias)
        k_cat, v_cat = seq(k), seq(v)
    else:
        o = _band_sample(seq(qb), seq(kb), seq(vb), past_k, past_v, seq(g), bias)
        k_cat = jnp.concatenate([past_k, seq(k)], axis=1)
        v_cat = jnp.concatenate([past_v, seq(v)], axis=1)
    y, _ = _proj_ln([o.reshape(m, CA_WIDTH)], x2, w_out, ln_g, ln_b, min(tm, 256))
    keep = min(CA_LEFT, k_cat.shape[1])
    return (seq(y), k_cat[:, -keep:].reshape(b, keep, CA_HEADS, HEAD_DIM),
            v_cat[:, -keep:].reshape(b, keep, CA_HEADS, HEAD_DIM))


def kernel(x_prompt, x_sample, cache_sb_k, cache_sb_v, state_ssm, state_conv, cache_band_k,
           cache_band_v, even_w_in, even_conv_w, even_conv_b, even_dt_bias, even_a_log, even_d_skip,
           even_norm_w, even_w_out, even_ln_g, even_ln_b, odd_w_in, odd_rel_bias, odd_w_out,
           odd_ln_g, odd_ln_b):
    bp, lp, _ = x_prompt.shape
    bs, ls, _ = x_sample.shape
    past = cache_sb_k.shape[2]
    w_in0 = even_w_in[0].astype(BF16)
    w_dt0 = jnp.pad(w_in0[:, L0_DT:], ((0, 0), (0, LANES - SSD_HEADS)))
    wts0 = (w_in0, w_dt0, even_conv_w[0], even_conv_b[0], even_dt_bias[0], even_a_log[0],
            even_d_skip[0], even_norm_w[0], even_w_out[0].astype(BF16), even_ln_g[0], even_ln_b[0])
    zero_conv = jnp.zeros((bp, SSD_CONV - 1, SSD_CONV_DIM), F32)
    zero_h = jnp.zeros((bp, SSD_WIDTH, SSD_STATE), F32)
    yp, ypb, p_sb_k, p_sb_v, p_conv, p_ssm = _even_layer(
        x_prompt, x_prompt.astype(BF16), None, None, zero_conv, zero_h, wts0, 2 * CHUNK, 512)
    ys, ysb, s_sb_k, s_sb_v, s_conv, s_ssm = _even_layer(
        x_sample, x_sample.astype(BF16), cache_sb_k[0].reshape(bs, past, SB_WIDTH),
        cache_sb_v[0].reshape(bs, past, SB_WIDTH), state_conv[0],
        state_ssm[0].reshape(bs, SSD_WIDTH, SSD_STATE), wts0, CHUNK, 512)
    w_in1 = odd_w_in[0].astype(BF16)
    w_out1 = odd_w_out[0].astype(BF16)
    wts1p = (w_in1, _band_bias_tiles(odd_rel_bias[0]), w_out1, odd_ln_g[0], odd_ln_b[0])
    wts1s = (w_in1, _band_sample_bias(odd_rel_bias[0], ls), w_out1, odd_ln_g[0], odd_ln_b[0])
    yp, p_band_k, p_band_v = _odd_layer(yp, ypb, None, None, wts1p, 512)
    band = cache_band_k.shape[2]
    ys, s_band_k, s_band_v = _odd_layer(
        ys, ysb, cache_band_k[0].reshape(bs, band, CA_WIDTH), cache_band_v[0].reshape(bs, band, CA_WIDTH),
        wts1s, 512)

    def lead(a):
        return a[None]

    return (yp, ys, lead(p_sb_k), lead(p_sb_v), lead(p_ssm), lead(p_conv), lead(p_band_k),
            lead(p_band_v), lead(s_sb_k), lead(s_sb_v), lead(s_ssm), lead(s_conv), lead(s_band_k),
            lead(s_band_v))
```

```python
import functools
import math

import jax
import jax.numpy as jnp
from jax import lax
from jax.experimental import pallas as pl
from jax.experimental.pallas import tpu as pltpu

F32 = jnp.float32
BF16 = jnp.bfloat16

D_MODEL = 2048
CHUNK = 64
HEAD_DIM = 64
LANES = 128
SB_WIDTH = 1024
SSD_WIDTH = 2048
SSD_HEADS = 32
SSD_GROUPS = 4
SSD_STATE = 128
SSD_CONV = 4
SSD_CONV_DIM = SSD_WIDTH + 2 * SSD_GROUPS * SSD_STATE
CA_WIDTH = 2048
CA_HEADS = 32
CA_LEFT = 8 * CHUNK
CA_REL_CLIP = 128
DEPTH = 2
DEEPNORM_ALPHA = (2 * DEPTH) ** 0.25
LN_EPS = 1e-5
RMS_EPS = 1e-5
ATTN_SCALE = HEAD_DIM ** -0.5
NEG_BIG = -1e30
L0_Q, L0_K, L0_V, L0_G, L0_Z, L0_XBC, L0_DT = 0, 1024, 2048, 3072, 4096, 6144, 9216
VMEM_LIMIT = 56 * 1024 * 1024

NT_DIMS = (((1,), (1,)), ((), ()))
TN_DIMS = (((0,), (0,)), ((), ()))


def _params(*sem):
    return pltpu.CompilerParams(dimension_semantics=sem, vmem_limit_bytes=VMEM_LIMIT)


def _silu(x):
    return x * (1.0 / (1.0 + jnp.exp(-x)))


def _split3(x):
    p0 = x.astype(BF16)
    r = x - p0.astype(F32)
    p1 = r.astype(BF16)
    p2 = (r - p1.astype(F32)).astype(BF16)
    return p0, p1, p2


def _mm_kernel(x_ref, w_ref, *refs, has_side):
    x = x_ref[...]
    o_refs = refs
    if has_side:
        ws_ref, *o_refs, os_ref = refs
        os_ref[...] = jnp.dot(x, ws_ref[...], preferred_element_type=F32)
    acc = jnp.dot(x, w_ref[...], preferred_element_type=F32)
    for o_ref in o_refs:
        o_ref[...] = acc.astype(o_ref.dtype)


def _matmul(x, w, col0, ncols, out_dtypes, tm, tn, w_side=None):
    m, k = x.shape
    tm = math.gcd(m, tm)
    assert ncols % tn == 0 and col0 % tn == 0
    cb0 = col0 // tn
    in_specs = [pl.BlockSpec((tm, k), lambda j, i: (i, 0)),
                pl.BlockSpec((k, tn), lambda j, i: (0, cb0 + j))]
    out_specs = [pl.BlockSpec((tm, tn), lambda j, i: (i, j)) for _ in out_dtypes]
    out_shape = [jax.ShapeDtypeStruct((m, ncols), dt) for dt in out_dtypes]
    args = [x, w]
    if w_side is not None:
        assert ncols == tn
        ns = w_side.shape[1]
        in_specs.append(pl.BlockSpec((k, ns), lambda j, i: (0, 0)))
        out_specs.append(pl.BlockSpec((tm, ns), lambda j, i: (i, 0)))
        out_shape.append(jax.ShapeDtypeStruct((m, ns), F32))
        args.append(w_side)
    outs = pl.pallas_call(
        functools.partial(_mm_kernel, has_side=w_side is not None),
        grid=(ncols // tn, m // tm),
        in_specs=in_specs,
        out_specs=out_specs,
        out_shape=out_shape,
        compiler_params=_params("parallel", "arbitrary"),
        name=f"mm_c{col0}_n{ncols}_m{m}",
    )(*args)
    return outs


def _proj_ln_kernel(*refs, widths):
    n = len(widths)
    o_refs = refs[:n]
    x_ref, w_ref, g_ref, b_ref, y_ref, yb_ref = refs[n:]
    mix = None
    row = 0
    for o_ref, wd in zip(o_refs, widths):
        part = jnp.dot(o_ref[...], w_ref[row:row + wd, :], preferred_element_type=F32)
        mix = part if mix is None else mix + part
        row += wd
    h = DEEPNORM_ALPHA * x_ref[...] + mix
    mu = jnp.mean(h, axis=-1, keepdims=True)
    hc = h - mu
    var = jnp.mean(hc * hc, axis=-1, keepdims=True)
    y = hc * lax.rsqrt(var + LN_EPS) * g_ref[...] + b_ref[...]
    y_ref[...] = y
    yb_ref[...] = y.astype(BF16)


def _proj_ln(o_parts, x, w_out, ln_g, ln_b, tm):
    m = x.shape[0]
    widths = tuple(o.shape[1] for o in o_parts)
    ktot = sum(widths)
    tm = math.gcd(m, tm)
    assert w_out.shape == (ktot, D_MODEL)
    in_specs = [pl.BlockSpec((tm, wd), lambda i: (i, 0)) for wd in widths]
    in_specs += [pl.BlockSpec((tm, D_MODEL), lambda i: (i, 0)),
                 pl.BlockSpec((ktot, D_MODEL), lambda i: (0, 0)),
                 pl.BlockSpec((1, D_MODEL), lambda i: (0, 0)),
                 pl.BlockSpec((1, D_MODEL), lambda i: (0, 0))]
    y, yb = pl.pallas_call(
        functools.partial(_proj_ln_kernel, widths=widths),
        grid=(m // tm,),
        in_specs=in_specs,
        out_specs=[pl.BlockSpec((tm, D_MODEL), lambda i: (i, 0)),
                   pl.BlockSpec((tm, D_MODEL), lambda i: (i, 0))],
        out_shape=[jax.ShapeDtypeStruct((m, D_MODEL), F32),
                   jax.ShapeDtypeStruct((m, D_MODEL), BF16)],
        compiler_params=_params("parallel"),
        name=f"proj_ln_k{ktot}_m{m}",
    )(*o_parts, x, w_out, ln_g.reshape(1, D_MODEL), ln_b.reshape(1, D_MODEL))
    return y, yb


def _head_masks(rows):
    lane = lax.broadcasted_iota(jnp.int32, (rows, LANES), 1)
    return lane < HEAD_DIM


SB_KEYS = 256
SB_DEAD = 104.5


def _pair_queries(q2):
    head0 = _head_masks(q2.shape[0])
    qs = q2 * jnp.asarray(ATTN_SCALE, q2.dtype)
    zero = jnp.zeros_like(qs)
    return jnp.where(head0, qs, zero), jnp.where(head0, zero, qs)


def _sb_block(qh, k2, v2, carry, acc, upper, diag_mask):
    z = lax.dot_general(qh, k2, NT_DIMS, preferred_element_type=F32)
    lg = jnp.log1p(jnp.exp(-jnp.abs(z)))
    sp = jnp.maximum(z, 0.0) + lg
    logsig = jnp.minimum(z, 0.0) - lg
    if diag_mask is not None:
        sp = jnp.where(diag_mask, sp, 0.0)
    sp_hi = sp.astype(BF16)
    sp_lo = (sp - sp_hi.astype(F32)).astype(BF16)
    later = (jnp.dot(sp_hi, upper, preferred_element_type=F32)
             + jnp.dot(sp_lo, upper, preferred_element_type=F32)) + carry
    w = jnp.exp(logsig - later)
    if diag_mask is not None:
        w = jnp.where(diag_mask, w, 0.0)
    acc = acc + jnp.dot(w.astype(BF16), v2, preferred_element_type=F32)
    carry = carry + jnp.sum(sp, axis=-1, keepdims=True)
    return carry, acc


def _sb_upper(n):
    r = lax.broadcasted_iota(jnp.int32, (n, n), 0)
    c = lax.broadcasted_iota(jnp.int32, (n, n), 1)
    return (r > c).astype(BF16)


def _sb_alive(c0, c1):
    return jnp.min(jnp.minimum(c0, c1)) < SB_DEAD


def _sb_finish(acc0, acc1, g, o_ref):
    head0 = _head_masks(acc0.shape[0])
    o = jnp.where(head0, acc0, acc1) * _silu(g)
    o_ref[0] = o.astype(o_ref.dtype)


def _sb_prompt_kernel(q_ref, k_ref, v_ref, g_ref, o_ref):
    qi = pl.program_id(2)
    tq = q_ref.shape[1]
    assert tq == SB_KEYS
    qh = _pair_queries(q_ref[0])
    upper = _sb_upper(SB_KEYS)
    r = lax.broadcasted_iota(jnp.int32, (tq, SB_KEYS), 0)
    c = lax.broadcasted_iota(jnp.int32, (tq, SB_KEYS), 1)
    strict = c < r

    def visit(kb, c0, a0, c1, a1, mask):
        start = pl.multiple_of(kb * SB_KEYS, SB_KEYS)
        k2 = k_ref[0, pl.ds(start, SB_KEYS), :]
        v2 = v_ref[0, pl.ds(start, SB_KEYS), :]
        c0, a0 = _sb_block(qh[0], k2, v2, c0, a0, upper, mask)
        c1, a1 = _sb_block(qh[1], k2, v2, c1, a1, upper, mask)
        return c0, a0, c1, a1

    zc = jnp.zeros((tq, 1), F32)
    za = jnp.zeros((tq, LANES), F32)
    c0, a0, c1, a1 = visit(qi, zc, za, zc, za, strict)

    def older(state):
        kb, _, c0, a0, c1, a1 = state
        c0, a0, c1, a1 = visit(kb, c0, a0, c1, a1, None)
        return kb - 1, _sb_alive(c0, c1), c0, a0, c1, a1

    state = lax.while_loop(lambda s: jnp.logical_and(s[0] >= 0, s[1]), older,
                           (qi - 1, _sb_alive(c0, c1), c0, a0, c1, a1))
    _sb_finish(state[3], state[5], g_ref[0], o_ref)


def _sb_prompt(q, k, v, g):
    b, l, wdt = q.shape
    tq = SB_KEYS
    assert l % tq == 0
    blk = pl.BlockSpec((1, tq, LANES), lambda bi, hp, qi: (bi, qi, hp))
    full = pl.BlockSpec((1, l, LANES), lambda bi, hp, qi: (bi, 0, hp))
    return pl.pallas_call(
        _sb_prompt_kernel,
        grid=(b, wdt // LANES, l // tq),
        in_specs=[blk, full, full, blk],
        out_specs=blk,
        out_shape=jax.ShapeDtypeStruct((b, l, wdt), BF16),
        compiler_params=_params("parallel", "parallel", "arbitrary"),
        name="sb_prompt",
    )(q, k, v, g)


def _sb_sample_kernel(q_ref, kn_ref, vn_ref, kc_ref, vc_ref, g_ref, o_ref):
    tq = q_ref.shape[1]
    n_cached = kc_ref.shape[1] // SB_KEYS
    qh = _pair_queries(q_ref[0])
    upper = _sb_upper(SB_KEYS)
    r = lax.broadcasted_iota(jnp.int32, (tq, LANES), 0)
    c = lax.broadcasted_iota(jnp.int32, (tq, LANES), 1)
    strict = c < r
    pad = jnp.zeros((LANES - tq, LANES), BF16)
    kn = jnp.concatenate([kn_ref[0], pad], axis=0)
    vn = jnp.concatenate([vn_ref[0], pad], axis=0)
    zc = jnp.zeros((tq, 1), F32)
    za = jnp.zeros((tq, LANES), F32)
    upper_new = _sb_upper(LANES)
    c0, a0 = _sb_block(qh[0], kn, vn, zc, za, upper_new, strict)
    c1, a1 = _sb_block(qh[1], kn, vn, zc, za, upper_new, strict)

    def older(state):
        kb, _, c0, a0, c1, a1 = state
        start = pl.multiple_of(kb * SB_KEYS, SB_KEYS)
        k2 = kc_ref[0, pl.ds(start, SB_KEYS), :].astype(BF16)
        v2 = vc_ref[0, pl.ds(start, SB_KEYS), :].astype(BF16)
        c0, a0 = _sb_block(qh[0], k2, v2, c0, a0, upper, None)
        c1, a1 = _sb_block(qh[1], k2, v2, c1, a1, upper, None)
        return kb - 1, _sb_alive(c0, c1), c0, a0, c1, a1

    state = lax.while_loop(lambda s: jnp.logical_and(s[0] >= 0, s[1]), older,
                           (jnp.int32(n_cached - 1), _sb_alive(c0, c1), c0, a0, c1, a1))
    _sb_finish(state[3], state[5], g_ref[0], o_ref)


def _sb_sample(q, kn, vn, kc, vc, g):
    b, t, wdt = q.shape
    p = kc.shape[1]
    assert p % SB_KEYS == 0 and t <= LANES
    new = pl.BlockSpec((1, t, LANES), lambda bi, hp: (bi, 0, hp))
    old = pl.BlockSpec((1, p, LANES), lambda bi, hp: (bi, 0, hp))
    return pl.pallas_call(
        _sb_sample_kernel,
        grid=(b, wdt // LANES),
        in_specs=[new, new, new, old, old, new],
        out_specs=new,
        out_shape=jax.ShapeDtypeStruct((b, t, wdt), BF16),
        compiler_params=_params("parallel", "parallel"),
        name="sb_sample",
    )(q, kn, vn, kc, vc, g)


CONV_PAD = 8
CONV_SLAB = 512


def _ssd_kernel(xbc_ref, z_ref, dt_ref, cs_ref, h0_ref, cw_ref, cb_ref, dtb_ref, alog_ref, drow_ref,
                nw_ref, o_ref, h_ref, xp_ref, xa_ref, y_ref):
    ci = pl.program_id(1)
    q = xbc_ref.shape[1]
    taps = SSD_CONV - 1

    @pl.when(ci == 0)
    def _():
        xp_ref[CONV_PAD - taps:CONV_PAD, :] = cs_ref[0]
        h_ref[0] = h0_ref[0]

    @pl.when(ci > 0)
    def _():
        xp_ref[CONV_PAD - taps:CONV_PAD, :] = xp_ref[CONV_PAD + q - taps:CONV_PAD + q, :]

    xp_ref[CONV_PAD:CONV_PAD + q, :] = xbc_ref[0]
    for cs in range(0, SSD_CONV_DIM, CONV_SLAB):
        acc = cb_ref[:, cs:cs + CONV_SLAB] + cw_ref[0:1, cs:cs + CONV_SLAB] * xp_ref[
            CONV_PAD - taps:CONV_PAD - taps + q, cs:cs + CONV_SLAB]
        for t in range(1, SSD_CONV):
            acc = acc + cw_ref[t:t + 1, cs:cs + CONV_SLAB] * xp_ref[
                CONV_PAD - taps + t:CONV_PAD - taps + t + q, cs:cs + CONV_SLAB]
        xa_ref[:, cs:cs + CONV_SLAB] = _silu(acc)

    dtv = dt_ref[0] + dtb_ref[...]
    dtv = jnp.maximum(dtv, 0.0) + jnp.log1p(jnp.exp(-jnp.abs(dtv)))
    da = dtv * (-jnp.exp(alog_ref[...]))
    r = lax.broadcasted_iota(jnp.int32, (q, q), 0)
    c = lax.broadcasted_iota(jnp.int32, (q, q), 1)
    causal = c <= r
    tril = causal.astype(BF16)
    cum = sum(jnp.dot(tril, p, preferred_element_type=F32) for p in _split3(da))
    er = lax.broadcasted_iota(jnp.int32, (LANES, LANES), 0)
    ec = lax.broadcasted_iota(jnp.int32, (LANES, LANES), 1)
    eye = (er == ec).astype(BF16)

    def transpose(x):
        return sum(lax.dot_general(eye, p, NT_DIMS, preferred_element_type=F32) for p in _split3(x))

    cum_t = transpose(cum)
    dt_t = transpose(dtv)
    cum_last = cum[q - 1:q, :]
    to_end = jnp.exp(cum_last - cum) * dtv
    exp_cum = jnp.exp(cum)
    chunk_decay = jnp.broadcast_to(jnp.exp(cum_last), (LANES, LANES))
    head0 = _head_masks(q)
    top = lax.broadcasted_iota(jnp.int32, (LANES, LANES), 0) < HEAD_DIM

    for g in range(SSD_GROUPS):
        b0 = SSD_WIDTH + g * SSD_STATE
        c0 = SSD_WIDTH + SSD_GROUPS * SSD_STATE + g * SSD_STATE
        bm = xa_ref[:, b0:b0 + SSD_STATE].astype(BF16)
        cm = xa_ref[:, c0:c0 + SSD_STATE].astype(BF16)
        cb = lax.dot_general(cm, bm, NT_DIMS, preferred_element_type=F32)
        for j in range(SSD_HEADS // SSD_GROUPS // 2):
            pr = g * (SSD_HEADS // SSD_GROUPS // 2) + j
            h0, h1 = 2 * pr, 2 * pr + 1
            lo = pr * LANES
            x2 = xa_ref[:, lo:lo + LANES]
            x2b = x2.astype(BF16)
            yd = []
            for h in (h0, h1):
                seg = cum[:, h:h + 1] - cum_t[h:h + 1, :]
                wgt = cb * jnp.exp(jnp.where(causal, seg, -jnp.inf)) * dt_t[h:h + 1, :]
                yd.append(jnp.dot(wgt.astype(BF16), x2b, preferred_element_type=F32))
            y = jnp.where(head0, yd[0], yd[1])
            hs = h_ref[0, lo:lo + LANES, :]
            yoff = lax.dot_general(cm, hs.astype(BF16), NT_DIMS, preferred_element_type=F32)
            y = y + yoff * jnp.where(head0, exp_cum[:, h0:h0 + 1], exp_cum[:, h1:h1 + 1])
            y_ref[:, lo:lo + LANES] = y + drow_ref[:, lo:lo + LANES] * x2
            xw = x2 * jnp.where(head0, to_end[:, h0:h0 + 1], to_end[:, h1:h1 + 1])
            st = lax.dot_general(xw.astype(BF16), bm, TN_DIMS, preferred_element_type=F32)
            dec = jnp.where(top, chunk_decay[:, h0:h0 + 1], chunk_decay[:, h1:h1 + 1])
            h_ref[0, lo:lo + LANES, :] = dec * hs + st

    gw = SSD_WIDTH // SSD_GROUPS
    for g in range(SSD_GROUPS):
        yg = y_ref[:, g * gw:(g + 1) * gw] * _silu(z_ref[0, :, g * gw:(g + 1) * gw])
        ms = jnp.mean(yg * yg, axis=-1, keepdims=True)
        o_ref[0, :, g * gw:(g + 1) * gw] = (yg * lax.rsqrt(ms + RMS_EPS)
                                            * nw_ref[:, g * gw:(g + 1) * gw]).astype(o_ref.dtype)


def _ssd(xbc, z, dt_raw, conv_state, h0, conv_w, conv_b, dt_bias, a_log, d_skip, norm_w, q):
    b, l, _ = xbc.shape
    assert l % q == 0
    pad = LANES - SSD_HEADS
    dtb = jnp.pad(dt_bias, (0, pad)).reshape(1, LANES)
    alog = jnp.pad(a_log, (0, pad)).reshape(1, LANES)
    drow = jnp.repeat(d_skip, HEAD_DIM).reshape(1, SSD_WIDTH)

    def tok(wd):
        return pl.BlockSpec((1, q, wd), lambda bi, ci: (bi, ci, 0))

    def per_seq(shape):
        return pl.BlockSpec((1,) + shape, lambda bi, ci: (bi, 0, 0))

    def const(shape):
        return pl.BlockSpec(shape, lambda bi, ci: (0, 0))

    o, h = pl.pallas_call(
        _ssd_kernel,
        grid=(b, l // q),
        in_specs=[tok(SSD_CONV_DIM), tok(SSD_WIDTH), tok(LANES),
                  per_seq((SSD_CONV - 1, SSD_CONV_DIM)), per_seq((SSD_WIDTH, SSD_STATE)),
                  const((SSD_CONV, SSD_CONV_DIM)), const((1, SSD_CONV_DIM)), const((1, LANES)),
                  const((1, LANES)), const((1, SSD_WIDTH)), const((1, SSD_WIDTH))],
        out_specs=[tok(SSD_WIDTH), per_seq((SSD_WIDTH, SSD_STATE))],
        out_shape=[jax.ShapeDtypeStruct((b, l, SSD_WIDTH), BF16),
                   jax.ShapeDtypeStruct((b, SSD_WIDTH, SSD_STATE), F32)],
        scratch_shapes=[pltpu.VMEM((CONV_PAD + q, SSD_CONV_DIM), F32),
                        pltpu.VMEM((q, SSD_CONV_DIM), F32),
                        pltpu.VMEM((q, SSD_WIDTH), F32)],
        compiler_params=_params("parallel", "arbitrary"),
        name=f"ssd_q{q}",
    )(xbc, z, dt_raw, conv_state, h0, conv_w, conv_b.reshape(1, SSD_CONV_DIM), dtb, alog, drow,
      norm_w.reshape(1, SSD_WIDTH))
    return o, h


BAND_BLOCKS = CA_LEFT // LANES + 1


def _band_softmax_pv(scores, values, g, o_ref):
    outs = []
    for s_list in scores:
        m = jnp.max(functools.reduce(jnp.maximum, s_list), axis=-1, keepdims=True)
        es = [jnp.exp(s - m) for s in s_list]
        den = jnp.sum(functools.reduce(lambda a, b: a + b, es), axis=-1, keepdims=True)
        num = functools.reduce(lambda a, b: a + b,
                               [jnp.dot(e.astype(BF16), v, preferred_element_type=F32)
                                for e, v in zip(es, values)])
        outs.append(num / den)
    head0 = _head_masks(outs[0].shape[0])
    o_ref[0] = (jnp.where(head0, outs[0], outs[1]) * _silu(g)).astype(o_ref.dtype)


def _band_prompt_kernel(q_ref, k_ref, v_ref, g_ref, bias_ref, o_ref):
    i = pl.program_id(2)
    qh = _pair_queries(q_ref[0])

    def tile(at_start):
        scores = ([], [])
        values = []
        for d in range(BAND_BLOCKS):
            jb = i - d
            start = pl.multiple_of(jnp.maximum(jb, 0) * LANES, LANES)
            k2 = k_ref[0, pl.ds(start, LANES), :]
            values.append(v_ref[0, pl.ds(start, LANES), :])
            bias_tile = min(d, 2) if d < BAND_BLOCKS - 1 else 3
            for h in range(2):
                s = lax.dot_general(qh[h], k2, NT_DIMS, preferred_element_type=F32)
                s = s + bias_ref[h, bias_tile]
                if at_start and d > 0:
                    s = jnp.where(jb >= 0, s, NEG_BIG)
                scores[h].append(s)
        _band_softmax_pv(scores, values, g_ref[0], o_ref)

    pl.when(i >= BAND_BLOCKS - 1)(lambda: tile(False))
    pl.when(i < BAND_BLOCKS - 1)(lambda: tile(True))


def _bias_lookup(rel_bias, t_minus_s):
    rel = jnp.clip(t_minus_s, -CA_REL_CLIP, CA_REL_CLIP) + CA_REL_CLIP
    onehot = (rel[..., None] == jnp.arange(2 * CA_REL_CLIP + 1)).astype(F32)
    return jnp.einsum("rck,hk->hrc", onehot, rel_bias, precision=lax.Precision.HIGHEST)


def _band_bias_tiles(rel_bias):
    r = jnp.arange(LANES)[:, None]
    u = jnp.arange(LANES)[None, :]

    def tile(dist):
        return _bias_lookup(rel_bias, dist * LANES + r - u)

    t0 = jnp.where((r < CHUNK) & (u >= CHUNK), NEG_BIG, tile(0))
    t1 = tile(1)
    t2 = jnp.broadcast_to(rel_bias[:, -1][:, None, None], t1.shape)
    t4 = jnp.where((r >= CHUNK) & (u < CHUNK), NEG_BIG, t2)
    return jnp.stack([t0, t1, t2, t4], axis=1).astype(F32)


def _band_prompt(q, k, v, g, bias_tiles):
    b, l, wdt = q.shape
    blk = pl.BlockSpec((1, LANES, LANES), lambda bi, hp, i: (bi, i, hp))
    full = pl.BlockSpec((1, l, LANES), lambda bi, hp, i: (bi, 0, hp))
    bias = pl.BlockSpec((2, 4, LANES, LANES), lambda bi, hp, i: (hp, 0, 0, 0))
    return pl.pallas_call(
        _band_prompt_kernel,
        grid=(b, wdt // LANES, l // LANES),
        in_specs=[blk, full, full, blk, bias],
        out_specs=blk,
        out_shape=jax.ShapeDtypeStruct((b, l, wdt), BF16),
        compiler_params=_params("parallel", "parallel", "arbitrary"),
        name="band_prompt",
    )(q, k, v, g, bias_tiles)


def _band_sample_kernel(q_ref, kn_ref, vn_ref, kc_ref, vc_ref, g_ref, bias_ref, o_ref):
    tq = q_ref.shape[1]
    qh = _pair_queries(q_ref[0])
    pad = jnp.zeros((LANES - tq, LANES), BF16)
    n_cached = kc_ref.shape[1] // LANES
    scores = ([], [])
    values = []
    for d in range(n_cached + 1):
        if d < n_cached:
            k2 = kc_ref[0, d * LANES:(d + 1) * LANES, :].astype(BF16)
            values.append(vc_ref[0, d * LANES:(d + 1) * LANES, :].astype(BF16))
        else:
            k2 = jnp.concatenate([kn_ref[0], pad], axis=0)
            values.append(jnp.concatenate([vn_ref[0], pad], axis=0))
        for h in range(2):
            s = lax.dot_general(qh[h], k2, NT_DIMS, preferred_element_type=F32)
            scores[h].append(s + bias_ref[h, :, d * LANES:(d + 1) * LANES])
    _band_softmax_pv(scores, values, g_ref[0], o_ref)


def _band_sample_bias(rel_bias, tq):
    qi = jnp.arange(tq)[:, None]
    kj = jnp.arange(CA_LEFT + LANES)[None, :]
    return jnp.where(kj < CA_LEFT + tq, _bias_lookup(rel_bias, qi - kj + CA_LEFT), NEG_BIG).astype(F32)


def _band_sample(q, kn, vn, kc, vc, g, bias):
    b, t, wdt = q.shape
    p = kc.shape[1]
    assert p == CA_LEFT and t == CHUNK
    new = pl.BlockSpec((1, t, LANES), lambda bi, hp: (bi, 0, hp))
    old = pl.BlockSpec((1, p, LANES), lambda bi, hp: (bi, 0, hp))
    bspec = pl.BlockSpec((2, t, CA_LEFT + LANES), lambda bi, hp: (hp, 0, 0))
    return pl.pallas_call(
        _band_sample_kernel,
        grid=(b, wdt // LANES),
        in_specs=[new, new, new, old, old, new, bspec],
        out_specs=new,
        out_shape=jax.ShapeDtypeStruct((b, t, wdt), BF16),
        compiler_params=_params("parallel", "parallel"),
        name="band_sample",
    )(q, kn, vn, kc, vc, g, bias)


def _even_layer(x, xb, past_k, past_v, conv_state, h0, wts, ssd_chunk, tm):
    w_in, w_dt, conv_w, conv_b, dt_bias, a_log, d_skip, norm_w, w_out, ln_g, ln_b = wts
    b, l, _ = x.shape
    m = b * l
    x2 = x.reshape(m, D_MODEL)
    xb2 = xb.reshape(m, D_MODEL)
    tn = 1024
    (qb,) = _matmul(xb2, w_in, L0_Q, SB_WIDTH, (BF16,), tm, tn)
    k, kb = _matmul(xb2, w_in, L0_K, SB_WIDTH, (F32, BF16), tm, tn)
    v, vb = _matmul(xb2, w_in, L0_V, SB_WIDTH, (F32, BF16), tm, tn)
    ga, dt_raw = _matmul(xb2, w_in, L0_G, SB_WIDTH, (F32,), tm, tn, w_side=w_dt)
    (z,) = _matmul(xb2, w_in, L0_Z, SSD_WIDTH, (F32,), tm, tn)
    (xbc,) = _matmul(xb2, w_in, L0_XBC, SSD_CONV_DIM, (F32,), tm, tn)

    def seq(a):
        return a.reshape(b, l, a.shape[-1])

    if past_k is None:
        o_a = _sb_prompt(seq(qb), seq(kb), seq(vb), seq(ga))
    else:
        o_a = _sb_sample(seq(qb), seq(kb), seq(vb), past_k, past_v, seq(ga))
    xbc3 = seq(xbc)
    o_b, h_new = _ssd(xbc3, seq(z), seq(dt_raw), conv_state, h0, conv_w, conv_b, dt_bias, a_log,
                      d_skip, norm_w, ssd_chunk)
    y, yb = _proj_ln([o_a.reshape(m, SB_WIDTH), o_b.reshape(m, SSD_WIDTH)], x2, w_out, ln_g, ln_b,
                     min(tm, 256))
    tail = jnp.concatenate([conv_state, xbc3], axis=1)[:, -(SSD_CONV - 1):]
    return (seq(y), seq(yb), k.reshape(b, l, 16, HEAD_DIM), v.reshape(b, l, 16, HEAD_DIM), tail,
            h_new.reshape(b, SSD_HEADS, HEAD_DIM, SSD_STATE))


def _odd_layer(x, xb, past_k, past_v, wts, tm):
    w_in, bias, w_out, ln_g, ln_b = wts
    b, l, _ = x.shape
    m = b * l
    x2 = x.reshape(m, D_MODEL)
    xb2 = xb.reshape(m, D_MODEL)
    tn = 1024
    (qb,) = _matmul(xb2, w_in, 0, CA_WIDTH, (BF16,), tm, tn)
    k, kb = _matmul(xb2, w_in, CA_WIDTH, CA_WIDTH, (F32, BF16), tm, tn)
    v, vb = _matmul(xb2, w_in, 2 * CA_WIDTH, CA_WIDTH, (F32, BF16), tm, tn)
    (g,) = _matmul(xb2, w_in, 3 * CA_WIDTH, CA_WIDTH, (F32,), tm, tn)

    def seq(a):
        return a.reshape(b, l, a.shape[-1])

    if past_k is None:
        o = _band_prompt(seq(qb), seq(kb), seq(vb), seq(g), bias)
        k_cat, v_cat = seq(k), seq(v)
    else:
        o = _band_sample(seq(qb), seq(kb), seq(vb), past_k, past_v, seq(g), bias)
        k_cat = jnp.concatenate([past_k, seq(k)], axis=1)
        v_cat = jnp.concatenate([past_v, seq(v)], axis=1)
    y, _ = _proj_ln([o.reshape(m, CA_WIDTH)], x2, w_out, ln_g, ln_b, min(tm, 256))
    keep = min(CA_LEFT, k_cat.shape[1])
    return (seq(y), k_cat[:, -keep:].reshape(b, keep, CA_HEADS, HEAD_DIM),
            v_cat[:, -keep:].reshape(b, keep, CA_HEADS, HEAD_DIM))


def kernel(x_prompt, x_sample, cache_sb_k, cache_sb_v, state_ssm, state_conv, cache_band_k,
           cache_band_v, even_w_in, even_conv_w, even_conv_b, even_dt_bias, even_a_log, even_d_skip,
           even_norm_w, even_w_out, even_ln_g, even_ln_b, odd_w_in, odd_rel_bias, odd_w_out,
           odd_ln_g, odd_ln_b):
    bp, lp, _ = x_prompt.shape
    bs, ls, _ = x_sample.shape
    past = cache_sb_k.shape[2]
    w_in0 = even_w_in[0].astype(BF16)
    w_dt0 = jnp.pad(w_in0[:, L0_DT:], ((0, 0), (0, LANES - SSD_HEADS)))
    wts0 = (w_in0, w_dt0, even_conv_w[0], even_conv_b[0], even_dt_bias[0], even_a_log[0],
            even_d_skip[0], even_norm_w[0], even_w_out[0].astype(BF16), even_ln_g[0], even_ln_b[0])
    zero_conv = jnp.zeros((bp, SSD_CONV - 1, SSD_CONV_DIM), F32)
    zero_h = jnp.zeros((bp, SSD_WIDTH, SSD_STATE), F32)
    yp, ypb, p_sb_k, p_sb_v, p_conv, p_ssm = _even_layer(
        x_prompt, x_prompt.astype(BF16), None, None, zero_conv, zero_h, wts0, 2 * CHUNK, 512)
    ys, ysb, s_sb_k, s_sb_v, s_conv, s_ssm = _even_layer(
        x_sample, x_sample.astype(BF16), cache_sb_k[0].reshape(bs, past, SB_WIDTH),
        cache_sb_v[0].reshape(bs, past, SB_WIDTH), state_conv[0],
        state_ssm[0].reshape(bs, SSD_WIDTH, SSD_STATE), wts0, CHUNK, 512)
    w_in1 = odd_w_in[0].astype(BF16)
    w_out1 = odd_w_out[0].astype(BF16)
    wts1p = (w_in1, _band_bias_tiles(odd_rel_bias[0]), w_out1, odd_ln_g[0], odd_ln_b[0])
    wts1s = (w_in1, _band_sample_bias(odd_rel_bias[0], ls), w_out1, odd_ln_g[0], odd_ln_b[0])
    yp, p_band_k, p_band_v = _odd_layer(yp, ypb, None, None, wts1p, 512)
    band = cache_band_k.shape[2]
    ys, s_band_k, s_band_v = _odd_layer(
        ys, ysb, cache_band_k[0].reshape(bs, band, CA_WIDTH), cache_band_v[0].reshape(bs, band, CA_WIDTH),
        wts1s, 512)

    def lead(a):
        return a[None]

    return (yp, ys, lead(p_sb_k), lead(p_sb_v), lead(p_ssm), lead(p_conv), lead(p_band_k),
            lead(p_band_v), lead(s_sb_k), lead(s_sb_v), lead(s_ssm), lead(s_conv), lead(s_band_k),
            lead(s_band_v))
```

```python
import functools
import math

import jax
import jax.numpy as jnp
from jax import lax
from jax.experimental import pallas as pl
from jax.experimental.pallas import tpu as pltpu

F32 = jnp.float32
BF16 = jnp.bfloat16

D_MODEL = 2048
CHUNK = 64
HEAD_DIM = 64
LANES = 128
SB_WIDTH = 1024
SSD_WIDTH = 2048
SSD_HEADS = 32
SSD_GROUPS = 4
SSD_STATE = 128
SSD_CONV = 4
SSD_CONV_DIM = SSD_WIDTH + 2 * SSD_GROUPS * SSD_STATE
CA_WIDTH = 2048
CA_LEFT = 8 * CHUNK
CA_REL_CLIP = 128
DEPTH = 2
DEEPNORM_ALPHA = (2 * DEPTH) ** 0.25
LN_EPS = 1e-5
RMS_EPS = 1e-5
ATTN_SCALE = HEAD_DIM ** -0.5
NEG_BIG = -1e30
L0_Q, L0_K, L0_V, L0_G, L0_Z, L0_XBC, L0_DT = 0, 1024, 2048, 3072, 4096, 6144, 9216
VMEM_LIMIT = 56 * 1024 * 1024
ROW_TILE = 512
COL_TILE = 1024

NT_DIMS = (((1,), (1,)), ((), ()))
TN_DIMS = (((0,), (0,)), ((), ()))


def _params(*sem):
    return pltpu.CompilerParams(dimension_semantics=sem, vmem_limit_bytes=VMEM_LIMIT)


def _silu(x):
    return x * (1.0 / (1.0 + jnp.exp(-x)))


def _split3(x):
    p0 = x.astype(BF16)
    r = x - p0.astype(F32)
    p1 = r.astype(BF16)
    p2 = (r - p1.astype(F32)).astype(BF16)
    return p0, p1, p2


def _mm_kernel(x_ref, w_ref, *refs, has_side):
    x = x_ref[...]
    o_refs = list(refs)
    if has_side:
        ws_ref = o_refs.pop(0)
        os_ref = o_refs.pop()
        os_ref[...] = jnp.dot(x, ws_ref[...], preferred_element_type=F32)
    acc = jnp.dot(x, w_ref[...], preferred_element_type=F32)
    for o_ref in o_refs:
        o_ref[...] = acc.astype(o_ref.dtype)


def _matmul(x, w, col0, ncols, out_dtypes, w_side=None):
    m, k = x.shape
    tm, tn = math.gcd(m, ROW_TILE), COL_TILE
    assert ncols % tn == 0 and col0 % tn == 0
    cb0 = col0 // tn
    in_specs = [pl.BlockSpec((tm, k), lambda j, i: (i, 0)),
                pl.BlockSpec((k, tn), lambda j, i: (0, cb0 + j))]
    out_specs = [pl.BlockSpec((tm, tn), lambda j, i: (i, j)) for _ in out_dtypes]
    out_shape = [jax.ShapeDtypeStruct((m, ncols), dt) for dt in out_dtypes]
    args = [x, w]
    if w_side is not None:
        assert ncols == tn
        ns = w_side.shape[1]
        in_specs.append(pl.BlockSpec((k, ns), lambda j, i: (0, 0)))
        out_specs.append(pl.BlockSpec((tm, ns), lambda j, i: (i, 0)))
        out_shape.append(jax.ShapeDtypeStruct((m, ns), F32))
        args.append(w_side)
    return pl.pallas_call(
        functools.partial(_mm_kernel, has_side=w_side is not None),
        grid=(ncols // tn, m // tm),
        in_specs=in_specs,
        out_specs=out_specs,
        out_shape=out_shape,
        compiler_params=_params("parallel", "arbitrary"),
        name=f"mm_c{col0}_n{ncols}_m{m}",
    )(*args)


def _mm_t_kernel(w_ref, x_ref, ob_ref, of_ref, *, first_kept):
    acc = lax.dot_general(w_ref[...], x_ref[0], NT_DIMS, preferred_element_type=F32)
    ob_ref[0] = acc.astype(ob_ref.dtype)

    def keep_f32():
        of_ref[0] = acc

    if first_kept == 0:
        keep_f32()
    else:
        pl.when(pl.program_id(1) >= first_kept)(keep_f32)


def _matmul_t(w_t, x, row0, nrows, keep):
    s, l, k = x.shape
    tm = math.gcd(l, ROW_TILE)
    assert row0 % nrows == 0 and keep % tm == 0 and keep <= l
    first_kept = (l - keep) // tm
    return pl.pallas_call(
        functools.partial(_mm_t_kernel, first_kept=first_kept),
        grid=(s, l // tm),
        in_specs=[pl.BlockSpec((nrows, k), lambda si, i: (row0 // nrows, 0)),
                  pl.BlockSpec((1, tm, k), lambda si, i: (si, i, 0))],
        out_specs=[pl.BlockSpec((1, nrows, tm), lambda si, i: (si, 0, i)),
                   pl.BlockSpec((1, nrows, tm), lambda si, i: (si, 0, jnp.maximum(i - first_kept, 0)))],
        out_shape=[jax.ShapeDtypeStruct((s, nrows, l), BF16),
                   jax.ShapeDtypeStruct((s, nrows, keep), F32)],
        compiler_params=_params("parallel", "arbitrary"),
        name=f"mm_t_r{row0}_n{nrows}_l{l}",
    )(w_t, x)


def _proj_ln_kernel(*refs, widths):
    n = len(widths)
    o_refs = refs[:n]
    x_ref, w_ref, g_ref, b_ref, y_ref, yb_ref = refs[n:]
    mix = None
    row = 0
    for o_ref, wd in zip(o_refs, widths):
        part = jnp.dot(o_ref[...], w_ref[row:row + wd, :], preferred_element_type=F32)
        mix = part if mix is None else mix + part
        row += wd
    h = DEEPNORM_ALPHA * x_ref[...] + mix
    mu = jnp.mean(h, axis=-1, keepdims=True)
    hc = h - mu
    var = jnp.mean(hc * hc, axis=-1, keepdims=True)
    y = hc * lax.rsqrt(var + LN_EPS) * g_ref[...] + b_ref[...]
    y_ref[...] = y
    yb_ref[...] = y.astype(BF16)


def _proj_ln(o_parts, x, w_out, ln_g, ln_b):
    m = x.shape[0]
    widths = tuple(o.shape[1] for o in o_parts)
    ktot = sum(widths)
    tm = math.gcd(m, ROW_TILE // 2)
    assert w_out.shape == (ktot, D_MODEL)
    in_specs = [pl.BlockSpec((tm, wd), lambda i: (i, 0)) for wd in widths]
    in_specs += [pl.BlockSpec((tm, D_MODEL), lambda i: (i, 0)),
                 pl.BlockSpec((ktot, D_MODEL), lambda i: (0, 0)),
                 pl.BlockSpec((1, D_MODEL), lambda i: (0, 0)),
                 pl.BlockSpec((1, D_MODEL), lambda i: (0, 0))]
    y, yb = pl.pallas_call(
        functools.partial(_proj_ln_kernel, widths=widths),
        grid=(m // tm,),
        in_specs=in_specs,
        out_specs=[pl.BlockSpec((tm, D_MODEL), lambda i: (i, 0)),
                   pl.BlockSpec((tm, D_MODEL), lambda i: (i, 0))],
        out_shape=[jax.ShapeDtypeStruct((m, D_MODEL), F32),
                   jax.ShapeDtypeStruct((m, D_MODEL), BF16)],
        compiler_params=_params("parallel"),
        name=f"proj_ln_k{ktot}_m{m}",
    )(*o_parts, x, w_out, ln_g.reshape(1, D_MODEL), ln_b.reshape(1, D_MODEL))
    return y, yb


def _head_masks(rows):
    lane = lax.broadcasted_iota(jnp.int32, (rows, LANES), 1)
    return lane < HEAD_DIM


SB_KEYS = 256
SB_STEP_LANES = 2 * LANES
SB_DEAD = 104.5


def _pair_queries(q2):
    head0 = _head_masks(q2.shape[0])
    qs = q2 * jnp.asarray(ATTN_SCALE, q2.dtype)
    zero = jnp.zeros_like(qs)
    return jnp.where(head0, qs, zero), jnp.where(head0, zero, qs)


def _sb_block(qh, kt, vt, carry, acc, upper, diag_mask):
    z = jnp.dot(qh, kt, preferred_element_type=F32)
    sp = jnp.maximum(z, 0.0) + jnp.log(1.0 + jnp.exp(-jnp.abs(z)))
    if diag_mask is not None:
        sp = jnp.where(diag_mask, sp, 0.0)
    sp_hi = sp.astype(BF16)
    sp_lo = (sp - sp_hi.astype(F32)).astype(BF16)
    mass = (jnp.dot(sp_hi, upper, preferred_element_type=F32)
            + jnp.dot(sp_lo, upper, preferred_element_type=F32)) + carry
    w = jnp.exp(z - mass)
    if diag_mask is not None:
        w = jnp.where(diag_mask, w, 0.0)
    acc = acc + lax.dot_general(w.astype(BF16), vt, NT_DIMS, preferred_element_type=F32)
    return mass[:, 0:1], acc


def _sb_upper(n):
    r = lax.broadcasted_iota(jnp.int32, (n, n), 0)
    c = lax.broadcasted_iota(jnp.int32, (n, n), 1)
    return (r >= c).astype(BF16)


def _sb_alive(carries):
    return jnp.min(functools.reduce(jnp.minimum, carries)) < SB_DEAD


def _sb_visit(qh, k_pairs, v_pairs, carries, accs, upper, mask):
    out_c, out_a = [], []
    for h, (q1, c, a) in enumerate(zip(qh, carries, accs)):
        c, a = _sb_block(q1, k_pairs[h // 2], v_pairs[h // 2], c, a, upper, mask)
        out_c.append(c)
        out_a.append(a)
    return tuple(out_c), tuple(out_a)


def _sb_queries(q_ref):
    qh = []
    for p in range(q_ref.shape[2] // LANES):
        qh.extend(_pair_queries(q_ref[0, :, p * LANES:(p + 1) * LANES]))
    return qh


def _sb_finish(accs, g_ref, o_ref):
    head0 = _head_masks(accs[0].shape[0])
    for p in range(len(accs) // 2):
        lanes = slice(p * LANES, (p + 1) * LANES)
        o = jnp.where(head0, accs[2 * p], accs[2 * p + 1]) * _silu(g_ref[0, :, lanes])
        o_ref[0, :, lanes] = o.astype(o_ref.dtype)


def _pair_slabs(ref, lead, n_pair, cols):
    return [ref[lead, p * LANES:(p + 1) * LANES, cols].astype(BF16) for p in range(n_pair)]


def _sb_prompt_kernel(q_ref, k_ref, v_ref, g_ref, o_ref):
    qi = pl.program_id(2)
    tq = q_ref.shape[1]
    assert tq == SB_KEYS
    qh = _sb_queries(q_ref)
    n_pair = len(qh) // 2
    upper = _sb_upper(SB_KEYS)
    r = lax.broadcasted_iota(jnp.int32, (tq, SB_KEYS), 0)
    c = lax.broadcasted_iota(jnp.int32, (tq, SB_KEYS), 1)
    strict = c < r

    def visit(kb, carries, accs, mask):
        cols = pl.ds(pl.multiple_of(kb * SB_KEYS, SB_KEYS), SB_KEYS)
        return _sb_visit(qh, _pair_slabs(k_ref, 0, n_pair, cols), _pair_slabs(v_ref, 0, n_pair, cols),
                         carries, accs, upper, mask)

    zc = (jnp.zeros((tq, 1), F32),) * len(qh)
    za = (jnp.zeros((tq, LANES), F32),) * len(qh)
    carries, accs = visit(qi, zc, za, strict)

    def older(state):
        kb, _, carries, accs = state
        carries, accs = visit(kb, carries, accs, None)
        return kb - 1, _sb_alive(carries), carries, accs

    state = lax.while_loop(lambda s: jnp.logical_and(s[0] >= 0, s[1]), older,
                           (qi - 1, _sb_alive(carries), carries, accs))
    _sb_finish(state[3], g_ref, o_ref)


def _sb_prompt(q, kt, vt, g):
    b, l, wdt = q.shape
    tq = SB_KEYS
    assert l % tq == 0 and wdt % SB_STEP_LANES == 0
    blk = pl.BlockSpec((1, tq, SB_STEP_LANES), lambda bi, hp, qi: (bi, qi, hp))
    full = pl.BlockSpec((1, SB_STEP_LANES, l), lambda bi, hp, qi: (bi, hp, 0))
    return pl.pallas_call(
        _sb_prompt_kernel,
        grid=(b, wdt // SB_STEP_LANES, l // tq),
        in_specs=[blk, full, full, blk],
        out_specs=blk,
        out_shape=jax.ShapeDtypeStruct((b, l, wdt), BF16),
        compiler_params=_params("parallel", "parallel", "arbitrary"),
        name="sb_prompt",
    )(q, kt, vt, g)


def _sb_sample_kernel(q_ref, kn_ref, vn_ref, g_ref, kc_hbm, vc_hbm, o_ref, kbuf, vbuf, sem, *, n_cached):
    seq = pl.program_id(0)
    tq = q_ref.shape[1]
    qh = _sb_queries(q_ref)
    n_pair = len(qh) // 2
    upper = _sb_upper(SB_KEYS)
    r = lax.broadcasted_iota(jnp.int32, (tq, LANES), 0)
    c = lax.broadcasted_iota(jnp.int32, (tq, LANES), 1)
    strict = c < r
    zc = (jnp.zeros((tq, 1), F32),) * len(qh)
    za = (jnp.zeros((tq, LANES), F32),) * len(qh)

    def block_copies(kb, slot):
        cols = pl.ds(pl.multiple_of(kb * SB_KEYS, SB_KEYS), SB_KEYS)
        return (pltpu.make_async_copy(kc_hbm.at[seq, :, cols], kbuf.at[slot], sem.at[0, slot]),
                pltpu.make_async_copy(vc_hbm.at[seq, :, cols], vbuf.at[slot], sem.at[1, slot]))

    def start_block(kb):
        for cp in block_copies(kb, kb % 2):
            cp.start()

    def wait_block(kb):
        for cp in block_copies(kb, kb % 2):
            cp.wait()

    start_block(n_cached - 1)
    everything = slice(None)
    carries, accs = _sb_visit(qh, _pair_slabs(kn_ref, 0, n_pair, everything),
                              _pair_slabs(vn_ref, 0, n_pair, everything), zc, za, _sb_upper(LANES), strict)

    def older(state):
        kb, _, carries, accs = state
        wait_block(kb)
        pl.when(kb > 0)(lambda: start_block(kb - 1))
        slot = kb % 2
        carries, accs = _sb_visit(qh, _pair_slabs(kbuf, slot, n_pair, everything),
                                  _pair_slabs(vbuf, slot, n_pair, everything), carries, accs, upper, None)
        return kb - 1, _sb_alive(carries), carries, accs

    state = lax.while_loop(lambda s: jnp.logical_and(s[0] >= 0, s[1]), older,
                           (jnp.int32(n_cached - 1), _sb_alive(carries), carries, accs))
    pl.when(state[0] >= 0)(lambda: wait_block(state[0]))
    _sb_finish(state[3], g_ref, o_ref)


def _sb_sample(q, knt, vnt, g, kct, vct):
    b, t, wdt = q.shape
    p = kct.shape[2]
    assert p % SB_KEYS == 0 and t <= LANES and knt.shape == (b, wdt, LANES)
    tok = pl.BlockSpec((1, t, wdt), lambda bi: (bi, 0, 0))
    new = pl.BlockSpec((1, wdt, LANES), lambda bi: (bi, 0, 0))
    hbm = pl.BlockSpec(memory_space=pl.ANY)
    return pl.pallas_call(
        functools.partial(_sb_sample_kernel, n_cached=p // SB_KEYS),
        grid=(b,),
        in_specs=[tok, new, new, tok, hbm, hbm],
        out_specs=tok,
        out_shape=jax.ShapeDtypeStruct((b, t, wdt), BF16),
        scratch_shapes=[pltpu.VMEM((2, wdt, SB_KEYS), F32),
                        pltpu.VMEM((2, wdt, SB_KEYS), F32),
                        pltpu.SemaphoreType.DMA((2, 2))],
        compiler_params=_params("arbitrary"),
        name="sb_sample",
    )(q, knt, vnt, g, kct, vct)


CONV_PAD = 8
CONV_SLAB = 512


def _ssd_kernel(xbc_ref, z_ref, dt_ref, cs_ref, h0_ref, cw_ref, cb_ref, dtb_ref, alog_ref, drow_ref,
                nw_ref, o_ref, h_ref, xp_ref, xa_ref, y_ref):
    ci = pl.program_id(1)
    q = xbc_ref.shape[1]
    taps = SSD_CONV - 1

    @pl.when(ci == 0)
    def _():
        xp_ref[CONV_PAD - taps:CONV_PAD, :] = cs_ref[0]
        h_ref[0] = h0_ref[0]

    @pl.when(ci > 0)
    def _():
        xp_ref[CONV_PAD - taps:CONV_PAD, :] = xp_ref[CONV_PAD + q - taps:CONV_PAD + q, :]

    xp_ref[CONV_PAD:CONV_PAD + q, :] = xbc_ref[0]
    for cs in range(0, SSD_CONV_DIM, CONV_SLAB):
        acc = cb_ref[:, cs:cs + CONV_SLAB] + cw_ref[0:1, cs:cs + CONV_SLAB] * xp_ref[
            CONV_PAD - taps:CONV_PAD - taps + q, cs:cs + CONV_SLAB]
        for t in range(1, SSD_CONV):
            acc = acc + cw_ref[t:t + 1, cs:cs + CONV_SLAB] * xp_ref[
                CONV_PAD - taps + t:CONV_PAD - taps + t + q, cs:cs + CONV_SLAB]
        xa_ref[:, cs:cs + CONV_SLAB] = _silu(acc)

    dtv = dt_ref[0] + dtb_ref[...]
    dtv = jnp.maximum(dtv, 0.0) + jnp.log1p(jnp.exp(-jnp.abs(dtv)))
    da = dtv * (-jnp.exp(alog_ref[...]))
    r = lax.broadcasted_iota(jnp.int32, (q, q), 0)
    c = lax.broadcasted_iota(jnp.int32, (q, q), 1)
    causal = c <= r
    tril = causal.astype(BF16)
    cum = sum(jnp.dot(tril, p, preferred_element_type=F32) for p in _split3(da))
    er = lax.broadcasted_iota(jnp.int32, (LANES, LANES), 0)
    ec = lax.broadcasted_iota(jnp.int32, (LANES, LANES), 1)
    eye = (er == ec).astype(BF16)

    def transpose(x):
        return sum(lax.dot_general(eye, p, NT_DIMS, preferred_element_type=F32) for p in _split3(x))

    cum_t = transpose(cum)
    dt_t = transpose(dtv)
    cum_last = cum[q - 1:q, :]
    to_end = jnp.exp(cum_last - cum) * dtv
    exp_cum = jnp.exp(cum)
    chunk_decay = jnp.broadcast_to(jnp.exp(cum_last), (LANES, LANES))
    head0 = _head_masks(q)
    top = lax.broadcasted_iota(jnp.int32, (LANES, LANES), 0) < HEAD_DIM

    for g in range(SSD_GROUPS):
        b0 = SSD_WIDTH + g * SSD_STATE
        c0 = SSD_WIDTH + SSD_GROUPS * SSD_STATE + g * SSD_STATE
        bm = xa_ref[:, b0:b0 + SSD_STATE].astype(BF16)
        cm = xa_ref[:, c0:c0 + SSD_STATE].astype(BF16)
        cb = lax.dot_general(cm, bm, NT_DIMS, preferred_element_type=F32)
        for j in range(SSD_HEADS // SSD_GROUPS // 2):
            pr = g * (SSD_HEADS // SSD_GROUPS // 2) + j
            h0, h1 = 2 * pr, 2 * pr + 1
            lo = pr * LANES
            x2 = xa_ref[:, lo:lo + LANES]
            x2b = x2.astype(BF16)
            yd = []
            for h in (h0, h1):
                seg = cum[:, h:h + 1] - cum_t[h:h + 1, :]
                wgt = cb * jnp.exp(jnp.where(causal, seg, -jnp.inf)) * dt_t[h:h + 1, :]
                yd.append(jnp.dot(wgt.astype(BF16), x2b, preferred_element_type=F32))
            y = jnp.where(head0, yd[0], yd[1])
            hs = h_ref[0, lo:lo + LANES, :]
            yoff = lax.dot_general(cm, hs.astype(BF16), NT_DIMS, preferred_element_type=F32)
            y = y + yoff * jnp.where(head0, exp_cum[:, h0:h0 + 1], exp_cum[:, h1:h1 + 1])
            y_ref[:, lo:lo + LANES] = y + drow_ref[:, lo:lo + LANES] * x2
            xw = x2 * jnp.where(head0, to_end[:, h0:h0 + 1], to_end[:, h1:h1 + 1])
            st = lax.dot_general(xw.astype(BF16), bm, TN_DIMS, preferred_element_type=F32)
            dec = jnp.where(top, chunk_decay[:, h0:h0 + 1], chunk_decay[:, h1:h1 + 1])
            h_ref[0, lo:lo + LANES, :] = dec * hs + st

    gw = SSD_WIDTH // SSD_GROUPS
    for g in range(SSD_GROUPS):
        yg = y_ref[:, g * gw:(g + 1) * gw] * _silu(z_ref[0, :, g * gw:(g + 1) * gw])
        ms = jnp.mean(yg * yg, axis=-1, keepdims=True)
        o_ref[0, :, g * gw:(g + 1) * gw] = (yg * lax.rsqrt(ms + RMS_EPS)
                                            * nw_ref[:, g * gw:(g + 1) * gw]).astype(o_ref.dtype)


def _ssd(xbc, z, dt_raw, conv_state, h0, conv_w, conv_b, dt_bias, a_log, d_skip, norm_w, q):
    b, l, _ = xbc.shape
    assert l % q == 0
    pad = LANES - SSD_HEADS
    dtb = jnp.pad(dt_bias, (0, pad)).reshape(1, LANES)
    alog = jnp.pad(a_log, (0, pad)).reshape(1, LANES)
    drow = jnp.repeat(d_skip, HEAD_DIM).reshape(1, SSD_WIDTH)

    def tok(wd):
        return pl.BlockSpec((1, q, wd), lambda bi, ci: (bi, ci, 0))

    def per_seq(shape):
        return pl.BlockSpec((1,) + shape, lambda bi, ci: (bi, 0, 0))

    def const(shape):
        return pl.BlockSpec(shape, lambda bi, ci: (0, 0))

    o, h = pl.pallas_call(
        _ssd_kernel,
        grid=(b, l // q),
        in_specs=[tok(SSD_CONV_DIM), tok(SSD_WIDTH), tok(LANES),
                  per_seq((SSD_CONV - 1, SSD_CONV_DIM)), per_seq((SSD_WIDTH, SSD_STATE)),
                  const((SSD_CONV, SSD_CONV_DIM)), const((1, SSD_CONV_DIM)), const((1, LANES)),
                  const((1, LANES)), const((1, SSD_WIDTH)), const((1, SSD_WIDTH))],
        out_specs=[tok(SSD_WIDTH), per_seq((SSD_WIDTH, SSD_STATE))],
        out_shape=[jax.ShapeDtypeStruct((b, l, SSD_WIDTH), BF16),
                   jax.ShapeDtypeStruct((b, SSD_WIDTH, SSD_STATE), F32)],
        scratch_shapes=[pltpu.VMEM((CONV_PAD + q, SSD_CONV_DIM), F32),
                        pltpu.VMEM((q, SSD_CONV_DIM), F32),
                        pltpu.VMEM((q, SSD_WIDTH), F32)],
        compiler_params=_params("parallel", "arbitrary"),
        name=f"ssd_q{q}",
    )(xbc, z, dt_raw, conv_state, h0, conv_w, conv_b.reshape(1, SSD_CONV_DIM), dtb, alog, drow,
      norm_w.reshape(1, SSD_WIDTH))
    return o, h


BAND_BLOCKS = CA_LEFT // LANES + 1


def _band_softmax_pv(scores, values, g, o_ref, rows=slice(None)):
    outs = []
    for s_list in scores:
        m = jnp.max(functools.reduce(jnp.maximum, s_list), axis=-1, keepdims=True)
        es = [jnp.exp(s - m) for s in s_list]
        den = jnp.sum(functools.reduce(lambda a, b: a + b, es), axis=-1, keepdims=True)
        num = functools.reduce(lambda a, b: a + b,
                               [lax.dot_general(e.astype(BF16), v, NT_DIMS, preferred_element_type=F32)
                                for e, v in zip(es, values)])
        outs.append(num / den)
    head0 = _head_masks(outs[0].shape[0])
    o_ref[0, rows, :] = (jnp.where(head0, outs[0], outs[1]) * _silu(g)).astype(o_ref.dtype)


def _band_prompt_kernel(q_ref, k_ref, v_ref, g_ref, bias_ref, o_ref):
    step = pl.program_id(2)
    n_sub = q_ref.shape[1] // LANES

    def sub_tile(s, i):
        rows = slice(s * LANES, (s + 1) * LANES)
        qh = _pair_queries(q_ref[0, rows, :])
        scores = ([], [])
        values = []
        for d in range(BAND_BLOCKS):
            if isinstance(i, int):
                if i - d < 0:
                    continue
                cols = pl.ds((i - d) * LANES, LANES)
            else:
                cols = pl.ds(pl.multiple_of((i - d) * LANES, LANES), LANES)
            kt = k_ref[0, :, cols]
            values.append(v_ref[0, :, cols])
            bias_tile = min(d, 2) if d < BAND_BLOCKS - 1 else 3
            for h in range(2):
                s_hd = jnp.dot(qh[h], kt, preferred_element_type=F32)
                scores[h].append(s_hd + bias_ref[h, bias_tile])
        _band_softmax_pv(scores, values, g_ref[0, rows, :], o_ref, rows)

    @pl.when(step == 0)
    def _():
        for s in range(n_sub):
            sub_tile(s, s)

    @pl.when(step > 0)
    def _():
        for s in range(n_sub):
            sub_tile(s, step * n_sub + s)


def _bias_lookup(rel_bias, t_minus_s):
    rel = jnp.clip(t_minus_s, -CA_REL_CLIP, CA_REL_CLIP) + CA_REL_CLIP
    onehot = (rel[..., None] == jnp.arange(2 * CA_REL_CLIP + 1)).astype(F32)
    return jnp.einsum("rck,hk->hrc", onehot, rel_bias, precision=lax.Precision.HIGHEST)


def _band_bias_tiles(rel_bias):
    r = jnp.arange(LANES)[:, None]
    u = jnp.arange(LANES)[None, :]

    def tile(dist):
        return _bias_lookup(rel_bias, dist * LANES + r - u)

    t0 = jnp.where((r < CHUNK) & (u >= CHUNK), NEG_BIG, tile(0))
    t1 = tile(1)
    t2 = jnp.broadcast_to(rel_bias[:, -1][:, None, None], t1.shape)
    t4 = jnp.where((r >= CHUNK) & (u < CHUNK), NEG_BIG, t2)
    return jnp.stack([t0, t1, t2, t4], axis=1).astype(F32)


def _band_prompt(q, kt, vt, g, bias_tiles):
    b, l, wdt = q.shape
    n_sub = math.gcd(l // LANES, BAND_BLOCKS - 1)
    assert n_sub == BAND_BLOCKS - 1 or l // LANES == n_sub
    tq = n_sub * LANES
    blk = pl.BlockSpec((1, tq, LANES), lambda bi, hp, i: (bi, i, hp))
    full = pl.BlockSpec((1, LANES, l), lambda bi, hp, i: (bi, hp, 0))
    bias = pl.BlockSpec((2, 4, LANES, LANES), lambda bi, hp, i: (hp, 0, 0, 0))
    return pl.pallas_call(
        _band_prompt_kernel,
        grid=(b, wdt // LANES, l // tq),
        in_specs=[blk, full, full, blk, bias],
        out_specs=blk,
        out_shape=jax.ShapeDtypeStruct((b, l, wdt), BF16),
        compiler_params=_params("parallel", "parallel", "arbitrary"),
        name="band_prompt",
    )(q, kt, vt, g, bias_tiles)


def _band_sample_kernel(q_ref, kn_ref, vn_ref, kc_ref, vc_ref, g_ref, bias_ref, o_ref):
    qh = _pair_queries(q_ref[0])
    n_cached = kc_ref.shape[2] // LANES
    scores = ([], [])
    values = []
    for d in range(n_cached + 1):
        if d < n_cached:
            kt = kc_ref[0, :, d * LANES:(d + 1) * LANES].astype(BF16)
            values.append(vc_ref[0, :, d * LANES:(d + 1) * LANES].astype(BF16))
        else:
            kt = kn_ref[0]
            values.append(vn_ref[0])
        for h in range(2):
            s = jnp.dot(qh[h], kt, preferred_element_type=F32)
            scores[h].append(s + bias_ref[h, :, d * LANES:(d + 1) * LANES])
    _band_softmax_pv(scores, values, g_ref[0], o_ref)


def _band_sample_bias(rel_bias, tq):
    qi = jnp.arange(tq)[:, None]
    kj = jnp.arange(CA_LEFT + LANES)[None, :]
    return jnp.where(kj < CA_LEFT + tq, _bias_lookup(rel_bias, qi - kj + CA_LEFT), NEG_BIG).astype(F32)


def _band_sample(q, knt, vnt, kct, vct, g, bias):
    b, t, wdt = q.shape
    p = kct.shape[2]
    assert p == CA_LEFT and t == CHUNK
    tok = pl.BlockSpec((1, t, LANES), lambda bi, hp: (bi, 0, hp))
    new = pl.BlockSpec((1, LANES, LANES), lambda bi, hp: (bi, hp, 0))
    old = pl.BlockSpec((1, LANES, p), lambda bi, hp: (bi, hp, 0))
    bspec = pl.BlockSpec((2, t, CA_LEFT + LANES), lambda bi, hp: (hp, 0, 0))
    return pl.pallas_call(
        _band_sample_kernel,
        grid=(b, wdt // LANES),
        in_specs=[tok, new, new, old, old, tok, bspec],
        out_specs=tok,
        out_shape=jax.ShapeDtypeStruct((b, t, wdt), BF16),
        compiler_params=_params("parallel", "parallel"),
        name="band_sample",
    )(q, knt, vnt, kct, vct, g, bias)


def _feature_major(cache):
    b, p, h, d = cache.shape
    return jnp.transpose(cache, (0, 2, 3, 1)).reshape(b, h * d, p)


def _token_major(kt, heads):
    b, _, t = kt.shape
    return jnp.transpose(kt.reshape(b, heads, HEAD_DIM, t), (0, 3, 1, 2))


def _pad_keys(kt):
    return jnp.pad(kt, ((0, 0), (0, 0), (0, LANES - kt.shape[2])))


def _even_layer(x, xb, past_k, past_v, conv_state, h0, wts, ssd_chunk):
    w_in, w_kv_t, w_dt, conv_w, conv_b, dt_bias, a_log, d_skip, norm_w, w_out, ln_g, ln_b = wts
    b, l, _ = x.shape
    m = b * l
    x2 = x.reshape(m, D_MODEL)
    xb2 = xb.reshape(m, D_MODEL)
    (qb,) = _matmul(xb2, w_in, L0_Q, SB_WIDTH, (BF16,))
    kt_b, kt = _matmul_t(w_kv_t, xb, 0, SB_WIDTH, l)
    vt_b, vt = _matmul_t(w_kv_t, xb, SB_WIDTH, SB_WIDTH, l)
    ga, dt_raw = _matmul(xb2, w_in, L0_G, SB_WIDTH, (F32,), w_side=w_dt)
    (z,) = _matmul(xb2, w_in, L0_Z, SSD_WIDTH, (F32,))
    (xbc,) = _matmul(xb2, w_in, L0_XBC, SSD_CONV_DIM, (F32,))

    def seq(a):
        return a.reshape(b, l, a.shape[-1])

    if past_k is None:
        o_a = _sb_prompt(seq(qb), kt_b, vt_b, seq(ga))
    else:
        o_a = _sb_sample(seq(qb), _pad_keys(kt_b), _pad_keys(vt_b), seq(ga), _feature_major(past_k),
                         _feature_major(past_v))
    xbc3 = seq(xbc)
    o_b, h_new = _ssd(xbc3, seq(z), seq(dt_raw), conv_state, h0, conv_w, conv_b, dt_bias, a_log,
                      d_skip, norm_w, ssd_chunk)
    y, yb = _proj_ln([o_a.reshape(m, SB_WIDTH), o_b.reshape(m, SSD_WIDTH)], x2, w_out, ln_g, ln_b)
    tail = jnp.concatenate([conv_state, xbc3], axis=1)[:, -(SSD_CONV - 1):]
    heads = SB_WIDTH // HEAD_DIM
    return (seq(y), seq(yb), _token_major(kt, heads), _token_major(vt, heads), tail,
            h_new.reshape(b, SSD_HEADS, HEAD_DIM, SSD_STATE))


def _odd_layer(x, xb, past_k, past_v, wts):
    w_in, w_kv_t, bias, w_out, ln_g, ln_b = wts
    b, l, _ = x.shape
    m = b * l
    x2 = x.reshape(m, D_MODEL)
    xb2 = xb.reshape(m, D_MODEL)
    keep = min(CA_LEFT, l)
    (qb,) = _matmul(xb2, w_in, 0, CA_WIDTH, (BF16,))
    kt_b, kt = _matmul_t(w_kv_t, xb, 0, CA_WIDTH, keep)
    vt_b, vt = _matmul_t(w_kv_t, xb, CA_WIDTH, CA_WIDTH, keep)
    (g,) = _matmul(xb2, w_in, 3 * CA_WIDTH, CA_WIDTH, (F32,))

    def seq(a):
        return a.reshape(b, l, a.shape[-1])

    if past_k is None:
        o = _band_prompt(seq(qb), kt_b, vt_b, seq(g), bias)
    else:
        past_kt, past_vt = _feature_major(past_k), _feature_major(past_v)
        o = _band_sample(seq(qb), _pad_keys(kt_b), _pad_keys(vt_b), past_kt, past_vt, seq(g), bias)
        kt = jnp.concatenate([past_kt, kt], axis=2)[:, :, -CA_LEFT:]
        vt = jnp.concatenate([past_vt, vt], axis=2)[:, :, -CA_LEFT:]
    y, _ = _proj_ln([o.reshape(m, CA_WIDTH)], x2, w_out, ln_g, ln_b)
    heads = CA_WIDTH // HEAD_DIM
    return seq(y), _token_major(kt, heads), _token_major(vt, heads)


def kernel(x_prompt, x_sample, cache_sb_k, cache_sb_v, state_ssm, state_conv, cache_band_k,
           cache_band_v, even_w_in, even_conv_w, even_conv_b, even_dt_bias, even_a_log, even_d_skip,
           even_norm_w, even_w_out, even_ln_g, even_ln_b, odd_w_in, odd_rel_bias, odd_w_out,
           odd_ln_g, odd_ln_b):
    bp, lp, _ = x_prompt.shape
    bs, ls, _ = x_sample.shape
    w_in0 = even_w_in[0].astype(BF16)
    w_kv_t0 = jnp.transpose(even_w_in[0][:, L0_K:L0_G]).astype(BF16)
    w_dt0 = jnp.pad(w_in0[:, L0_DT:], ((0, 0), (0, LANES - SSD_HEADS)))
    wts0 = (w_in0, w_kv_t0, w_dt0, even_conv_w[0], even_conv_b[0], even_dt_bias[0], even_a_log[0],
            even_d_skip[0], even_norm_w[0], even_w_out[0].astype(BF16), even_ln_g[0], even_ln_b[0])
    zero_conv = jnp.zeros((bp, SSD_CONV - 1, SSD_CONV_DIM), F32)
    zero_h = jnp.zeros((bp, SSD_WIDTH, SSD_STATE), F32)
    yp, ypb, p_sb_k, p_sb_v, p_conv, p_ssm = _even_layer(
        x_prompt, x_prompt.astype(BF16), None, None, zero_conv, zero_h, wts0, 2 * CHUNK)
    ys, ysb, s_sb_k, s_sb_v, s_conv, s_ssm = _even_layer(
        x_sample, x_sample.astype(BF16), cache_sb_k[0], cache_sb_v[0], state_conv[0],
        state_ssm[0].reshape(bs, SSD_WIDTH, SSD_STATE), wts0, CHUNK)
    w_in1 = odd_w_in[0].astype(BF16)
    w_kv_t1 = jnp.transpose(odd_w_in[0][:, CA_WIDTH:3 * CA_WIDTH]).astype(BF16)
    w_out1 = odd_w_out[0].astype(BF16)
    wts1p = (w_in1, w_kv_t1, _band_bias_tiles(odd_rel_bias[0]), w_out1, odd_ln_g[0], odd_ln_b[0])
    wts1s = (w_in1, w_kv_t1, _band_sample_bias(odd_rel_bias[0], ls), w_out1, odd_ln_g[0], odd_ln_b[0])
    yp, p_band_k, p_band_v = _odd_layer(yp, ypb, None, None, wts1p)
    ys, s_band_k, s_band_v = _odd_layer(ys, ysb, cache_band_k[0], cache_band_v[0], wts1s)

    def lead(a):
        return a[None]

    return (yp, ys, lead(p_sb_k), lead(p_sb_v), lead(p_ssm), lead(p_conv), lead(p_band_k),
            lead(p_band_v), lead(s_sb_k), lead(s_sb_v), lead(s_ssm), lead(s_conv), lead(s_band_k),
            lead(s_band_v))
```

```python
import functools
import math

import jax
import jax.numpy as jnp
from jax import lax
from jax.experimental import pallas as pl
from jax.experimental.pallas import tpu as pltpu

F32 = jnp.float32
BF16 = jnp.bfloat16

D_MODEL = 2048
CHUNK = 64
HEAD_DIM = 64
LANES = 128
SB_WIDTH = 1024
SSD_WIDTH = 2048
SSD_HEADS = 32
SSD_GROUPS = 4
SSD_STATE = 128
SSD_CONV = 4
SSD_CONV_DIM = SSD_WIDTH + 2 * SSD_GROUPS * SSD_STATE
CA_WIDTH = 2048
CA_LEFT = 8 * CHUNK
CA_REL_CLIP = 128
DEPTH = 2
DEEPNORM_ALPHA = (2 * DEPTH) ** 0.25
LN_EPS = 1e-5
RMS_EPS = 1e-5
ATTN_SCALE = HEAD_DIM ** -0.5
NEG_BIG = -1e30
L0_Q, L0_K, L0_V, L0_G, L0_Z, L0_XBC, L0_DT = 0, 1024, 2048, 3072, 4096, 6144, 9216
VMEM_LIMIT = 56 * 1024 * 1024
ROW_TILE = 512
COL_TILE = 1024

NT_DIMS = (((1,), (1,)), ((), ()))
TN_DIMS = (((0,), (0,)), ((), ()))


def _params(*sem):
    return pltpu.CompilerParams(dimension_semantics=sem, vmem_limit_bytes=VMEM_LIMIT)


def _silu(x):
    return x * (1.0 / (1.0 + jnp.exp(-x)))


def _split3(x):
    p0 = x.astype(BF16)
    r = x - p0.astype(F32)
    p1 = r.astype(BF16)
    p2 = (r - p1.astype(F32)).astype(BF16)
    return p0, p1, p2


def _mm_kernel(x_ref, w_ref, *refs, has_side):
    x = x_ref[...]
    o_refs = list(refs)
    if has_side:
        ws_ref = o_refs.pop(0)
        os_ref = o_refs.pop()
        os_ref[...] = jnp.dot(x, ws_ref[...], preferred_element_type=F32)
    acc = jnp.dot(x, w_ref[...], preferred_element_type=F32)
    for o_ref in o_refs:
        o_ref[...] = acc.astype(o_ref.dtype)


def _matmul(x, w, col0, ncols, out_dtypes, w_side=None):
    m, k = x.shape
    tm, tn = math.gcd(m, ROW_TILE), COL_TILE
    assert ncols % tn == 0 and col0 % tn == 0
    cb0 = col0 // tn
    in_specs = [pl.BlockSpec((tm, k), lambda j, i: (i, 0)),
                pl.BlockSpec((k, tn), lambda j, i: (0, cb0 + j))]
    out_specs = [pl.BlockSpec((tm, tn), lambda j, i: (i, j)) for _ in out_dtypes]
    out_shape = [jax.ShapeDtypeStruct((m, ncols), dt) for dt in out_dtypes]
    args = [x, w]
    if w_side is not None:
        assert ncols == tn
        ns = w_side.shape[1]
        in_specs.append(pl.BlockSpec((k, ns), lambda j, i: (0, 0)))
        out_specs.append(pl.BlockSpec((tm, ns), lambda j, i: (i, 0)))
        out_shape.append(jax.ShapeDtypeStruct((m, ns), F32))
        args.append(w_side)
    return pl.pallas_call(
        functools.partial(_mm_kernel, has_side=w_side is not None),
        grid=(ncols // tn, m // tm),
        in_specs=in_specs,
        out_specs=out_specs,
        out_shape=out_shape,
        compiler_params=_params("parallel", "arbitrary"),
        name=f"mm_c{col0}_n{ncols}_m{m}",
    )(*args)


def _mm_t_kernel(w_ref, x_ref, ob_ref, of_ref, *, first_kept):
    acc = lax.dot_general(w_ref[...], x_ref[0], NT_DIMS, preferred_element_type=F32)
    ob_ref[0] = acc.astype(ob_ref.dtype)

    def keep_f32():
        of_ref[0] = acc

    if first_kept == 0:
        keep_f32()
    else:
        pl.when(pl.program_id(1) >= first_kept)(keep_f32)


def _matmul_t(w_t, x, row0, nrows, keep):
    s, l, k = x.shape
    tm = math.gcd(l, ROW_TILE)
    assert row0 % nrows == 0 and keep % tm == 0 and keep <= l
    first_kept = (l - keep) // tm
    return pl.pallas_call(
        functools.partial(_mm_t_kernel, first_kept=first_kept),
        grid=(s, l // tm),
        in_specs=[pl.BlockSpec((nrows, k), lambda si, i: (row0 // nrows, 0)),
                  pl.BlockSpec((1, tm, k), lambda si, i: (si, i, 0))],
        out_specs=[pl.BlockSpec((1, nrows, tm), lambda si, i: (si, 0, i)),
                   pl.BlockSpec((1, nrows, tm), lambda si, i: (si, 0, jnp.maximum(i - first_kept, 0)))],
        out_shape=[jax.ShapeDtypeStruct((s, nrows, l), BF16),
                   jax.ShapeDtypeStruct((s, nrows, keep), F32)],
        compiler_params=_params("parallel", "arbitrary"),
        name=f"mm_t_r{row0}_n{nrows}_l{l}",
    )(w_t, x)


def _proj_ln_kernel(*refs, widths):
    n = len(widths)
    o_refs = refs[:n]
    x_ref, w_ref, g_ref, b_ref, y_ref, yb_ref = refs[n:]
    mix = None
    row = 0
    for o_ref, wd in zip(o_refs, widths):
        part = jnp.dot(o_ref[...], w_ref[row:row + wd, :], preferred_element_type=F32)
        mix = part if mix is None else mix + part
        row += wd
    h = DEEPNORM_ALPHA * x_ref[...] + mix
    mu = jnp.mean(h, axis=-1, keepdims=True)
    hc = h - mu
    var = jnp.mean(hc * hc, axis=-1, keepdims=True)
    y = hc * lax.rsqrt(var + LN_EPS) * g_ref[...] + b_ref[...]
    y_ref[...] = y
    yb_ref[...] = y.astype(BF16)


def _proj_ln(o_parts, x, w_out, ln_g, ln_b):
    m = x.shape[0]
    widths = tuple(o.shape[1] for o in o_parts)
    ktot = sum(widths)
    tm = math.gcd(m, ROW_TILE // 2)
    assert w_out.shape == (ktot, D_MODEL)
    in_specs = [pl.BlockSpec((tm, wd), lambda i: (i, 0)) for wd in widths]
    in_specs += [pl.BlockSpec((tm, D_MODEL), lambda i: (i, 0)),
                 pl.BlockSpec((ktot, D_MODEL), lambda i: (0, 0)),
                 pl.BlockSpec((1, D_MODEL), lambda i: (0, 0)),
                 pl.BlockSpec((1, D_MODEL), lambda i: (0, 0))]
    y, yb = pl.pallas_call(
        functools.partial(_proj_ln_kernel, widths=widths),
        grid=(m // tm,),
        in_specs=in_specs,
        out_specs=[pl.BlockSpec((tm, D_MODEL), lambda i: (i, 0)),
                   pl.BlockSpec((tm, D_MODEL), lambda i: (i, 0))],
        out_shape=[jax.ShapeDtypeStruct((m, D_MODEL), F32),
                   jax.ShapeDtypeStruct((m, D_MODEL), BF16)],
        compiler_params=_params("parallel"),
        name=f"proj_ln_k{ktot}_m{m}",
    )(*o_parts, x, w_out, ln_g.reshape(1, D_MODEL), ln_b.reshape(1, D_MODEL))
    return y, yb


def _head_masks(rows):
    lane = lax.broadcasted_iota(jnp.int32, (rows, LANES), 1)
    return lane < HEAD_DIM


SB_KEYS = 256
SB_STEP_LANES = 2 * LANES
SB_DEAD = 104.5


def _pair_queries(q2):
    head0 = _head_masks(q2.shape[0])
    qs = q2 * jnp.asarray(ATTN_SCALE, q2.dtype)
    zero = jnp.zeros_like(qs)
    return jnp.concatenate([jnp.where(head0, qs, zero), jnp.where(head0, zero, qs)], axis=0)


def _pair_outputs(stacked):
    tq = stacked.shape[0] // 2
    return jnp.where(_head_masks(tq), stacked[:tq], stacked[tq:])


def _sb_visit(qh, k_pairs, v_pairs, carries, accs, upper, diag_mask):
    zs = [jnp.dot(q2, kt, preferred_element_type=F32) for q2, kt in zip(qh, k_pairs)]
    sps = [jnp.maximum(z, 0.0) + jnp.log(1.0 + jnp.exp(-jnp.abs(z))) for z in zs]
    if diag_mask is not None:
        sps = [jnp.where(diag_mask, sp, 0.0) for sp in sps]
    his = [sp.astype(BF16) for sp in sps]
    los = [(sp - hi.astype(F32)).astype(BF16) for sp, hi in zip(sps, his)]
    masses = [(jnp.dot(hi, upper, preferred_element_type=F32)
               + jnp.dot(lo, upper, preferred_element_type=F32)) + c
              for hi, lo, c in zip(his, los, carries)]
    ws = [jnp.exp(z - mass) for z, mass in zip(zs, masses)]
    if diag_mask is not None:
        ws = [jnp.where(diag_mask, w, 0.0) for w in ws]
    accs = tuple(a + lax.dot_general(w.astype(BF16), vt, NT_DIMS, preferred_element_type=F32)
                 for a, w, vt in zip(accs, ws, v_pairs))
    return tuple(mass[:, 0:1] for mass in masses), accs


def _sb_upper(n):
    r = lax.broadcasted_iota(jnp.int32, (n, n), 0)
    c = lax.broadcasted_iota(jnp.int32, (n, n), 1)
    return (r >= c).astype(BF16)


def _sb_alive(carries):
    return jnp.min(functools.reduce(jnp.minimum, carries)) < SB_DEAD


def _sb_queries(q_ref):
    return [_pair_queries(q_ref[0, :, p * LANES:(p + 1) * LANES]) for p in range(q_ref.shape[2] // LANES)]


def _sb_strict(tq, keys):
    r = lax.broadcasted_iota(jnp.int32, (2 * tq, keys), 0)
    c = lax.broadcasted_iota(jnp.int32, (2 * tq, keys), 1)
    return c < jnp.where(r >= tq, r - tq, r)


def _sb_finish(accs, g_ref, o_ref):
    for p, acc in enumerate(accs):
        lanes = slice(p * LANES, (p + 1) * LANES)
        o_ref[0, :, lanes] = (_pair_outputs(acc) * _silu(g_ref[0, :, lanes])).astype(o_ref.dtype)


def _pair_slabs(ref, lead, n_pair, cols):
    return [ref[lead, p * LANES:(p + 1) * LANES, cols].astype(BF16) for p in range(n_pair)]


def _sb_prompt_kernel(q_ref, k_ref, v_ref, g_ref, o_ref):
    qi = pl.program_id(2)
    tq = q_ref.shape[1]
    assert tq == SB_KEYS
    qh = _sb_queries(q_ref)
    n_pair = len(qh)
    upper = _sb_upper(SB_KEYS)
    strict = _sb_strict(tq, SB_KEYS)

    def visit(kb, carries, accs, mask):
        cols = pl.ds(pl.multiple_of(kb * SB_KEYS, SB_KEYS), SB_KEYS)
        return _sb_visit(qh, _pair_slabs(k_ref, 0, n_pair, cols), _pair_slabs(v_ref, 0, n_pair, cols),
                         carries, accs, upper, mask)

    zc = (jnp.zeros((2 * tq, 1), F32),) * n_pair
    za = (jnp.zeros((2 * tq, LANES), F32),) * n_pair
    carries, accs = visit(qi, zc, za, strict)

    def older(state):
        kb, _, carries, accs = state
        carries, accs = visit(kb, carries, accs, None)
        return kb - 1, _sb_alive(carries), carries, accs

    state = lax.while_loop(lambda s: jnp.logical_and(s[0] >= 0, s[1]), older,
                           (qi - 1, _sb_alive(carries), carries, accs))
    _sb_finish(state[3], g_ref, o_ref)


def _sb_prompt(q, kt, vt, g):
    b, l, wdt = q.shape
    tq = SB_KEYS
    assert l % tq == 0 and wdt % SB_STEP_LANES == 0
    blk = pl.BlockSpec((1, tq, SB_STEP_LANES), lambda bi, hp, qi: (bi, qi, hp))
    full = pl.BlockSpec((1, SB_STEP_LANES, l), lambda bi, hp, qi: (bi, hp, 0))
    return pl.pallas_call(
        _sb_prompt_kernel,
        grid=(b, wdt // SB_STEP_LANES, l // tq),
        in_specs=[blk, full, full, blk],
        out_specs=blk,
        out_shape=jax.ShapeDtypeStruct((b, l, wdt), BF16),
        compiler_params=_params("parallel", "parallel", "arbitrary"),
        name="sb_prompt",
    )(q, kt, vt, g)


def _sb_sample_kernel(q_ref, kn_ref, vn_ref, g_ref, kc_hbm, vc_hbm, o_ref, kbuf, vbuf, sem, *, n_cached):
    seq = pl.program_id(0)
    tq = q_ref.shape[1]
    qh = _sb_queries(q_ref)
    n_pair = len(qh)
    upper = _sb_upper(SB_KEYS)
    strict = _sb_strict(tq, LANES)
    zc = (jnp.zeros((2 * tq, 1), F32),) * n_pair
    za = (jnp.zeros((2 * tq, LANES), F32),) * n_pair

    def block_copies(kb, slot):
        cols = pl.ds(pl.multiple_of(kb * SB_KEYS, SB_KEYS), SB_KEYS)
        return (pltpu.make_async_copy(kc_hbm.at[seq, :, cols], kbuf.at[slot], sem.at[0, slot]),
                pltpu.make_async_copy(vc_hbm.at[seq, :, cols], vbuf.at[slot], sem.at[1, slot]))

    def start_block(kb):
        for cp in block_copies(kb, kb % 2):
            cp.start()

    def wait_block(kb):
        for cp in block_copies(kb, kb % 2):
            cp.wait()

    start_block(n_cached - 1)
    everything = slice(None)
    carries, accs = _sb_visit(qh, _pair_slabs(kn_ref, 0, n_pair, everything),
                              _pair_slabs(vn_ref, 0, n_pair, everything), zc, za, _sb_upper(LANES), strict)

    def older(state):
        kb, _, carries, accs = state
        wait_block(kb)
        pl.when(kb > 0)(lambda: start_block(kb - 1))
        slot = kb % 2
        carries, accs = _sb_visit(qh, _pair_slabs(kbuf, slot, n_pair, everything),
                                  _pair_slabs(vbuf, slot, n_pair, everything), carries, accs, upper, None)
        return kb - 1, _sb_alive(carries), carries, accs

    state = lax.while_loop(lambda s: jnp.logical_and(s[0] >= 0, s[1]), older,
                           (jnp.int32(n_cached - 1), _sb_alive(carries), carries, accs))
    pl.when(state[0] >= 0)(lambda: wait_block(state[0]))
    _sb_finish(state[3], g_ref, o_ref)


def _sb_sample(q, knt, vnt, g, kct, vct):
    b, t, wdt = q.shape
    p = kct.shape[2]
    assert p % SB_KEYS == 0 and t <= LANES and knt.shape == (b, wdt, LANES)
    tok = pl.BlockSpec((1, t, wdt), lambda bi: (bi, 0, 0))
    new = pl.BlockSpec((1, wdt, LANES), lambda bi: (bi, 0, 0))
    hbm = pl.BlockSpec(memory_space=pl.ANY)
    return pl.pallas_call(
        functools.partial(_sb_sample_kernel, n_cached=p // SB_KEYS),
        grid=(b,),
        in_specs=[tok, new, new, tok, hbm, hbm],
        out_specs=tok,
        out_shape=jax.ShapeDtypeStruct((b, t, wdt), BF16),
        scratch_shapes=[pltpu.VMEM((2, wdt, SB_KEYS), F32),
                        pltpu.VMEM((2, wdt, SB_KEYS), F32),
                        pltpu.SemaphoreType.DMA((2, 2))],
        compiler_params=_params("arbitrary"),
        name="sb_sample",
    )(q, knt, vnt, g, kct, vct)


CONV_PAD = 8
CONV_SLAB = 512


def _ssd_kernel(xbc_ref, z_ref, dt_ref, cs_ref, h0_ref, cw_ref, cb_ref, dtb_ref, alog_ref, drow_ref,
                nw_ref, o_ref, h_ref, xp_ref, xa_ref, y_ref):
    ci = pl.program_id(1)
    q = xbc_ref.shape[1]
    taps = SSD_CONV - 1

    @pl.when(ci == 0)
    def _():
        xp_ref[CONV_PAD - taps:CONV_PAD, :] = cs_ref[0]
        h_ref[0] = h0_ref[0]

    @pl.when(ci > 0)
    def _():
        xp_ref[CONV_PAD - taps:CONV_PAD, :] = xp_ref[CONV_PAD + q - taps:CONV_PAD + q, :]

    xp_ref[CONV_PAD:CONV_PAD + q, :] = xbc_ref[0]
    for cs in range(0, SSD_CONV_DIM, CONV_SLAB):
        acc = cb_ref[:, cs:cs + CONV_SLAB] + cw_ref[0:1, cs:cs + CONV_SLAB] * xp_ref[
            CONV_PAD - taps:CONV_PAD - taps + q, cs:cs + CONV_SLAB]
        for t in range(1, SSD_CONV):
            acc = acc + cw_ref[t:t + 1, cs:cs + CONV_SLAB] * xp_ref[
                CONV_PAD - taps + t:CONV_PAD - taps + t + q, cs:cs + CONV_SLAB]
        xa_ref[:, cs:cs + CONV_SLAB] = _silu(acc)

    dtv = dt_ref[0] + dtb_ref[...]
    dtv = jnp.maximum(dtv, 0.0) + jnp.log1p(jnp.exp(-jnp.abs(dtv)))
    da = dtv * (-jnp.exp(alog_ref[...]))
    r = lax.broadcasted_iota(jnp.int32, (q, q), 0)
    c = lax.broadcasted_iota(jnp.int32, (q, q), 1)
    causal = c <= r
    tril = causal.astype(BF16)
    cum = sum(jnp.dot(tril, p, preferred_element_type=F32) for p in _split3(da))
    er = lax.broadcasted_iota(jnp.int32, (LANES, LANES), 0)
    ec = lax.broadcasted_iota(jnp.int32, (LANES, LANES), 1)
    eye = (er == ec).astype(BF16)

    def transpose(x):
        return sum(lax.dot_general(eye, p, NT_DIMS, preferred_element_type=F32) for p in _split3(x))

    cum_t = transpose(cum)
    dt_t = transpose(dtv)
    cum_last = cum[q - 1:q, :]
    to_end = jnp.exp(cum_last - cum) * dtv
    exp_cum = jnp.exp(cum)
    chunk_decay = jnp.broadcast_to(jnp.exp(cum_last), (LANES, LANES))
    head0 = _head_masks(q)
    top = lax.broadcasted_iota(jnp.int32, (LANES, LANES), 0) < HEAD_DIM

    for g in range(SSD_GROUPS):
        b0 = SSD_WIDTH + g * SSD_STATE
        c0 = SSD_WIDTH + SSD_GROUPS * SSD_STATE + g * SSD_STATE
        bm = xa_ref[:, b0:b0 + SSD_STATE].astype(BF16)
        cm = xa_ref[:, c0:c0 + SSD_STATE].astype(BF16)
        cb = lax.dot_general(cm, bm, NT_DIMS, preferred_element_type=F32)
        for j in range(SSD_HEADS // SSD_GROUPS // 2):
            pr = g * (SSD_HEADS // SSD_GROUPS // 2) + j
            h0, h1 = 2 * pr, 2 * pr + 1
            lo = pr * LANES
            x2 = xa_ref[:, lo:lo + LANES]
            x2b = x2.astype(BF16)
            yd = []
            for h in (h0, h1):
                seg = cum[:, h:h + 1] - cum_t[h:h + 1, :]
                wgt = cb * jnp.exp(jnp.where(causal, seg, -jnp.inf)) * dt_t[h:h + 1, :]
                yd.append(jnp.dot(wgt.astype(BF16), x2b, preferred_element_type=F32))
            y = jnp.where(head0, yd[0], yd[1])
            hs = h_ref[0, lo:lo + LANES, :]
            yoff = lax.dot_general(cm, hs.astype(BF16), NT_DIMS, preferred_element_type=F32)
            y = y + yoff * jnp.where(head0, exp_cum[:, h0:h0 + 1], exp_cum[:, h1:h1 + 1])
            y_ref[:, lo:lo + LANES] = y + drow_ref[:, lo:lo + LANES] * x2
            xw = x2 * jnp.where(head0, to_end[:, h0:h0 + 1], to_end[:, h1:h1 + 1])
            st = lax.dot_general(xw.astype(BF16), bm, TN_DIMS, preferred_element_type=F32)
            dec = jnp.where(top, chunk_decay[:, h0:h0 + 1], chunk_decay[:, h1:h1 + 1])
            h_ref[0, lo:lo + LANES, :] = dec * hs + st

    gw = SSD_WIDTH // SSD_GROUPS
    for g in range(SSD_GROUPS):
        yg = y_ref[:, g * gw:(g + 1) * gw] * _silu(z_ref[0, :, g * gw:(g + 1) * gw])
        ms = jnp.mean(yg * yg, axis=-1, keepdims=True)
        o_ref[0, :, g * gw:(g + 1) * gw] = (yg * lax.rsqrt(ms + RMS_EPS)
                                            * nw_ref[:, g * gw:(g + 1) * gw]).astype(o_ref.dtype)


def _ssd(xbc, z, dt_raw, conv_state, h0, conv_w, conv_b, dt_bias, a_log, d_skip, norm_w, q):
    b, l, _ = xbc.shape
    assert l % q == 0
    pad = LANES - SSD_HEADS
    dtb = jnp.pad(dt_bias, (0, pad)).reshape(1, LANES)
    alog = jnp.pad(a_log, (0, pad)).reshape(1, LANES)
    drow = jnp.repeat(d_skip, HEAD_DIM).reshape(1, SSD_WIDTH)

    def tok(wd):
        return pl.BlockSpec((1, q, wd), lambda bi, ci: (bi, ci, 0))

    def per_seq(shape):
        return pl.BlockSpec((1,) + shape, lambda bi, ci: (bi, 0, 0))

    def const(shape):
        return pl.BlockSpec(shape, lambda bi, ci: (0, 0))

    o, h = pl.pallas_call(
        _ssd_kernel,
        grid=(b, l // q),
        in_specs=[tok(SSD_CONV_DIM), tok(SSD_WIDTH), tok(LANES),
                  per_seq((SSD_CONV - 1, SSD_CONV_DIM)), per_seq((SSD_WIDTH, SSD_STATE)),
                  const((SSD_CONV, SSD_CONV_DIM)), const((1, SSD_CONV_DIM)), const((1, LANES)),
                  const((1, LANES)), const((1, SSD_WIDTH)), const((1, SSD_WIDTH))],
        out_specs=[tok(SSD_WIDTH), per_seq((SSD_WIDTH, SSD_STATE))],
        out_shape=[jax.ShapeDtypeStruct((b, l, SSD_WIDTH), BF16),
                   jax.ShapeDtypeStruct((b, SSD_WIDTH, SSD_STATE), F32)],
        scratch_shapes=[pltpu.VMEM((CONV_PAD + q, SSD_CONV_DIM), F32),
                        pltpu.VMEM((q, SSD_CONV_DIM), F32),
                        pltpu.VMEM((q, SSD_WIDTH), F32)],
        compiler_params=_params("parallel", "arbitrary"),
        name=f"ssd_q{q}",
    )(xbc, z, dt_raw, conv_state, h0, conv_w, conv_b.reshape(1, SSD_CONV_DIM), dtb, alog, drow,
      norm_w.reshape(1, SSD_WIDTH))
    return o, h


BAND_BLOCKS = CA_LEFT // LANES + 1


def _band_attend(jobs):
    scores = [jnp.dot(_pair_queries(q2), kt, preferred_element_type=F32) + bias2
              for q2, kt, _, bias2, _ in jobs]
    probs = [jnp.exp(s - jnp.max(s, axis=-1, keepdims=True)) for s in scores]
    dens = [jnp.sum(e, axis=-1, keepdims=True) for e in probs]
    nums = [lax.dot_general(e.astype(BF16), job[2], NT_DIMS, preferred_element_type=F32)
            for e, job in zip(probs, jobs)]
    return [_pair_outputs(num / den) * _silu(job[4]) for num, den, job in zip(nums, dens, jobs)]


def _band_prompt_kernel(q_ref, k_ref, v_ref, g_ref, bias_ref, o_ref):
    step = pl.program_id(2)
    n_sub = q_ref.shape[1] // LANES
    band = BAND_BLOCKS * LANES

    def rows(s):
        return slice(s * LANES, (s + 1) * LANES)

    def sub_tile(s, i):
        if isinstance(i, int):
            keys = min(i + 1, BAND_BLOCKS) * LANES
            cols = pl.ds((i + 1) * LANES - keys, keys)
        else:
            keys = band
            cols = pl.ds(pl.multiple_of((i + 1) * LANES - band, LANES), band)
        bias2 = jnp.concatenate([bias_ref[0, :, band - keys:], bias_ref[1, :, band - keys:]], axis=0)
        return q_ref[0, rows(s), :], k_ref[0, :, cols], v_ref[0, :, cols], bias2, g_ref[0, rows(s), :]

    def step_tiles(first):
        for s, o in enumerate(_band_attend([sub_tile(s, first + s) for s in range(n_sub)])):
            o_ref[0, rows(s), :] = o.astype(o_ref.dtype)

    pl.when(step == 0)(lambda: step_tiles(0))
    pl.when(step > 0)(lambda: step_tiles(step * n_sub))


def _bias_lookup(rel_bias, t_minus_s):
    rel = jnp.clip(t_minus_s, -CA_REL_CLIP, CA_REL_CLIP) + CA_REL_CLIP
    onehot = (rel[..., None] == jnp.arange(2 * CA_REL_CLIP + 1)).astype(F32)
    return jnp.einsum("rck,hk->hrc", onehot, rel_bias, precision=lax.Precision.HIGHEST)


def _band_bias(rel_bias):
    r = jnp.arange(LANES)[:, None]
    u = jnp.arange(LANES)[None, :]

    def tile(dist):
        return _bias_lookup(rel_bias, dist * LANES + r - u)

    t0 = jnp.where((r < CHUNK) & (u >= CHUNK), NEG_BIG, tile(0))
    t1 = tile(1)
    t2 = jnp.broadcast_to(rel_bias[:, -1][:, None, None], t1.shape)
    t4 = jnp.where((r >= CHUNK) & (u < CHUNK), NEG_BIG, t2)
    return jnp.concatenate([t4, t2, t2, t1, t0], axis=-1).astype(F32)


def _band_prompt(q, kt, vt, g, band_bias):
    b, l, wdt = q.shape
    n_sub = math.gcd(l // LANES, BAND_BLOCKS - 1)
    assert n_sub == BAND_BLOCKS - 1 or l // LANES == n_sub
    tq = n_sub * LANES
    blk = pl.BlockSpec((1, tq, LANES), lambda bi, hp, i: (bi, i, hp))
    full = pl.BlockSpec((1, LANES, l), lambda bi, hp, i: (bi, hp, 0))
    bias = pl.BlockSpec((2, LANES, BAND_BLOCKS * LANES), lambda bi, hp, i: (hp, 0, 0))
    return pl.pallas_call(
        _band_prompt_kernel,
        grid=(b, wdt // LANES, l // tq),
        in_specs=[blk, full, full, blk, bias],
        out_specs=blk,
        out_shape=jax.ShapeDtypeStruct((b, l, wdt), BF16),
        compiler_params=_params("parallel", "parallel", "arbitrary"),
        name="band_prompt",
    )(q, kt, vt, g, band_bias)


BAND_SAMPLE_PAIRS = 4


def _band_sample_kernel(q_ref, kn_ref, vn_ref, kc_ref, vc_ref, g_ref, bias_ref, o_ref):
    def lanes(p):
        return slice(p * LANES, (p + 1) * LANES)

    def pair(p):
        kt = jnp.concatenate([kc_ref[0, lanes(p), :].astype(BF16), kn_ref[0, lanes(p), :]], axis=1)
        vt = jnp.concatenate([vc_ref[0, lanes(p), :].astype(BF16), vn_ref[0, lanes(p), :]], axis=1)
        bias2 = jnp.concatenate([bias_ref[2 * p], bias_ref[2 * p + 1]], axis=0)
        return q_ref[0, :, lanes(p)], kt, vt, bias2, g_ref[0, :, lanes(p)]

    for p, o in enumerate(_band_attend([pair(p) for p in range(q_ref.shape[2] // LANES)])):
        o_ref[0, :, lanes(p)] = o.astype(o_ref.dtype)


def _band_sample_bias(rel_bias, tq):
    qi = jnp.arange(tq)[:, None]
    kj = jnp.arange(CA_LEFT + LANES)[None, :]
    return jnp.where(kj < CA_LEFT + tq, _bias_lookup(rel_bias, qi - kj + CA_LEFT), NEG_BIG).astype(F32)


def _band_sample(q, knt, vnt, kct, vct, g, bias):
    b, t, wdt = q.shape
    p = kct.shape[2]
    step_lanes = BAND_SAMPLE_PAIRS * LANES
    assert p == CA_LEFT and t == CHUNK and wdt % step_lanes == 0
    tok = pl.BlockSpec((1, t, step_lanes), lambda bi, hp: (bi, 0, hp))
    new = pl.BlockSpec((1, step_lanes, LANES), lambda bi, hp: (bi, hp, 0))
    old = pl.BlockSpec((1, step_lanes, p), lambda bi, hp: (bi, hp, 0))
    bspec = pl.BlockSpec((2 * BAND_SAMPLE_PAIRS, t, CA_LEFT + LANES), lambda bi, hp: (hp, 0, 0))
    return pl.pallas_call(
        _band_sample_kernel,
        grid=(b, wdt // step_lanes),
        in_specs=[tok, new, new, old, old, tok, bspec],
        out_specs=tok,
        out_shape=jax.ShapeDtypeStruct((b, t, wdt), BF16),
        compiler_params=_params("parallel", "parallel"),
        name="band_sample",
    )(q, knt, vnt, kct, vct, g, bias)


def _feature_major(cache):
    b, p, h, d = cache.shape
    return jnp.transpose(cache, (0, 2, 3, 1)).reshape(b, h * d, p)


def _token_major(kt, heads):
    b, _, t = kt.shape
    return jnp.transpose(kt.reshape(b, heads, HEAD_DIM, t), (0, 3, 1, 2))


def _pad_keys(kt):
    return jnp.pad(kt, ((0, 0), (0, 0), (0, LANES - kt.shape[2])))


def _even_layer(x, xb, past_k, past_v, conv_state, h0, wts, ssd_chunk):
    w_in, w_kv_t, w_dt, conv_w, conv_b, dt_bias, a_log, d_skip, norm_w, w_out, ln_g, ln_b = wts
    b, l, _ = x.shape
    m = b * l
    x2 = x.reshape(m, D_MODEL)
    xb2 = xb.reshape(m, D_MODEL)
    (qb,) = _matmul(xb2, w_in, L0_Q, SB_WIDTH, (BF16,))
    kt_b, kt = _matmul_t(w_kv_t, xb, 0, SB_WIDTH, l)
    vt_b, vt = _matmul_t(w_kv_t, xb, SB_WIDTH, SB_WIDTH, l)
    ga, dt_raw = _matmul(xb2, w_in, L0_G, SB_WIDTH, (F32,), w_side=w_dt)
    (z,) = _matmul(xb2, w_in, L0_Z, SSD_WIDTH, (F32,))
    (xbc,) = _matmul(xb2, w_in, L0_XBC, SSD_CONV_DIM, (F32,))

    def seq(a):
        return a.reshape(b, l, a.shape[-1])

    if past_k is None:
        o_a = _sb_prompt(seq(qb), kt_b, vt_b, seq(ga))
    else:
        o_a = _sb_sample(seq(qb), _pad_keys(kt_b), _pad_keys(vt_b), seq(ga), _feature_major(past_k),
                         _feature_major(past_v))
    xbc3 = seq(xbc)
    o_b, h_new = _ssd(xbc3, seq(z), seq(dt_raw), conv_state, h0, conv_w, conv_b, dt_bias, a_log,
                      d_skip, norm_w, ssd_chunk)
    y, yb = _proj_ln([o_a.reshape(m, SB_WIDTH), o_b.reshape(m, SSD_WIDTH)], x2, w_out, ln_g, ln_b)
    tail = jnp.concatenate([conv_state, xbc3], axis=1)[:, -(SSD_CONV - 1):]
    heads = SB_WIDTH // HEAD_DIM
    return (seq(y), seq(yb), _token_major(kt, heads), _token_major(vt, heads), tail,
            h_new.reshape(b, SSD_HEADS, HEAD_DIM, SSD_STATE))


def _odd_layer(x, xb, past_k, past_v, wts):
    w_in, w_kv_t, bias, w_out, ln_g, ln_b = wts
    b, l, _ = x.shape
    m = b * l
    x2 = x.reshape(m, D_MODEL)
    xb2 = xb.reshape(m, D_MODEL)
    keep = min(CA_LEFT, l)
    (qb,) = _matmul(xb2, w_in, 0, CA_WIDTH, (BF16,))
    kt_b, kt = _matmul_t(w_kv_t, xb, 0, CA_WIDTH, keep)
    vt_b, vt = _matmul_t(w_kv_t, xb, CA_WIDTH, CA_WIDTH, keep)
    (g,) = _matmul(xb2, w_in, 3 * CA_WIDTH, CA_WIDTH, (F32,))

    def seq(a):
        return a.reshape(b, l, a.shape[-1])

    if past_k is None:
        o = _band_prompt(seq(qb), kt_b, vt_b, seq(g), bias)
    else:
        past_kt, past_vt = _feature_major(past_k), _feature_major(past_v)
        o = _band_sample(seq(qb), _pad_keys(kt_b), _pad_keys(vt_b), past_kt, past_vt, seq(g), bias)
        kt = jnp.concatenate([past_kt, kt], axis=2)[:, :, -CA_LEFT:]
        vt = jnp.concatenate([past_vt, vt], axis=2)[:, :, -CA_LEFT:]
    y, _ = _proj_ln([o.reshape(m, CA_WIDTH)], x2, w_out, ln_g, ln_b)
    heads = CA_WIDTH // HEAD_DIM
    return seq(y), _token_major(kt, heads), _token_major(vt, heads)


def kernel(x_prompt, x_sample, cache_sb_k, cache_sb_v, state_ssm, state_conv, cache_band_k,
           cache_band_v, even_w_in, even_conv_w, even_conv_b, even_dt_bias, even_a_log, even_d_skip,
           even_norm_w, even_w_out, even_ln_g, even_ln_b, odd_w_in, odd_rel_bias, odd_w_out,
           odd_ln_g, odd_ln_b):
    bp, lp, _ = x_prompt.shape
    bs, ls, _ = x_sample.shape
    w_in0 = even_w_in[0].astype(BF16)
    w_kv_t0 = jnp.transpose(even_w_in[0][:, L0_K:L0_G]).astype(BF16)
    w_dt0 = jnp.pad(w_in0[:, L0_DT:], ((0, 0), (0, LANES - SSD_HEADS)))
    wts0 = (w_in0, w_kv_t0, w_dt0, even_conv_w[0], even_conv_b[0], even_dt_bias[0], even_a_log[0],
            even_d_skip[0], even_norm_w[0], even_w_out[0].astype(BF16), even_ln_g[0], even_ln_b[0])
    zero_conv = jnp.zeros((bp, SSD_CONV - 1, SSD_CONV_DIM), F32)
    zero_h = jnp.zeros((bp, SSD_WIDTH, SSD_STATE), F32)
    yp, ypb, p_sb_k, p_sb_v, p_conv, p_ssm = _even_layer(
        x_prompt, x_prompt.astype(BF16), None, None, zero_conv, zero_h, wts0, 2 * CHUNK)
    ys, ysb, s_sb_k, s_sb_v, s_conv, s_ssm = _even_layer(
        x_sample, x_sample.astype(BF16), cache_sb_k[0], cache_sb_v[0], state_conv[0],
        state_ssm[0].reshape(bs, SSD_WIDTH, SSD_STATE), wts0, CHUNK)
    w_in1 = odd_w_in[0].astype(BF16)
    w_kv_t1 = jnp.transpose(odd_w_in[0][:, CA_WIDTH:3 * CA_WIDTH]).astype(BF16)
    w_out1 = odd_w_out[0].astype(BF16)
    wts1p = (w_in1, w_kv_t1, _band_bias(odd_rel_bias[0]), w_out1, odd_ln_g[0], odd_ln_b[0])
    wts1s = (w_in1, w_kv_t1, _band_sample_bias(odd_rel_bias[0], ls), w_out1, odd_ln_g[0], odd_ln_b[0])
    yp, p_band_k, p_band_v = _odd_layer(yp, ypb, None, None, wts1p)
    ys, s_band_k, s_band_v = _odd_layer(ys, ysb, cache_band_k[0], cache_band_v[0], wts1s)

    def lead(a):
        return a[None]

    return (yp, ys, lead(p_sb_k), lead(p_sb_v), lead(p_ssm), lead(p_conv), lead(p_band_k),
            lead(p_band_v), lead(s_sb_k), lead(s_sb_v), lead(s_ssm), lead(s_conv), lead(s_band_k),
            lead(s_band_v))
```

```python
import functools
import math

import jax
import jax.numpy as jnp
from jax import lax
from jax.experimental import pallas as pl
from jax.experimental.pallas import tpu as pltpu

F32 = jnp.float32
BF16 = jnp.bfloat16

D_MODEL = 2048
CHUNK = 64
HEAD_DIM = 64
LANES = 128
SB_WIDTH = 1024
SSD_WIDTH = 2048
SSD_HEADS = 32
SSD_GROUPS = 4
SSD_STATE = 128
SSD_CONV = 4
SSD_CONV_DIM = SSD_WIDTH + 2 * SSD_GROUPS * SSD_STATE
CA_WIDTH = 2048
CA_LEFT = 8 * CHUNK
CA_REL_CLIP = 128
DEPTH = 2
DEEPNORM_ALPHA = (2 * DEPTH) ** 0.25
LN_EPS = 1e-5
RMS_EPS = 1e-5
ATTN_SCALE = HEAD_DIM ** -0.5
NEG_BIG = -1e30
L0_Q, L0_K, L0_V, L0_G, L0_Z, L0_XBC, L0_DT = 0, 1024, 2048, 3072, 4096, 6144, 9216
VMEM_LIMIT = 56 * 1024 * 1024
ROW_TILE = 512
COL_TILE = 1024

NT_DIMS = (((1,), (1,)), ((), ()))
TN_DIMS = (((0,), (0,)), ((), ()))


def _params(*sem):
    return pltpu.CompilerParams(dimension_semantics=sem, vmem_limit_bytes=VMEM_LIMIT)


def _silu(x):
    return x * (1.0 / (1.0 + jnp.exp(-x)))


def _split3(x):
    p0 = x.astype(BF16)
    r = x - p0.astype(F32)
    p1 = r.astype(BF16)
    p2 = (r - p1.astype(F32)).astype(BF16)
    return p0, p1, p2


def _mm_kernel(x_ref, w_ref, *refs, has_side, w_is_t):
    *o_refs, wb_ref = refs

    @pl.when(pl.program_id(1) == 0)
    def _():
        wb_ref[...] = w_ref[...].astype(BF16)

    def product(w):
        if w_is_t:
            return lax.dot_general(x, w, NT_DIMS, preferred_element_type=F32)
        return jnp.dot(x, w, preferred_element_type=F32)

    x = x_ref[...]
    if has_side:
        ws_ref = o_refs.pop(0)
        os_ref = o_refs.pop()
        os_ref[...] = product(ws_ref[...])
    acc = product(wb_ref[...])
    for o_ref in o_refs:
        o_ref[...] = acc.astype(o_ref.dtype)


def _matmul(x, w, col0, ncols, out_dtypes, w_side=None, w_is_t=False):
    m, k = x.shape
    tm, tn = math.gcd(m, ROW_TILE), COL_TILE
    assert ncols % tn == 0 and col0 % tn == 0
    cb0 = col0 // tn
    if w_is_t:
        w_spec = pl.BlockSpec((tn, k), lambda j, i: (cb0 + j, 0))
    else:
        w_spec = pl.BlockSpec((k, tn), lambda j, i: (0, cb0 + j))
    in_specs = [pl.BlockSpec((tm, k), lambda j, i: (i, 0)), w_spec]
    out_specs = [pl.BlockSpec((tm, tn), lambda j, i: (i, j)) for _ in out_dtypes]
    out_shape = [jax.ShapeDtypeStruct((m, ncols), dt) for dt in out_dtypes]
    args = [x, w]
    if w_side is not None:
        assert ncols == tn
        ns = w_side.shape[0 if w_is_t else 1]
        in_specs.append(pl.BlockSpec(w_side.shape, lambda j, i: (0, 0)))
        out_specs.append(pl.BlockSpec((tm, ns), lambda j, i: (i, 0)))
        out_shape.append(jax.ShapeDtypeStruct((m, ns), F32))
        args.append(w_side)
    return pl.pallas_call(
        functools.partial(_mm_kernel, has_side=w_side is not None, w_is_t=w_is_t),
        grid=(ncols // tn, m // tm),
        in_specs=in_specs,
        out_specs=out_specs,
        out_shape=out_shape,
        scratch_shapes=[pltpu.VMEM(w_spec.block_shape, BF16)],
        compiler_params=_params("arbitrary", "arbitrary"),
        name=f"mm_c{col0}_n{ncols}_m{m}",
    )(*args)


def _mm_t_kernel(w_ref, x_ref, ob_ref, of_ref, wb_ref, *, first_kept):
    @pl.when(jnp.logical_and(pl.program_id(0) == 0, pl.program_id(1) == 0))
    def _():
        wb_ref[...] = w_ref[...].astype(BF16)

    acc = lax.dot_general(wb_ref[...], x_ref[0], NT_DIMS, preferred_element_type=F32)
    ob_ref[0] = acc.astype(ob_ref.dtype)

    def keep_f32():
        of_ref[0] = acc

    if first_kept == 0:
        keep_f32()
    else:
        pl.when(pl.program_id(1) >= first_kept)(keep_f32)


def _matmul_t(w_t, x, row0, nrows, keep):
    s, l, k = x.shape
    tm = math.gcd(l, ROW_TILE)
    assert row0 % nrows == 0 and keep % tm == 0 and keep <= l
    first_kept = (l - keep) // tm
    return pl.pallas_call(
        functools.partial(_mm_t_kernel, first_kept=first_kept),
        grid=(s, l // tm),
        in_specs=[pl.BlockSpec((nrows, k), lambda si, i: (row0 // nrows, 0)),
                  pl.BlockSpec((1, tm, k), lambda si, i: (si, i, 0))],
        out_specs=[pl.BlockSpec((1, nrows, tm), lambda si, i: (si, 0, i)),
                   pl.BlockSpec((1, nrows, tm), lambda si, i: (si, 0, jnp.maximum(i - first_kept, 0)))],
        out_shape=[jax.ShapeDtypeStruct((s, nrows, l), BF16),
                   jax.ShapeDtypeStruct((s, nrows, keep), F32)],
        scratch_shapes=[pltpu.VMEM((nrows, k), BF16)],
        compiler_params=_params("arbitrary", "arbitrary"),
        name=f"mm_t_r{row0}_n{nrows}_l{l}",
    )(w_t, x)


def _proj_ln_kernel(*refs, widths):
    n = len(widths)
    o_refs = refs[:n]
    x_ref, w_ref, g_ref, b_ref, y_ref, yb_ref = refs[n:]
    mix = None
    row = 0
    for o_ref, wd in zip(o_refs, widths):
        part = jnp.dot(o_ref[...], w_ref[row:row + wd, :], preferred_element_type=F32)
        mix = part if mix is None else mix + part
        row += wd
    h = DEEPNORM_ALPHA * x_ref[...] + mix
    mu = jnp.mean(h, axis=-1, keepdims=True)
    hc = h - mu
    var = jnp.mean(hc * hc, axis=-1, keepdims=True)
    y = hc * lax.rsqrt(var + LN_EPS) * g_ref[...] + b_ref[...]
    y_ref[...] = y
    yb_ref[...] = y.astype(BF16)


def _proj_ln(o_parts, x, w_out, ln_g, ln_b):
    m = x.shape[0]
    widths = tuple(o.shape[1] for o in o_parts)
    ktot = sum(widths)
    tm = math.gcd(m, ROW_TILE // 2)
    assert w_out.shape == (ktot, D_MODEL)
    in_specs = [pl.BlockSpec((tm, wd), lambda i: (i, 0)) for wd in widths]
    in_specs += [pl.BlockSpec((tm, D_MODEL), lambda i: (i, 0)),
                 pl.BlockSpec((ktot, D_MODEL), lambda i: (0, 0)),
                 pl.BlockSpec((1, D_MODEL), lambda i: (0, 0)),
                 pl.BlockSpec((1, D_MODEL), lambda i: (0, 0))]
    y, yb = pl.pallas_call(
        functools.partial(_proj_ln_kernel, widths=widths),
        grid=(m // tm,),
        in_specs=in_specs,
        out_specs=[pl.BlockSpec((tm, D_MODEL), lambda i: (i, 0)),
                   pl.BlockSpec((tm, D_MODEL), lambda i: (i, 0))],
        out_shape=[jax.ShapeDtypeStruct((m, D_MODEL), F32),
                   jax.ShapeDtypeStruct((m, D_MODEL), BF16)],
        compiler_params=_params("parallel"),
        name=f"proj_ln_k{ktot}_m{m}",
    )(*o_parts, x, w_out, ln_g.reshape(1, D_MODEL), ln_b.reshape(1, D_MODEL))
    return y, yb


def _head_masks(rows):
    lane = lax.broadcasted_iota(jnp.int32, (rows, LANES), 1)
    return lane < HEAD_DIM


SB_KEYS = 256
SB_STEP_LANES = 2 * LANES
SB_DEAD = 104.5


def _pair_queries(q2):
    head0 = _head_masks(q2.shape[0])
    qs = q2 * jnp.asarray(ATTN_SCALE, q2.dtype)
    zero = jnp.zeros_like(qs)
    return jnp.concatenate([jnp.where(head0, qs, zero), jnp.where(head0, zero, qs)], axis=0)


def _pair_outputs(stacked):
    tq = stacked.shape[0] // 2
    return jnp.where(_head_masks(tq), stacked[:tq], stacked[tq:])


def _sb_visit(qh, k_pairs, v_pairs, carries, accs, upper, diag_mask):
    zs = [jnp.dot(q2, kt, preferred_element_type=F32) for q2, kt in zip(qh, k_pairs)]
    sps = [jnp.maximum(z, 0.0) + jnp.log(1.0 + jnp.exp(-jnp.abs(z))) for z in zs]
    if diag_mask is not None:
        sps = [jnp.where(diag_mask, sp, 0.0) for sp in sps]
    masses = [jnp.dot(sp.astype(BF16), upper, preferred_element_type=F32) + c
              for sp, c in zip(sps, carries)]
    ws = [jnp.exp(z - mass) for z, mass in zip(zs, masses)]
    if diag_mask is not None:
        ws = [jnp.where(diag_mask, w, 0.0) for w in ws]
    accs = tuple(a + lax.dot_general(w.astype(BF16), vt, NT_DIMS, preferred_element_type=F32)
                 for a, w, vt in zip(accs, ws, v_pairs))
    return tuple(mass[:, 0:1] for mass in masses), accs


def _sb_upper(n):
    r = lax.broadcasted_iota(jnp.int32, (n, n), 0)
    c = lax.broadcasted_iota(jnp.int32, (n, n), 1)
    return (r >= c).astype(BF16)


def _sb_alive(carries):
    return jnp.min(functools.reduce(jnp.minimum, carries)) < SB_DEAD


def _sb_queries(q_ref):
    return [_pair_queries(q_ref[0, :, p * LANES:(p + 1) * LANES]) for p in range(q_ref.shape[2] // LANES)]


def _sb_strict(tq, keys):
    r = lax.broadcasted_iota(jnp.int32, (2 * tq, keys), 0)
    c = lax.broadcasted_iota(jnp.int32, (2 * tq, keys), 1)
    return c < jnp.where(r >= tq, r - tq, r)


def _sb_finish(accs, g_ref, o_ref):
    for p, acc in enumerate(accs):
        lanes = slice(p * LANES, (p + 1) * LANES)
        o_ref[0, :, lanes] = (_pair_outputs(acc) * _silu(g_ref[0, :, lanes])).astype(o_ref.dtype)


def _pair_slabs(ref, lead, n_pair, cols):
    return [ref[lead, p * LANES:(p + 1) * LANES, cols].astype(BF16) for p in range(n_pair)]


def _sb_prompt_kernel(q_ref, k_ref, v_ref, g_ref, o_ref):
    qi = pl.program_id(2)
    tq = q_ref.shape[1]
    assert tq == SB_KEYS
    qh = _sb_queries(q_ref)
    n_pair = len(qh)
    upper = _sb_upper(SB_KEYS)
    strict = _sb_strict(tq, SB_KEYS)

    def visit(kb, carries, accs, mask):
        cols = pl.ds(pl.multiple_of(kb * SB_KEYS, SB_KEYS), SB_KEYS)
        return _sb_visit(qh, _pair_slabs(k_ref, 0, n_pair, cols), _pair_slabs(v_ref, 0, n_pair, cols),
                         carries, accs, upper, mask)

    zc = (jnp.zeros((2 * tq, 1), F32),) * n_pair
    za = (jnp.zeros((2 * tq, LANES), F32),) * n_pair
    carries, accs = visit(qi, zc, za, strict)

    def older(state):
        kb, _, carries, accs = state
        carries, accs = visit(kb, carries, accs, None)
        return kb - 1, _sb_alive(carries), carries, accs

    state = lax.while_loop(lambda s: jnp.logical_and(s[0] >= 0, s[1]), older,
                           (qi - 1, _sb_alive(carries), carries, accs))
    _sb_finish(state[3], g_ref, o_ref)


def _sb_prompt(q, kt, vt, g):
    b, l, wdt = q.shape
    tq = SB_KEYS
    assert l % tq == 0 and wdt % SB_STEP_LANES == 0
    blk = pl.BlockSpec((1, tq, SB_STEP_LANES), lambda bi, hp, qi: (bi, qi, hp))
    full = pl.BlockSpec((1, SB_STEP_LANES, l), lambda bi, hp, qi: (bi, hp, 0))
    return pl.pallas_call(
        _sb_prompt_kernel,
        grid=(b, wdt // SB_STEP_LANES, l // tq),
        in_specs=[blk, full, full, blk],
        out_specs=blk,
        out_shape=jax.ShapeDtypeStruct((b, l, wdt), BF16),
        compiler_params=_params("parallel", "parallel", "arbitrary"),
        name="sb_prompt",
    )(q, kt, vt, g)


def _sb_sample_kernel(q_ref, kn_ref, vn_ref, g_ref, kc_hbm, vc_hbm, o_ref, kbuf, vbuf, sem, *, n_cached):
    seq = pl.program_id(0)
    tq = q_ref.shape[1]
    qh = _sb_queries(q_ref)
    n_pair = len(qh)
    upper = _sb_upper(SB_KEYS)
    strict = _sb_strict(tq, LANES)
    zc = (jnp.zeros((2 * tq, 1), F32),) * n_pair
    za = (jnp.zeros((2 * tq, LANES), F32),) * n_pair

    def block_copies(kb, slot):
        cols = pl.ds(pl.multiple_of(kb * SB_KEYS, SB_KEYS), SB_KEYS)
        return (pltpu.make_async_copy(kc_hbm.at[seq, :, cols], kbuf.at[slot], sem.at[0, slot]),
                pltpu.make_async_copy(vc_hbm.at[seq, :, cols], vbuf.at[slot], sem.at[1, slot]))

    def start_block(kb):
        for cp in block_copies(kb, kb % 2):
            cp.start()

    def wait_block(kb):
        for cp in block_copies(kb, kb % 2):
            cp.wait()

    start_block(n_cached - 1)
    everything = slice(None)
    carries, accs = _sb_visit(qh, _pair_slabs(kn_ref, 0, n_pair, everything),
                              _pair_slabs(vn_ref, 0, n_pair, everything), zc, za, _sb_upper(LANES), strict)

    def older(state):
        kb, _, carries, accs = state
        wait_block(kb)
        pl.when(kb > 0)(lambda: start_block(kb - 1))
        slot = kb % 2
        carries, accs = _sb_visit(qh, _pair_slabs(kbuf, slot, n_pair, everything),
                                  _pair_slabs(vbuf, slot, n_pair, everything), carries, accs, upper, None)
        return kb - 1, _sb_alive(carries), carries, accs

    state = lax.while_loop(lambda s: jnp.logical_and(s[0] >= 0, s[1]), older,
                           (jnp.int32(n_cached - 1), _sb_alive(carries), carries, accs))
    pl.when(state[0] >= 0)(lambda: wait_block(state[0]))
    _sb_finish(state[3], g_ref, o_ref)


def _sb_sample(q, knt, vnt, g, kct, vct):
    b, t, wdt = q.shape
    p = kct.shape[2]
    assert p % SB_KEYS == 0 and t <= LANES and knt.shape == (b, wdt, LANES)
    tok = pl.BlockSpec((1, t, wdt), lambda bi: (bi, 0, 0))
    new = pl.BlockSpec((1, wdt, LANES), lambda bi: (bi, 0, 0))
    hbm = pl.BlockSpec(memory_space=pl.ANY)
    return pl.pallas_call(
        functools.partial(_sb_sample_kernel, n_cached=p // SB_KEYS),
        grid=(b,),
        in_specs=[tok, new, new, tok, hbm, hbm],
        out_specs=tok,
        out_shape=jax.ShapeDtypeStruct((b, t, wdt), BF16),
        scratch_shapes=[pltpu.VMEM((2, wdt, SB_KEYS), F32),
                        pltpu.VMEM((2, wdt, SB_KEYS), F32),
                        pltpu.SemaphoreType.DMA((2, 2))],
        compiler_params=_params("arbitrary"),
        name="sb_sample",
    )(q, knt, vnt, g, kct, vct)


CONV_PAD = 8
CONV_SLAB = 512


def _ssd_kernel(xbc_ref, z_ref, dt_ref, cs_ref, h0_ref, cw_ref, cb_ref, dtb_ref, alog_ref, drow_ref,
                nw_ref, o_ref, h_ref, xp_ref, xa_ref, y_ref):
    ci = pl.program_id(1)
    q = xbc_ref.shape[1]
    taps = SSD_CONV - 1

    @pl.when(ci == 0)
    def _():
        xp_ref[CONV_PAD - taps:CONV_PAD, :] = cs_ref[0]
        h_ref[0] = h0_ref[0]

    @pl.when(ci > 0)
    def _():
        xp_ref[CONV_PAD - taps:CONV_PAD, :] = xp_ref[CONV_PAD + q - taps:CONV_PAD + q, :]

    xp_ref[CONV_PAD:CONV_PAD + q, :] = xbc_ref[0]
    for cs in range(0, SSD_CONV_DIM, CONV_SLAB):
        acc = cb_ref[:, cs:cs + CONV_SLAB] + cw_ref[0:1, cs:cs + CONV_SLAB] * xp_ref[
            CONV_PAD - taps:CONV_PAD - taps + q, cs:cs + CONV_SLAB]
        for t in range(1, SSD_CONV):
            acc = acc + cw_ref[t:t + 1, cs:cs + CONV_SLAB] * xp_ref[
                CONV_PAD - taps + t:CONV_PAD - taps + t + q, cs:cs + CONV_SLAB]
        xa_ref[:, cs:cs + CONV_SLAB] = _silu(acc)

    dtv = dt_ref[0] + dtb_ref[...]
    dtv = jnp.maximum(dtv, 0.0) + jnp.log1p(jnp.exp(-jnp.abs(dtv)))
    da = dtv * (-jnp.exp(alog_ref[...]))
    r = lax.broadcasted_iota(jnp.int32, (q, q), 0)
    c = lax.broadcasted_iota(jnp.int32, (q, q), 1)
    causal = c <= r
    tril = causal.astype(BF16)
    cum = sum(jnp.dot(tril, p, preferred_element_type=F32) for p in _split3(da))
    er = lax.broadcasted_iota(jnp.int32, (LANES, LANES), 0)
    ec = lax.broadcasted_iota(jnp.int32, (LANES, LANES), 1)
    eye = (er == ec).astype(BF16)

    def transpose(x):
        return sum(lax.dot_general(eye, p, NT_DIMS, preferred_element_type=F32) for p in _split3(x))

    cum_t = transpose(cum)
    dt_t = transpose(dtv)
    cum_last = cum[q - 1:q, :]
    to_end = jnp.exp(cum_last - cum) * dtv
    exp_cum = jnp.exp(cum)
    chunk_decay = jnp.broadcast_to(jnp.exp(cum_last), (LANES, LANES))
    head0 = _head_masks(q)
    top = lax.broadcasted_iota(jnp.int32, (LANES, LANES), 0) < HEAD_DIM

    for g in range(SSD_GROUPS):
        b0 = SSD_WIDTH + g * SSD_STATE
        c0 = SSD_WIDTH + SSD_GROUPS * SSD_STATE + g * SSD_STATE
        bm = xa_ref[:, b0:b0 + SSD_STATE].astype(BF16)
        cm = xa_ref[:, c0:c0 + SSD_STATE].astype(BF16)
        cb = lax.dot_general(cm, bm, NT_DIMS, preferred_element_type=F32)
        for j in range(SSD_HEADS // SSD_GROUPS // 2):
            pr = g * (SSD_HEADS // SSD_GROUPS // 2) + j
            h0, h1 = 2 * pr, 2 * pr + 1
            lo = pr * LANES
            x2 = xa_ref[:, lo:lo + LANES]
            x2b = x2.astype(BF16)
            yd = []
            for h in (h0, h1):
                seg = cum[:, h:h + 1] - cum_t[h:h + 1, :]
                wgt = cb * jnp.exp(jnp.where(causal, seg, -jnp.inf)) * dt_t[h:h + 1, :]
                yd.append(jnp.dot(wgt.astype(BF16), x2b, preferred_element_type=F32))
            y = jnp.where(head0, yd[0], yd[1])
            hs = h_ref[0, lo:lo + LANES, :]
            yoff = lax.dot_general(cm, hs.astype(BF16), NT_DIMS, preferred_element_type=F32)
            y = y + yoff * jnp.where(head0, exp_cum[:, h0:h0 + 1], exp_cum[:, h1:h1 + 1])
            y_ref[:, lo:lo + LANES] = y + drow_ref[:, lo:lo + LANES] * x2
            xw = x2 * jnp.where(head0, to_end[:, h0:h0 + 1], to_end[:, h1:h1 + 1])
            st = lax.dot_general(xw.astype(BF16), bm, TN_DIMS, preferred_element_type=F32)
            dec = jnp.where(top, chunk_decay[:, h0:h0 + 1], chunk_decay[:, h1:h1 + 1])
            h_ref[0, lo:lo + LANES, :] = dec * hs + st

    gw = SSD_WIDTH // SSD_GROUPS
    for g in range(SSD_GROUPS):
        yg = y_ref[:, g * gw:(g + 1) * gw] * _silu(z_ref[0, :, g * gw:(g + 1) * gw])
        ms = jnp.mean(yg * yg, axis=-1, keepdims=True)
        o_ref[0, :, g * gw:(g + 1) * gw] = (yg * lax.rsqrt(ms + RMS_EPS)
                                            * nw_ref[:, g * gw:(g + 1) * gw]).astype(o_ref.dtype)


def _ssd(xbc, z, dt_raw, conv_state, h0, conv_w, conv_b, dt_bias, a_log, d_skip, norm_w, q):
    b, l, _ = xbc.shape
    assert l % q == 0
    pad = LANES - SSD_HEADS
    dtb = jnp.pad(dt_bias, (0, pad)).reshape(1, LANES)
    alog = jnp.pad(a_log, (0, pad)).reshape(1, LANES)
    drow = jnp.repeat(d_skip, HEAD_DIM).reshape(1, SSD_WIDTH)

    def tok(wd):
        return pl.BlockSpec((1, q, wd), lambda bi, ci: (bi, ci, 0))

    def per_seq(shape):
        return pl.BlockSpec((1,) + shape, lambda bi, ci: (bi, 0, 0))

    def const(shape):
        return pl.BlockSpec(shape, lambda bi, ci: (0, 0))

    o, h = pl.pallas_call(
        _ssd_kernel,
        grid=(b, l // q),
        in_specs=[tok(SSD_CONV_DIM), tok(SSD_WIDTH), tok(LANES),
                  per_seq((SSD_CONV - 1, SSD_CONV_DIM)), per_seq((SSD_WIDTH, SSD_STATE)),
                  const((SSD_CONV, SSD_CONV_DIM)), const((1, SSD_CONV_DIM)), const((1, LANES)),
                  const((1, LANES)), const((1, SSD_WIDTH)), const((1, SSD_WIDTH))],
        out_specs=[tok(SSD_WIDTH), per_seq((SSD_WIDTH, SSD_STATE))],
        out_shape=[jax.ShapeDtypeStruct((b, l, SSD_WIDTH), BF16),
                   jax.ShapeDtypeStruct((b, SSD_WIDTH, SSD_STATE), F32)],
        scratch_shapes=[pltpu.VMEM((CONV_PAD + q, SSD_CONV_DIM), F32),
                        pltpu.VMEM((q, SSD_CONV_DIM), F32),
                        pltpu.VMEM((q, SSD_WIDTH), F32)],
        compiler_params=_params("parallel", "arbitrary"),
        name=f"ssd_q{q}",
    )(xbc, z, dt_raw, conv_state, h0, conv_w, conv_b.reshape(1, SSD_CONV_DIM), dtb, alog, drow,
      norm_w.reshape(1, SSD_WIDTH))
    return o, h


BAND_BLOCKS = CA_LEFT // LANES + 1
BAND_PROMPT_PAIRS = 2


def _band_attend(jobs):
    scores = [jnp.dot(_pair_queries(q2), kt, preferred_element_type=F32) + bias2
              for q2, kt, _, bias2, _ in jobs]
    probs = [jnp.exp(s - jnp.max(s, axis=-1, keepdims=True)) for s in scores]
    dens = [jnp.sum(e, axis=-1, keepdims=True) for e in probs]
    nums = [lax.dot_general(e.astype(BF16), job[2], NT_DIMS, preferred_element_type=F32)
            for e, job in zip(probs, jobs)]
    return [_pair_outputs(num / den) * _silu(job[4]) for num, den, job in zip(nums, dens, jobs)]


def _band_prompt_kernel(q_ref, k_ref, v_ref, g_ref, bias_ref, o_ref):
    step = pl.program_id(2)
    n_sub = q_ref.shape[1] // LANES
    n_pair = q_ref.shape[2] // LANES
    band = BAND_BLOCKS * LANES

    def tile(n):
        return slice(n * LANES, (n + 1) * LANES)

    def sub_tile(p, s, i):
        if isinstance(i, int):
            keys = min(i + 1, BAND_BLOCKS) * LANES
            cols = pl.ds((i + 1) * LANES - keys, keys)
        else:
            keys = band
            cols = pl.ds(pl.multiple_of((i + 1) * LANES - band, LANES), band)
        bias2 = jnp.concatenate([bias_ref[2 * p, :, band - keys:], bias_ref[2 * p + 1, :, band - keys:]],
                                axis=0)
        return (q_ref[0, tile(s), tile(p)], k_ref[0, tile(p), cols], v_ref[0, tile(p), cols], bias2,
                g_ref[0, tile(s), tile(p)])

    def step_tiles(first):
        jobs = [(p, s) for p in range(n_pair) for s in range(n_sub)]
        outs = _band_attend([sub_tile(p, s, first + s) for p, s in jobs])
        for (p, s), o in zip(jobs, outs):
            o_ref[0, tile(s), tile(p)] = o.astype(o_ref.dtype)

    pl.when(step == 0)(lambda: step_tiles(0))
    pl.when(step > 0)(lambda: step_tiles(step * n_sub))


def _bias_lookup(rel_bias, t_minus_s):
    rel = jnp.clip(t_minus_s, -CA_REL_CLIP, CA_REL_CLIP) + CA_REL_CLIP
    onehot = (rel[..., None] == jnp.arange(2 * CA_REL_CLIP + 1)).astype(F32)
    return jnp.einsum("rck,hk->hrc", onehot, rel_bias, precision=lax.Precision.HIGHEST)


def _band_bias(rel_bias):
    r = jnp.arange(LANES)[:, None]
    u = jnp.arange(LANES)[None, :]

    def tile(dist):
        return _bias_lookup(rel_bias, dist * LANES + r - u)

    t0 = jnp.where((r < CHUNK) & (u >= CHUNK), NEG_BIG, tile(0))
    t1 = tile(1)
    t2 = jnp.broadcast_to(rel_bias[:, -1][:, None, None], t1.shape)
    t4 = jnp.where((r >= CHUNK) & (u < CHUNK), NEG_BIG, t2)
    return jnp.concatenate([t4, t2, t2, t1, t0], axis=-1).astype(F32)


def _band_prompt(q, kt, vt, g, band_bias):
    b, l, wdt = q.shape
    n_sub = math.gcd(l // LANES, BAND_BLOCKS - 1)
    assert n_sub == BAND_BLOCKS - 1 or l // LANES == n_sub
    tq = n_sub * LANES
    step_lanes = BAND_PROMPT_PAIRS * LANES
    assert wdt % step_lanes == 0
    blk = pl.BlockSpec((1, tq, step_lanes), lambda bi, hp, i: (bi, i, hp))
    full = pl.BlockSpec((1, step_lanes, l), lambda bi, hp, i: (bi, hp, 0))
    bias = pl.BlockSpec((2 * BAND_PROMPT_PAIRS, LANES, BAND_BLOCKS * LANES), lambda bi, hp, i: (hp, 0, 0))
    return pl.pallas_call(
        _band_prompt_kernel,
        grid=(b, wdt // step_lanes, l // tq),
        in_specs=[blk, full, full, blk, bias],
        out_specs=blk,
        out_shape=jax.ShapeDtypeStruct((b, l, wdt), BF16),
        compiler_params=_params("parallel", "parallel", "arbitrary"),
        name="band_prompt",
    )(q, kt, vt, g, band_bias)


BAND_SAMPLE_PAIRS = 4


def _band_sample_kernel(q_ref, kn_ref, vn_ref, knf_ref, vnf_ref, kc_ref, vc_ref, g_ref, bias_ref, o_ref,
                        ok_ref, ov_ref):
    t = q_ref.shape[1]
    p_len = kc_ref.shape[2]
    lane = lax.broadcasted_iota(jnp.int32, (kc_ref.shape[1], LANES), 1)
    for cache_ref, new_ref, out_ref in ((kc_ref, knf_ref, ok_ref), (vc_ref, vnf_ref, ov_ref)):
        shifted = pltpu.roll(cache_ref[0], p_len - t, axis=1)
        moved = pltpu.roll(new_ref[0], LANES - t, axis=1)
        out_ref[0, :, :p_len - LANES] = shifted[:, :p_len - LANES]
        out_ref[0, :, p_len - LANES:] = jnp.where(lane < LANES - t, shifted[:, p_len - LANES:], moved)

    def lanes(p):
        return slice(p * LANES, (p + 1) * LANES)

    def pair(p):
        kt = jnp.concatenate([kc_ref[0, lanes(p), :].astype(BF16), kn_ref[0, lanes(p), :]], axis=1)
        vt = jnp.concatenate([vc_ref[0, lanes(p), :].astype(BF16), vn_ref[0, lanes(p), :]], axis=1)
        bias2 = jnp.concatenate([bias_ref[2 * p], bias_ref[2 * p + 1]], axis=0)
        return q_ref[0, :, lanes(p)], kt, vt, bias2, g_ref[0, :, lanes(p)]

    for p, o in enumerate(_band_attend([pair(p) for p in range(q_ref.shape[2] // LANES)])):
        o_ref[0, :, lanes(p)] = o.astype(o_ref.dtype)


def _band_sample_bias(rel_bias, tq):
    qi = jnp.arange(tq)[:, None]
    kj = jnp.arange(CA_LEFT + LANES)[None, :]
    return jnp.where(kj < CA_LEFT + tq, _bias_lookup(rel_bias, qi - kj + CA_LEFT), NEG_BIG).astype(F32)


def _band_sample(q, knt, vnt, knf, vnf, kct, vct, g, bias):
    b, t, wdt = q.shape
    p = kct.shape[2]
    step_lanes = BAND_SAMPLE_PAIRS * LANES
    assert p == CA_LEFT and t == CHUNK and wdt % step_lanes == 0
    tok = pl.BlockSpec((1, t, step_lanes), lambda bi, hp: (bi, 0, hp))
    new = pl.BlockSpec((1, step_lanes, LANES), lambda bi, hp: (bi, hp, 0))
    old = pl.BlockSpec((1, step_lanes, p), lambda bi, hp: (bi, hp, 0))
    bspec = pl.BlockSpec((2 * BAND_SAMPLE_PAIRS, t, CA_LEFT + LANES), lambda bi, hp: (hp, 0, 0))
    return pl.pallas_call(
        _band_sample_kernel,
        grid=(b, wdt // step_lanes),
        in_specs=[tok, new, new, new, new, old, old, tok, bspec],
        out_specs=[tok, old, old],
        out_shape=[jax.ShapeDtypeStruct((b, t, wdt), BF16), jax.ShapeDtypeStruct(kct.shape, F32),
                   jax.ShapeDtypeStruct(vct.shape, F32)],
        compiler_params=_params("parallel", "parallel"),
        name="band_sample",
    )(q, knt, vnt, knf, vnf, kct, vct, g, bias)


def _feature_major(cache):
    b, p, h, d = cache.shape
    return jnp.transpose(cache, (0, 2, 3, 1)).reshape(b, h * d, p)


def _token_major(kt, heads):
    b, _, t = kt.shape
    return jnp.transpose(kt.reshape(b, heads, HEAD_DIM, t), (0, 3, 1, 2))


def _pad_keys(kt):
    return jnp.pad(kt, ((0, 0), (0, 0), (0, LANES - kt.shape[2])))


def _even_layer(x, xb, past_k, past_v, conv_state, h0, wts, ssd_chunk):
    w_in_t, w_dt_t, conv_w, conv_b, dt_bias, a_log, d_skip, norm_w, w_out, ln_g, ln_b = wts
    b, l, _ = x.shape
    m = b * l
    x2 = x.reshape(m, D_MODEL)
    xb2 = xb.reshape(m, D_MODEL)
    (qb,) = _matmul(xb2, w_in_t, L0_Q, SB_WIDTH, (BF16,), w_is_t=True)
    kt_b, kt = _matmul_t(w_in_t, xb, L0_K, SB_WIDTH, l)
    vt_b, vt = _matmul_t(w_in_t, xb, L0_V, SB_WIDTH, l)
    ga, dt_raw = _matmul(xb2, w_in_t, L0_G, SB_WIDTH, (F32,), w_side=w_dt_t, w_is_t=True)
    (z,) = _matmul(xb2, w_in_t, L0_Z, SSD_WIDTH, (F32,), w_is_t=True)
    (xbc,) = _matmul(xb2, w_in_t, L0_XBC, SSD_CONV_DIM, (F32,), w_is_t=True)

    def seq(a):
        return a.reshape(b, l, a.shape[-1])

    if past_k is None:
        o_a = _sb_prompt(seq(qb), kt_b, vt_b, seq(ga))
    else:
        o_a = _sb_sample(seq(qb), _pad_keys(kt_b), _pad_keys(vt_b), seq(ga), _feature_major(past_k),
                         _feature_major(past_v))
    xbc3 = seq(xbc)
    o_b, h_new = _ssd(xbc3, seq(z), seq(dt_raw), conv_state, h0, conv_w, conv_b, dt_bias, a_log,
                      d_skip, norm_w, ssd_chunk)
    y, yb = _proj_ln([o_a.reshape(m, SB_WIDTH), o_b.reshape(m, SSD_WIDTH)], x2, w_out, ln_g, ln_b)
    tail = jnp.concatenate([conv_state, xbc3], axis=1)[:, -(SSD_CONV - 1):]
    heads = SB_WIDTH // HEAD_DIM
    return (seq(y), seq(yb), _token_major(kt, heads), _token_major(vt, heads), tail,
            h_new.reshape(b, SSD_HEADS, HEAD_DIM, SSD_STATE))


def _odd_layer(x, xb, past_k, past_v, wts):
    w_in, w_kv_t, bias, w_out, ln_g, ln_b = wts
    b, l, _ = x.shape
    m = b * l
    x2 = x.reshape(m, D_MODEL)
    xb2 = xb.reshape(m, D_MODEL)
    keep = min(CA_LEFT, l)
    (qb,) = _matmul(xb2, w_in, 0, CA_WIDTH, (BF16,))
    kt_b, kt = _matmul_t(w_kv_t, xb, 0, CA_WIDTH, keep)
    vt_b, vt = _matmul_t(w_kv_t, xb, CA_WIDTH, CA_WIDTH, keep)
    (g,) = _matmul(xb2, w_in, 3 * CA_WIDTH, CA_WIDTH, (F32,))

    def seq(a):
        return a.reshape(b, l, a.shape[-1])

    if past_k is None:
        o = _band_prompt(seq(qb), kt_b, vt_b, seq(g), bias)
    else:
        o, kt, vt = _band_sample(seq(qb), _pad_keys(kt_b), _pad_keys(vt_b), _pad_keys(kt), _pad_keys(vt),
                                 _feature_major(past_k), _feature_major(past_v), seq(g), bias)
    y, _ = _proj_ln([o.reshape(m, CA_WIDTH)], x2, w_out, ln_g, ln_b)
    heads = CA_WIDTH // HEAD_DIM
    return seq(y), _token_major(kt, heads), _token_major(vt, heads)


def kernel(x_prompt, x_sample, cache_sb_k, cache_sb_v, state_ssm, state_conv, cache_band_k,
           cache_band_v, even_w_in, even_conv_w, even_conv_b, even_dt_bias, even_a_log, even_d_skip,
           even_norm_w, even_w_out, even_ln_g, even_ln_b, odd_w_in, odd_rel_bias, odd_w_out,
           odd_ln_g, odd_ln_b):
    bp, lp, _ = x_prompt.shape
    bs, ls, _ = x_sample.shape
    w_in_t0 = jnp.transpose(even_w_in[0])
    w_dt_t0 = jnp.pad(w_in_t0[L0_DT:].astype(BF16), ((0, LANES - SSD_HEADS), (0, 0)))
    wts0 = (w_in_t0, w_dt_t0, even_conv_w[0], even_conv_b[0], even_dt_bias[0], even_a_log[0],
            even_d_skip[0], even_norm_w[0], even_w_out[0].astype(BF16), even_ln_g[0], even_ln_b[0])
    zero_conv = jnp.zeros((bp, SSD_CONV - 1, SSD_CONV_DIM), F32)
    zero_h = jnp.zeros((bp, SSD_WIDTH, SSD_STATE), F32)
    yp, ypb, p_sb_k, p_sb_v, p_conv, p_ssm = _even_layer(
        x_prompt, x_prompt.astype(BF16), None, None, zero_conv, zero_h, wts0, 2 * CHUNK)
    ys, ysb, s_sb_k, s_sb_v, s_conv, s_ssm = _even_layer(
        x_sample, x_sample.astype(BF16), cache_sb_k[0], cache_sb_v[0], state_conv[0],
        state_ssm[0].reshape(bs, SSD_WIDTH, SSD_STATE), wts0, CHUNK)
    w_in1 = odd_w_in[0]
    w_kv_t1 = jnp.transpose(odd_w_in[0][:, CA_WIDTH:3 * CA_WIDTH]).astype(BF16)
    w_out1 = odd_w_out[0].astype(BF16)
    wts1p = (w_in1, w_kv_t1, _band_bias(odd_rel_bias[0]), w_out1, odd_ln_g[0], odd_ln_b[0])
    wts1s = (w_in1, w_kv_t1, _band_sample_bias(odd_rel_bias[0], ls), w_out1, odd_ln_g[0], odd_ln_b[0])
    yp, p_band_k, p_band_v = _odd_layer(yp, ypb, None, None, wts1p)
    ys, s_band_k, s_band_v = _odd_layer(ys, ysb, cache_band_k[0], cache_band_v[0], wts1s)

    def lead(a):
        return a[None]

    return (yp, ys, lead(p_sb_k), lead(p_sb_v), lead(p_ssm), lead(p_conv), lead(p_band_k),
            lead(p_band_v), lead(s_sb_k), lead(s_sb_v), lead(s_ssm), lead(s_conv), lead(s_band_k),
            lead(s_band_v))
```

```python
import functools
import math

import jax
import jax.numpy as jnp
from jax import lax
from jax.experimental import pallas as pl
from jax.experimental.pallas import tpu as pltpu

F32 = jnp.float32
BF16 = jnp.bfloat16

D_MODEL = 2048
CHUNK = 64
HEAD_DIM = 64
LANES = 128
SB_WIDTH = 1024
SSD_WIDTH = 2048
SSD_HEADS = 32
SSD_GROUPS = 4
SSD_STATE = 128
SSD_CONV = 4
SSD_CONV_DIM = SSD_WIDTH + 2 * SSD_GROUPS * SSD_STATE
CA_WIDTH = 2048
CA_LEFT = 8 * CHUNK
CA_REL_CLIP = 128
DEPTH = 2
DEEPNORM_ALPHA = (2 * DEPTH) ** 0.25
LN_EPS = 1e-5
RMS_EPS = 1e-5
ATTN_SCALE = HEAD_DIM ** -0.5
NEG_BIG = -1e30
L0_Q, L0_K, L0_V, L0_G, L0_Z, L0_XBC, L0_DT = 0, 1024, 2048, 3072, 4096, 6144, 9216
VMEM_LIMIT = 56 * 1024 * 1024
ROW_TILE = 512
COL_TILE = 1024

NT_DIMS = (((1,), (1,)), ((), ()))
TN_DIMS = (((0,), (0,)), ((), ()))


def _params(*sem):
    return pltpu.CompilerParams(dimension_semantics=sem, vmem_limit_bytes=VMEM_LIMIT)


def _silu(x):
    return x * (1.0 / (1.0 + jnp.exp(-x)))


def _split3(x):
    p0 = x.astype(BF16)
    r = x - p0.astype(F32)
    p1 = r.astype(BF16)
    p2 = (r - p1.astype(F32)).astype(BF16)
    return p0, p1, p2


def _mm_kernel(x_ref, w_ref, *refs, has_side, w_is_t, copy_x):
    *o_refs, wb_ref = refs

    @pl.when(pl.program_id(1) == 0)
    def _():
        wb_ref[...] = w_ref[...].astype(BF16)

    def product(w):
        if w_is_t:
            return lax.dot_general(x, w, NT_DIMS, preferred_element_type=F32)
        return jnp.dot(x, w, preferred_element_type=F32)

    x = x_ref[...].astype(BF16)
    if copy_x:
        o_refs.pop()[...] = x
    if has_side:
        ws_ref = o_refs.pop(0)
        os_ref = o_refs.pop()
        os_ref[...] = product(ws_ref[...])
    acc = product(wb_ref[...])
    for o_ref in o_refs:
        o_ref[...] = acc.astype(o_ref.dtype)


def _matmul(x, w, col0, ncols, out_dtypes, w_side=None, w_is_t=False, copy_x=False):
    m, k = x.shape
    tm, tn = math.gcd(m, ROW_TILE), COL_TILE
    assert ncols % tn == 0 and col0 % tn == 0
    cb0 = col0 // tn
    if w_is_t:
        w_spec = pl.BlockSpec((tn, k), lambda j, i: (cb0 + j, 0))
    else:
        w_spec = pl.BlockSpec((k, tn), lambda j, i: (0, cb0 + j))
    in_specs = [pl.BlockSpec((tm, k), lambda j, i: (i, 0)), w_spec]
    out_specs = [pl.BlockSpec((tm, tn), lambda j, i: (i, j)) for _ in out_dtypes]
    out_shape = [jax.ShapeDtypeStruct((m, ncols), dt) for dt in out_dtypes]
    args = [x, w]
    if w_side is not None:
        assert ncols == tn
        ns = w_side.shape[0 if w_is_t else 1]
        in_specs.append(pl.BlockSpec(w_side.shape, lambda j, i: (0, 0)))
        out_specs.append(pl.BlockSpec((tm, ns), lambda j, i: (i, 0)))
        out_shape.append(jax.ShapeDtypeStruct((m, ns), F32))
        args.append(w_side)
    if copy_x:
        assert ncols == tn
        out_specs.append(pl.BlockSpec((tm, k), lambda j, i: (i, 0)))
        out_shape.append(jax.ShapeDtypeStruct((m, k), BF16))
    return pl.pallas_call(
        functools.partial(_mm_kernel, has_side=w_side is not None, w_is_t=w_is_t, copy_x=copy_x),
        grid=(ncols // tn, m // tm),
        in_specs=in_specs,
        out_specs=out_specs,
        out_shape=out_shape,
        scratch_shapes=[pltpu.VMEM(w_spec.block_shape, BF16)],
        compiler_params=_params("arbitrary", "arbitrary"),
        name=f"mm_c{col0}_n{ncols}_m{m}",
    )(*args)


def _mm_t_kernel(w_ref, x_ref, ob_ref, of_ref, wb_ref, *, first_kept):
    @pl.when(jnp.logical_and(pl.program_id(0) == 0, pl.program_id(1) == 0))
    def _():
        wb_ref[...] = w_ref[...].astype(BF16)

    acc = lax.dot_general(wb_ref[...], x_ref[0], NT_DIMS, preferred_element_type=F32)
    ob_ref[0] = acc.astype(ob_ref.dtype)

    def keep_f32():
        of_ref[0] = acc

    if first_kept == 0:
        keep_f32()
    else:
        pl.when(pl.program_id(1) >= first_kept)(keep_f32)


def _matmul_t(w_t, x, row0, nrows, keep):
    s, l, k = x.shape
    tm = math.gcd(l, ROW_TILE)
    assert row0 % nrows == 0 and keep % tm == 0 and keep <= l
    first_kept = (l - keep) // tm
    return pl.pallas_call(
        functools.partial(_mm_t_kernel, first_kept=first_kept),
        grid=(s, l // tm),
        in_specs=[pl.BlockSpec((nrows, k), lambda si, i: (row0 // nrows, 0)),
                  pl.BlockSpec((1, tm, k), lambda si, i: (si, i, 0))],
        out_specs=[pl.BlockSpec((1, nrows, tm), lambda si, i: (si, 0, i)),
                   pl.BlockSpec((1, nrows, tm), lambda si, i: (si, 0, jnp.maximum(i - first_kept, 0)))],
        out_shape=[jax.ShapeDtypeStruct((s, nrows, l), BF16),
                   jax.ShapeDtypeStruct((s, nrows, keep), F32)],
        scratch_shapes=[pltpu.VMEM((nrows, k), BF16)],
        compiler_params=_params("arbitrary", "arbitrary"),
        name=f"mm_t_r{row0}_n{nrows}_l{l}",
    )(w_t, x)


def _proj_ln_kernel(*refs, widths):
    n = len(widths)
    o_refs = refs[:n]
    x_ref, w_ref, g_ref, b_ref, y_ref, yb_ref = refs[n:]
    mix = None
    row = 0
    for o_ref, wd in zip(o_refs, widths):
        part = jnp.dot(o_ref[...], w_ref[row:row + wd, :], preferred_element_type=F32)
        mix = part if mix is None else mix + part
        row += wd
    h = DEEPNORM_ALPHA * x_ref[...] + mix
    mu = jnp.mean(h, axis=-1, keepdims=True)
    hc = h - mu
    var = jnp.mean(hc * hc, axis=-1, keepdims=True)
    y = hc * lax.rsqrt(var + LN_EPS) * g_ref[...] + b_ref[...]
    y_ref[...] = y
    yb_ref[...] = y.astype(BF16)


def _proj_ln(o_parts, x, w_out, ln_g, ln_b):
    m = x.shape[0]
    widths = tuple(o.shape[1] for o in o_parts)
    ktot = sum(widths)
    tm = math.gcd(m, ROW_TILE)
    assert w_out.shape == (ktot, D_MODEL)
    in_specs = [pl.BlockSpec((tm, wd), lambda i: (i, 0)) for wd in widths]
    in_specs += [pl.BlockSpec((tm, D_MODEL), lambda i: (i, 0)),
                 pl.BlockSpec((ktot, D_MODEL), lambda i: (0, 0), pipeline_mode=pl.Buffered(1)),
                 pl.BlockSpec((1, D_MODEL), lambda i: (0, 0)),
                 pl.BlockSpec((1, D_MODEL), lambda i: (0, 0))]
    y, yb = pl.pallas_call(
        functools.partial(_proj_ln_kernel, widths=widths),
        grid=(m // tm,),
        in_specs=in_specs,
        out_specs=[pl.BlockSpec((tm, D_MODEL), lambda i: (i, 0)),
                   pl.BlockSpec((tm, D_MODEL), lambda i: (i, 0))],
        out_shape=[jax.ShapeDtypeStruct((m, D_MODEL), F32),
                   jax.ShapeDtypeStruct((m, D_MODEL), BF16)],
        compiler_params=_params("parallel"),
        name=f"proj_ln_k{ktot}_m{m}",
    )(*o_parts, x, w_out, ln_g.reshape(1, D_MODEL), ln_b.reshape(1, D_MODEL))
    return y, yb


def _head_masks(rows):
    lane = lax.broadcasted_iota(jnp.int32, (rows, LANES), 1)
    return lane < HEAD_DIM


SB_KEYS = 256
SB_STEP_LANES = 2 * LANES
SB_DEAD = 104.5


def _pair_queries(q2):
    head0 = _head_masks(q2.shape[0])
    qs = q2 * jnp.asarray(ATTN_SCALE, q2.dtype)
    zero = jnp.zeros_like(qs)
    return jnp.concatenate([jnp.where(head0, qs, zero), jnp.where(head0, zero, qs)], axis=0)


def _pair_outputs(stacked):
    tq = stacked.shape[0] // 2
    return jnp.where(_head_masks(tq), stacked[:tq], stacked[tq:])


def _sb_visit(qh, k_pairs, v_pairs, carries, accs, upper, diag_mask):
    zs = [jnp.dot(q2, kt, preferred_element_type=F32) for q2, kt in zip(qh, k_pairs)]
    sps = [jnp.maximum(z, 0.0) + jnp.log(1.0 + jnp.exp(-jnp.abs(z))) for z in zs]
    if diag_mask is not None:
        sps = [jnp.where(diag_mask, sp, 0.0) for sp in sps]
    masses = [jnp.dot(sp.astype(BF16), upper, preferred_element_type=F32) + c
              for sp, c in zip(sps, carries)]
    ws = [jnp.exp(z - mass) for z, mass in zip(zs, masses)]
    if diag_mask is not None:
        ws = [jnp.where(diag_mask, w, 0.0) for w in ws]
    accs = tuple(a + lax.dot_general(w.astype(BF16), vt, NT_DIMS, preferred_element_type=F32)
                 for a, w, vt in zip(accs, ws, v_pairs))
    return tuple(mass[:, 0:1] for mass in masses), accs


def _sb_upper(n):
    r = lax.broadcasted_iota(jnp.int32, (n, n), 0)
    c = lax.broadcasted_iota(jnp.int32, (n, n), 1)
    return (r >= c).astype(BF16)


def _sb_alive(carries):
    return jnp.min(functools.reduce(jnp.minimum, carries)) < SB_DEAD


def _sb_queries(q_ref):
    return [_pair_queries(q_ref[0, :, p * LANES:(p + 1) * LANES]) for p in range(q_ref.shape[2] // LANES)]


def _sb_strict(tq, keys):
    r = lax.broadcasted_iota(jnp.int32, (2 * tq, keys), 0)
    c = lax.broadcasted_iota(jnp.int32, (2 * tq, keys), 1)
    return c < jnp.where(r >= tq, r - tq, r)


def _sb_finish(accs, g_ref, o_ref):
    for p, acc in enumerate(accs):
        lanes = slice(p * LANES, (p + 1) * LANES)
        o_ref[0, :, lanes] = (_pair_outputs(acc) * _silu(g_ref[0, :, lanes])).astype(o_ref.dtype)


def _pair_slabs(ref, lead, n_pair, cols):
    return [ref[lead, p * LANES:(p + 1) * LANES, cols].astype(BF16) for p in range(n_pair)]


def _sb_prompt_kernel(q_ref, k_ref, v_ref, g_ref, o_ref):
    qi = pl.program_id(2)
    tq = q_ref.shape[1]
    assert tq == SB_KEYS
    qh = _sb_queries(q_ref)
    n_pair = len(qh)
    upper = _sb_upper(SB_KEYS)
    strict = _sb_strict(tq, SB_KEYS)

    def visit(kb, carries, accs, mask):
        cols = pl.ds(pl.multiple_of(kb * SB_KEYS, SB_KEYS), SB_KEYS)
        return _sb_visit(qh, _pair_slabs(k_ref, 0, n_pair, cols), _pair_slabs(v_ref, 0, n_pair, cols),
                         carries, accs, upper, mask)

    zc = (jnp.zeros((2 * tq, 1), F32),) * n_pair
    za = (jnp.zeros((2 * tq, LANES), F32),) * n_pair
    carries, accs = visit(qi, zc, za, strict)

    def older(state):
        kb, _, carries, accs = state
        carries, accs = visit(kb, carries, accs, None)
        return kb - 1, _sb_alive(carries), carries, accs

    state = lax.while_loop(lambda s: jnp.logical_and(s[0] >= 0, s[1]), older,
                           (qi - 1, _sb_alive(carries), carries, accs))
    _sb_finish(state[3], g_ref, o_ref)


def _sb_prompt(q, kt, vt, g):
    b, l, wdt = q.shape
    tq = SB_KEYS
    assert l % tq == 0 and wdt % SB_STEP_LANES == 0
    blk = pl.BlockSpec((1, tq, SB_STEP_LANES), lambda bi, hp, qi: (bi, qi, hp))
    full = pl.BlockSpec((1, SB_STEP_LANES, l), lambda bi, hp, qi: (bi, hp, 0))
    return pl.pallas_call(
        _sb_prompt_kernel,
        grid=(b, wdt // SB_STEP_LANES, l // tq),
        in_specs=[blk, full, full, blk],
        out_specs=blk,
        out_shape=jax.ShapeDtypeStruct((b, l, wdt), BF16),
        compiler_params=_params("parallel", "parallel", "arbitrary"),
        name="sb_prompt",
    )(q, kt, vt, g)


def _sb_sample_kernel(q_ref, kn_ref, vn_ref, g_ref, kc_hbm, vc_hbm, o_ref, kbuf, vbuf, sem, *, n_cached):
    seq = pl.program_id(0)
    tq = q_ref.shape[1]
    qh = _sb_queries(q_ref)
    n_pair = len(qh)
    upper = _sb_upper(SB_KEYS)
    strict = _sb_strict(tq, LANES)
    zc = (jnp.zeros((2 * tq, 1), F32),) * n_pair
    za = (jnp.zeros((2 * tq, LANES), F32),) * n_pair

    def block_copies(kb, slot):
        cols = pl.ds(pl.multiple_of(kb * SB_KEYS, SB_KEYS), SB_KEYS)
        return (pltpu.make_async_copy(kc_hbm.at[seq, :, cols], kbuf.at[slot], sem.at[0, slot]),
                pltpu.make_async_copy(vc_hbm.at[seq, :, cols], vbuf.at[slot], sem.at[1, slot]))

    def start_block(kb):
        for cp in block_copies(kb, kb % 2):
            cp.start()

    def wait_block(kb):
        for cp in block_copies(kb, kb % 2):
            cp.wait()

    start_block(n_cached - 1)
    everything = slice(None)

    def new_pairs(ref):
        return [_own_columns(ref[p * LANES:(p + 1) * LANES, :], seq, tq).astype(BF16) for p in range(n_pair)]

    carries, accs = _sb_visit(qh, new_pairs(kn_ref), new_pairs(vn_ref), zc, za, _sb_upper(LANES), strict)

    def older(state):
        kb, _, carries, accs = state
        wait_block(kb)
        pl.when(kb > 0)(lambda: start_block(kb - 1))
        slot = kb % 2
        carries, accs = _sb_visit(qh, _pair_slabs(kbuf, slot, n_pair, everything),
                                  _pair_slabs(vbuf, slot, n_pair, everything), carries, accs, upper, None)
        return kb - 1, _sb_alive(carries), carries, accs

    state = lax.while_loop(lambda s: jnp.logical_and(s[0] >= 0, s[1]), older,
                           (jnp.int32(n_cached - 1), _sb_alive(carries), carries, accs))
    pl.when(state[0] >= 0)(lambda: wait_block(state[0]))
    _sb_finish(state[3], g_ref, o_ref)


def _own_columns(block, seq, t):
    assert 2 * t == LANES
    return jnp.where(seq % 2 == 1, pltpu.roll(block, t, axis=1), block)


def _sb_sample(q, knt, vnt, g, kct, vct):
    b, t, wdt = q.shape
    p = kct.shape[2]
    assert p % SB_KEYS == 0 and knt.shape == (wdt, b * t)
    tok = pl.BlockSpec((1, t, wdt), lambda bi: (bi, 0, 0))
    new = pl.BlockSpec((wdt, LANES), lambda bi: (0, bi * t // LANES))
    hbm = pl.BlockSpec(memory_space=pl.ANY)
    return pl.pallas_call(
        functools.partial(_sb_sample_kernel, n_cached=p // SB_KEYS),
        grid=(b,),
        in_specs=[tok, new, new, tok, hbm, hbm],
        out_specs=tok,
        out_shape=jax.ShapeDtypeStruct((b, t, wdt), BF16),
        scratch_shapes=[pltpu.VMEM((2, wdt, SB_KEYS), F32),
                        pltpu.VMEM((2, wdt, SB_KEYS), F32),
                        pltpu.SemaphoreType.DMA((2, 2))],
        compiler_params=_params("arbitrary"),
        name="sb_sample",
    )(q, knt, vnt, g, kct, vct)


CONV_PAD = 8
CONV_SLAB = 512


def _ssd_kernel(xbc_ref, z_ref, dt_ref, cs_ref, h0_ref, cw_ref, cb_ref, dtb_ref, alog_ref, drow_ref,
                nw_ref, o_ref, h_ref, xp_ref, xa_ref, y_ref):
    ci = pl.program_id(1)
    q = xbc_ref.shape[1]
    taps = SSD_CONV - 1

    @pl.when(ci == 0)
    def _():
        xp_ref[CONV_PAD - taps:CONV_PAD, :] = cs_ref[0]
        h_ref[0] = h0_ref[0]

    @pl.when(ci > 0)
    def _():
        xp_ref[CONV_PAD - taps:CONV_PAD, :] = xp_ref[CONV_PAD + q - taps:CONV_PAD + q, :]

    xp_ref[CONV_PAD:CONV_PAD + q, :] = xbc_ref[0]
    for cs in range(0, SSD_CONV_DIM, CONV_SLAB):
        acc = cb_ref[:, cs:cs + CONV_SLAB] + cw_ref[0:1, cs:cs + CONV_SLAB] * xp_ref[
            CONV_PAD - taps:CONV_PAD - taps + q, cs:cs + CONV_SLAB]
        for t in range(1, SSD_CONV):
            acc = acc + cw_ref[t:t + 1, cs:cs + CONV_SLAB] * xp_ref[
                CONV_PAD - taps + t:CONV_PAD - taps + t + q, cs:cs + CONV_SLAB]
        xa_ref[:, cs:cs + CONV_SLAB] = _silu(acc)

    dtv = dt_ref[0] + dtb_ref[...]
    dtv = jnp.maximum(dtv, 0.0) + jnp.log1p(jnp.exp(-jnp.abs(dtv)))
    da = dtv * (-jnp.exp(alog_ref[...]))
    r = lax.broadcasted_iota(jnp.int32, (q, q), 0)
    c = lax.broadcasted_iota(jnp.int32, (q, q), 1)
    causal = c <= r
    tril = causal.astype(BF16)
    cum = sum(jnp.dot(tril, p, preferred_element_type=F32) for p in _split3(da))
    er = lax.broadcasted_iota(jnp.int32, (LANES, LANES), 0)
    ec = lax.broadcasted_iota(jnp.int32, (LANES, LANES), 1)
    eye = (er == ec).astype(BF16)

    def transpose(x):
        return sum(lax.dot_general(eye, p, NT_DIMS, preferred_element_type=F32) for p in _split3(x))

    cum_t = transpose(cum)
    dt_t = transpose(dtv)
    cum_last = cum[q - 1:q, :]
    to_end = jnp.exp(cum_last - cum) * dtv
    exp_cum = jnp.exp(cum)
    chunk_decay = jnp.broadcast_to(jnp.exp(cum_last), (LANES, LANES))
    head0 = _head_masks(q)
    top = lax.broadcasted_iota(jnp.int32, (LANES, LANES), 0) < HEAD_DIM

    for g in range(SSD_GROUPS):
        b0 = SSD_WIDTH + g * SSD_STATE
        c0 = SSD_WIDTH + SSD_GROUPS * SSD_STATE + g * SSD_STATE
        bm = xa_ref[:, b0:b0 + SSD_STATE].astype(BF16)
        cm = xa_ref[:, c0:c0 + SSD_STATE].astype(BF16)
        cb = lax.dot_general(cm, bm, NT_DIMS, preferred_element_type=F32)
        for j in range(SSD_HEADS // SSD_GROUPS // 2):
            pr = g * (SSD_HEADS // SSD_GROUPS // 2) + j
            h0, h1 = 2 * pr, 2 * pr + 1
            lo = pr * LANES
            x2 = xa_ref[:, lo:lo + LANES]
            x2b = x2.astype(BF16)
            yd = []
            for h in (h0, h1):
                seg = cum[:, h:h + 1] - cum_t[h:h + 1, :]
                wgt = cb * jnp.exp(jnp.where(causal, seg, -jnp.inf)) * dt_t[h:h + 1, :]
                yd.append(jnp.dot(wgt.astype(BF16), x2b, preferred_element_type=F32))
            y = jnp.where(head0, yd[0], yd[1])
            hs = h_ref[0, lo:lo + LANES, :]
            yoff = lax.dot_general(cm, hs.astype(BF16), NT_DIMS, preferred_element_type=F32)
            y = y + yoff * jnp.where(head0, exp_cum[:, h0:h0 + 1], exp_cum[:, h1:h1 + 1])
            y_ref[:, lo:lo + LANES] = y + drow_ref[:, lo:lo + LANES] * x2
            xw = x2 * jnp.where(head0, to_end[:, h0:h0 + 1], to_end[:, h1:h1 + 1])
            st = lax.dot_general(xw.astype(BF16), bm, TN_DIMS, preferred_element_type=F32)
            dec = jnp.where(top, chunk_decay[:, h0:h0 + 1], chunk_decay[:, h1:h1 + 1])
            h_ref[0, lo:lo + LANES, :] = dec * hs + st

    gw = SSD_WIDTH // SSD_GROUPS
    for g in range(SSD_GROUPS):
        yg = y_ref[:, g * gw:(g + 1) * gw] * _silu(z_ref[0, :, g * gw:(g + 1) * gw])
        ms = jnp.mean(yg * yg, axis=-1, keepdims=True)
        o_ref[0, :, g * gw:(g + 1) * gw] = (yg * lax.rsqrt(ms + RMS_EPS)
                                            * nw_ref[:, g * gw:(g + 1) * gw]).astype(o_ref.dtype)


def _ssd(xbc, z, dt_raw, conv_state, h0, conv_w, conv_b, dt_bias, a_log, d_skip, norm_w, q):
    b, l, _ = xbc.shape
    assert l % q == 0
    pad = LANES - SSD_HEADS
    dtb = jnp.pad(dt_bias, (0, pad)).reshape(1, LANES)
    alog = jnp.pad(a_log, (0, pad)).reshape(1, LANES)
    drow = jnp.repeat(d_skip, HEAD_DIM).reshape(1, SSD_WIDTH)

    def tok(wd):
        return pl.BlockSpec((1, q, wd), lambda bi, ci: (bi, ci, 0))

    def per_seq(shape):
        return pl.BlockSpec((1,) + shape, lambda bi, ci: (bi, 0, 0))

    def const(shape):
        return pl.BlockSpec(shape, lambda bi, ci: (0, 0))

    o, h = pl.pallas_call(
        _ssd_kernel,
        grid=(b, l // q),
        in_specs=[tok(SSD_CONV_DIM), tok(SSD_WIDTH), tok(LANES),
                  per_seq((SSD_CONV - 1, SSD_CONV_DIM)), per_seq((SSD_WIDTH, SSD_STATE)),
                  const((SSD_CONV, SSD_CONV_DIM)), const((1, SSD_CONV_DIM)), const((1, LANES)),
                  const((1, LANES)), const((1, SSD_WIDTH)), const((1, SSD_WIDTH))],
        out_specs=[tok(SSD_WIDTH), per_seq((SSD_WIDTH, SSD_STATE))],
        out_shape=[jax.ShapeDtypeStruct((b, l, SSD_WIDTH), BF16),
                   jax.ShapeDtypeStruct((b, SSD_WIDTH, SSD_STATE), F32)],
        scratch_shapes=[pltpu.VMEM((CONV_PAD + q, SSD_CONV_DIM), F32),
                        pltpu.VMEM((q, SSD_CONV_DIM), F32),
                        pltpu.VMEM((q, SSD_WIDTH), F32)],
        compiler_params=_params("parallel", "arbitrary"),
        name=f"ssd_q{q}",
    )(xbc, z, dt_raw, conv_state, h0, conv_w, conv_b.reshape(1, SSD_CONV_DIM), dtb, alog, drow,
      norm_w.reshape(1, SSD_WIDTH))
    return o, h


BAND_BLOCKS = CA_LEFT // LANES + 1
BAND_PROMPT_PAIRS = 2


def _band_attend(jobs):
    scores = [jnp.dot(_pair_queries(q2), kt, preferred_element_type=F32) + bias2
              for q2, kt, _, bias2, _ in jobs]
    probs = [jnp.exp(s - jnp.max(s, axis=-1, keepdims=True)) for s in scores]
    dens = [jnp.sum(e, axis=-1, keepdims=True) for e in probs]
    nums = [lax.dot_general(e.astype(BF16), job[2], NT_DIMS, preferred_element_type=F32)
            for e, job in zip(probs, jobs)]
    return [_pair_outputs(num / den) * _silu(job[4]) for num, den, job in zip(nums, dens, jobs)]


def _band_prompt_kernel(q_ref, k_ref, v_ref, g_ref, bias_ref, o_ref):
    step = pl.program_id(2)
    n_sub = q_ref.shape[1] // LANES
    n_pair = q_ref.shape[2] // LANES
    band = BAND_BLOCKS * LANES

    def tile(n):
        return slice(n * LANES, (n + 1) * LANES)

    def sub_tile(p, s, i):
        if isinstance(i, int):
            keys = min(i + 1, BAND_BLOCKS) * LANES
            cols = pl.ds((i + 1) * LANES - keys, keys)
        else:
            keys = band
            cols = pl.ds(pl.multiple_of((i + 1) * LANES - band, LANES), band)
        bias2 = jnp.concatenate([bias_ref[2 * p, :, band - keys:], bias_ref[2 * p + 1, :, band - keys:]],
                                axis=0)
        return (q_ref[0, tile(s), tile(p)], k_ref[0, tile(p), cols], v_ref[0, tile(p), cols], bias2,
                g_ref[0, tile(s), tile(p)])

    def step_tiles(first):
        jobs = [(p, s) for p in range(n_pair) for s in range(n_sub)]
        outs = _band_attend([sub_tile(p, s, first + s) for p, s in jobs])
        for (p, s), o in zip(jobs, outs):
            o_ref[0, tile(s), tile(p)] = o.astype(o_ref.dtype)

    pl.when(step == 0)(lambda: step_tiles(0))
    pl.when(step > 0)(lambda: step_tiles(step * n_sub))


def _bias_lookup(rel_bias, t_minus_s):
    rel = jnp.clip(t_minus_s, -CA_REL_CLIP, CA_REL_CLIP) + CA_REL_CLIP
    onehot = (rel[..., None] == jnp.arange(2 * CA_REL_CLIP + 1)).astype(F32)
    return jnp.einsum("rck,hk->hrc", onehot, rel_bias, precision=lax.Precision.HIGHEST)


def _band_bias(rel_bias):
    r = jnp.arange(LANES)[:, None]
    u = jnp.arange(LANES)[None, :]

    def tile(dist):
        return _bias_lookup(rel_bias, dist * LANES + r - u)

    t0 = jnp.where((r < CHUNK) & (u >= CHUNK), NEG_BIG, tile(0))
    t1 = tile(1)
    t2 = jnp.broadcast_to(rel_bias[:, -1][:, None, None], t1.shape)
    t4 = jnp.where((r >= CHUNK) & (u < CHUNK), NEG_BIG, t2)
    return jnp.concatenate([t4, t2, t2, t1, t0], axis=-1).astype(F32)


def _band_prompt(q, kt, vt, g, band_bias):
    b, l, wdt = q.shape
    n_sub = math.gcd(l // LANES, BAND_BLOCKS - 1)
    assert n_sub == BAND_BLOCKS - 1 or l // LANES == n_sub
    tq = n_sub * LANES
    step_lanes = BAND_PROMPT_PAIRS * LANES
    assert wdt % step_lanes == 0
    blk = pl.BlockSpec((1, tq, step_lanes), lambda bi, hp, i: (bi, i, hp))
    full = pl.BlockSpec((1, step_lanes, l), lambda bi, hp, i: (bi, hp, 0))
    bias = pl.BlockSpec((2 * BAND_PROMPT_PAIRS, LANES, BAND_BLOCKS * LANES), lambda bi, hp, i: (hp, 0, 0))
    return pl.pallas_call(
        _band_prompt_kernel,
        grid=(b, wdt // step_lanes, l // tq),
        in_specs=[blk, full, full, blk, bias],
        out_specs=blk,
        out_shape=jax.ShapeDtypeStruct((b, l, wdt), BF16),
        compiler_params=_params("parallel", "parallel", "arbitrary"),
        name="band_prompt",
    )(q, kt, vt, g, band_bias)


BAND_SAMPLE_PAIRS = 4


def _band_sample_kernel(q_ref, kn_ref, vn_ref, kc_ref, vc_ref, g_ref, bias_ref, o_ref, ok_ref, ov_ref):
    seq = pl.program_id(0)
    t = q_ref.shape[1]
    p_len = kc_ref.shape[2]
    lane = lax.broadcasted_iota(jnp.int32, (kc_ref.shape[1], LANES), 1)
    k_new = _own_columns(kn_ref[...], seq, t)
    v_new = _own_columns(vn_ref[...], seq, t)
    for cache_ref, new, out_ref in ((kc_ref, k_new, ok_ref), (vc_ref, v_new, ov_ref)):
        shifted = pltpu.roll(cache_ref[0], p_len - t, axis=1)
        moved = pltpu.roll(new, LANES - t, axis=1)
        out_ref[0, :, :p_len - LANES] = shifted[:, :p_len - LANES]
        out_ref[0, :, p_len - LANES:] = jnp.where(lane < LANES - t, shifted[:, p_len - LANES:], moved)

    def lanes(p):
        return slice(p * LANES, (p + 1) * LANES)

    def pair(p):
        kt = jnp.concatenate([kc_ref[0, lanes(p), :], k_new[lanes(p)]], axis=1).astype(BF16)
        vt = jnp.concatenate([vc_ref[0, lanes(p), :], v_new[lanes(p)]], axis=1).astype(BF16)
        bias2 = jnp.concatenate([bias_ref[2 * p], bias_ref[2 * p + 1]], axis=0)
        return q_ref[0, :, lanes(p)], kt, vt, bias2, g_ref[0, :, lanes(p)]

    for p, o in enumerate(_band_attend([pair(p) for p in range(q_ref.shape[2] // LANES)])):
        o_ref[0, :, lanes(p)] = o.astype(o_ref.dtype)


def _band_sample_bias(rel_bias, tq):
    qi = jnp.arange(tq)[:, None]
    kj = jnp.arange(CA_LEFT + LANES)[None, :]
    return jnp.where(kj < CA_LEFT + tq, _bias_lookup(rel_bias, qi - kj + CA_LEFT), NEG_BIG).astype(F32)


def _band_sample(q, knt, vnt, kct, vct, g, bias):
    b, t, wdt = q.shape
    p = kct.shape[2]
    step_lanes = BAND_SAMPLE_PAIRS * LANES
    assert p == CA_LEFT and t == CHUNK and wdt % step_lanes == 0 and knt.shape == (wdt, b * t)
    tok = pl.BlockSpec((1, t, step_lanes), lambda bi, hp: (bi, 0, hp))
    new = pl.BlockSpec((step_lanes, LANES), lambda bi, hp: (hp, bi * t // LANES))
    old = pl.BlockSpec((1, step_lanes, p), lambda bi, hp: (bi, hp, 0))
    bspec = pl.BlockSpec((2 * BAND_SAMPLE_PAIRS, t, CA_LEFT + LANES), lambda bi, hp: (hp, 0, 0))
    return pl.pallas_call(
        _band_sample_kernel,
        grid=(b, wdt // step_lanes),
        in_specs=[tok, new, new, old, old, tok, bspec],
        out_specs=[tok, old, old],
        out_shape=[jax.ShapeDtypeStruct((b, t, wdt), BF16), jax.ShapeDtypeStruct(kct.shape, F32),
                   jax.ShapeDtypeStruct(vct.shape, F32)],
        compiler_params=_params("parallel", "parallel"),
        name="band_sample",
    )(q, knt, vnt, kct, vct, g, bias)


def _feature_major(cache):
    b, p, h, d = cache.shape
    return jnp.transpose(cache, (0, 2, 3, 1)).reshape(b, h * d, p)


def _token_major(kt, heads):
    b, _, t = kt.shape
    return jnp.transpose(kt.reshape(b, heads, HEAD_DIM, t), (0, 3, 1, 2))


def _new_keys(w_t, xb, row0, nrows, has_cache, keep):
    b, l, d = xb.shape
    if not has_cache:
        return _matmul_t(w_t, xb, row0, nrows, keep)
    _, kt = _matmul_t(w_t, xb.reshape(1, b * l, d), row0, nrows, b * l)
    return None, kt[0]


def _sample_token_major(kt, b, heads):
    t = kt.shape[1] // b
    return jnp.transpose(kt.reshape(heads, HEAD_DIM, b, t), (2, 3, 0, 1))


def _even_layer(x, past_k, past_v, conv_state, h0, wts, ssd_chunk):
    w_in_t, w_dt_t, conv_w, conv_b, dt_bias, a_log, d_skip, norm_w, w_out, ln_g, ln_b = wts
    b, l, _ = x.shape
    m = b * l
    heads = SB_WIDTH // HEAD_DIM
    x2 = x.reshape(m, D_MODEL)
    qb, xb2 = _matmul(x2, w_in_t, L0_Q, SB_WIDTH, (BF16,), w_is_t=True, copy_x=True)
    xb = xb2.reshape(b, l, D_MODEL)
    kt_b, kt = _new_keys(w_in_t, xb, L0_K, SB_WIDTH, past_k is not None, l)
    vt_b, vt = _new_keys(w_in_t, xb, L0_V, SB_WIDTH, past_k is not None, l)
    ga, dt_raw = _matmul(xb2, w_in_t, L0_G, SB_WIDTH, (F32,), w_side=w_dt_t, w_is_t=True)
    (z,) = _matmul(xb2, w_in_t, L0_Z, SSD_WIDTH, (F32,), w_is_t=True)
    (xbc,) = _matmul(xb2, w_in_t, L0_XBC, SSD_CONV_DIM, (F32,), w_is_t=True)

    def seq(a):
        return a.reshape(b, l, a.shape[-1])

    if past_k is None:
        o_a = _sb_prompt(seq(qb), kt_b, vt_b, seq(ga))
        k_new, v_new = _token_major(kt, heads), _token_major(vt, heads)
    else:
        o_a = _sb_sample(seq(qb), kt, vt, seq(ga), _feature_major(past_k), _feature_major(past_v))
        k_new, v_new = _sample_token_major(kt, b, heads), _sample_token_major(vt, b, heads)
    xbc3 = seq(xbc)
    o_b, h_new = _ssd(xbc3, seq(z), seq(dt_raw), conv_state, h0, conv_w, conv_b, dt_bias, a_log,
                      d_skip, norm_w, ssd_chunk)
    y, yb = _proj_ln([o_a.reshape(m, SB_WIDTH), o_b.reshape(m, SSD_WIDTH)], x2, w_out, ln_g, ln_b)
    tail = jnp.concatenate([conv_state, xbc3], axis=1)[:, -(SSD_CONV - 1):]
    return seq(y), seq(yb), k_new, v_new, tail, h_new.reshape(b, SSD_HEADS, HEAD_DIM, SSD_STATE)


def _odd_layer(x, xb, past_k, past_v, wts):
    w_in, w_kv_t, bias, w_out, ln_g, ln_b = wts
    b, l, _ = x.shape
    m = b * l
    x2 = x.reshape(m, D_MODEL)
    xb2 = xb.reshape(m, D_MODEL)
    keep = min(CA_LEFT, l)
    (qb,) = _matmul(xb2, w_in, 0, CA_WIDTH, (BF16,))
    kt_b, kt = _new_keys(w_kv_t, xb, 0, CA_WIDTH, past_k is not None, keep)
    vt_b, vt = _new_keys(w_kv_t, xb, CA_WIDTH, CA_WIDTH, past_k is not None, keep)
    (g,) = _matmul(xb2, w_in, 3 * CA_WIDTH, CA_WIDTH, (F32,))

    def seq(a):
        return a.reshape(b, l, a.shape[-1])

    if past_k is None:
        o = _band_prompt(seq(qb), kt_b, vt_b, seq(g), bias)
    else:
        o, kt, vt = _band_sample(seq(qb), kt, vt, _feature_major(past_k), _feature_major(past_v), seq(g),
                                 bias)
    y, _ = _proj_ln([o.reshape(m, CA_WIDTH)], x2, w_out, ln_g, ln_b)
    heads = CA_WIDTH // HEAD_DIM
    return seq(y), _token_major(kt, heads), _token_major(vt, heads)


def kernel(x_prompt, x_sample, cache_sb_k, cache_sb_v, state_ssm, state_conv, cache_band_k,
           cache_band_v, even_w_in, even_conv_w, even_conv_b, even_dt_bias, even_a_log, even_d_skip,
           even_norm_w, even_w_out, even_ln_g, even_ln_b, odd_w_in, odd_rel_bias, odd_w_out,
           odd_ln_g, odd_ln_b):
    bp, lp, _ = x_prompt.shape
    bs, ls, _ = x_sample.shape
    w_in_t0 = jnp.transpose(even_w_in[0])
    w_dt_t0 = jnp.pad(w_in_t0[L0_DT:].astype(BF16), ((0, LANES - SSD_HEADS), (0, 0)))
    wts0 = (w_in_t0, w_dt_t0, even_conv_w[0], even_conv_b[0], even_dt_bias[0], even_a_log[0],
            even_d_skip[0], even_norm_w[0], even_w_out[0].astype(BF16), even_ln_g[0], even_ln_b[0])
    zero_conv = jnp.zeros((bp, SSD_CONV - 1, SSD_CONV_DIM), F32)
    zero_h = jnp.zeros((bp, SSD_WIDTH, SSD_STATE), F32)
    yp, ypb, p_sb_k, p_sb_v, p_conv, p_ssm = _even_layer(
        x_prompt, None, None, zero_conv, zero_h, wts0, 2 * CHUNK)
    ys, ysb, s_sb_k, s_sb_v, s_conv, s_ssm = _even_layer(
        x_sample, cache_sb_k[0], cache_sb_v[0], state_conv[0],
        state_ssm[0].reshape(bs, SSD_WIDTH, SSD_STATE), wts0, CHUNK)
    w_in1 = odd_w_in[0]
    w_kv_t1 = jnp.transpose(odd_w_in[0][:, CA_WIDTH:3 * CA_WIDTH]).astype(BF16)
    w_out1 = odd_w_out[0].astype(BF16)
    wts1p = (w_in1, w_kv_t1, _band_bias(odd_rel_bias[0]), w_out1, odd_ln_g[0], odd_ln_b[0])
    wts1s = (w_in1, w_kv_t1, _band_sample_bias(odd_rel_bias[0], ls), w_out1, odd_ln_g[0], odd_ln_b[0])
    yp, p_band_k, p_band_v = _odd_layer(yp, ypb, None, None, wts1p)
    ys, s_band_k, s_band_v = _odd_layer(ys, ysb, cache_band_k[0], cache_band_v[0], wts1s)

    def lead(a):
        return a[None]

    return (yp, ys, lead(p_sb_k), lead(p_sb_v), lead(p_ssm), lead(p_conv), lead(p_band_k),
            lead(p_band_v), lead(s_sb_k), lead(s_sb_v), lead(s_ssm), lead(s_conv), lead(s_band_k),
            lead(s_band_v))
```

```python
import functools
import math

import jax
import jax.numpy as jnp
from jax import lax
from jax.experimental import pallas as pl
from jax.experimental.pallas import tpu as pltpu

F32 = jnp.float32
BF16 = jnp.bfloat16

D_MODEL = 2048
CHUNK = 64
HEAD_DIM = 64
LANES = 128
SB_WIDTH = 1024
SSD_WIDTH = 2048
SSD_HEADS = 32
SSD_GROUPS = 4
SSD_STATE = 128
SSD_CONV = 4
SSD_CONV_DIM = SSD_WIDTH + 2 * SSD_GROUPS * SSD_STATE
CA_WIDTH = 2048
CA_LEFT = 8 * CHUNK
CA_REL_CLIP = 128
DEPTH = 2
DEEPNORM_ALPHA = (2 * DEPTH) ** 0.25
LN_EPS = 1e-5
RMS_EPS = 1e-5
ATTN_SCALE = HEAD_DIM ** -0.5
NEG_BIG = -1e30
L0_Q, L0_K, L0_V, L0_G, L0_Z, L0_XBC, L0_DT = 0, 1024, 2048, 3072, 4096, 6144, 9216
VMEM_LIMIT = 56 * 1024 * 1024
ROW_TILE = 512
COL_TILE = 1024

NT_DIMS = (((1,), (1,)), ((), ()))
TN_DIMS = (((0,), (0,)), ((), ()))


def _params(*sem):
    return pltpu.CompilerParams(dimension_semantics=sem, vmem_limit_bytes=VMEM_LIMIT)


def _silu(x):
    return x * (1.0 / (1.0 + jnp.exp(-x)))


def _split3(x):
    p0 = x.astype(BF16)
    r = x - p0.astype(F32)
    p1 = r.astype(BF16)
    p2 = (r - p1.astype(F32)).astype(BF16)
    return p0, p1, p2


def _mm_kernel(x_ref, w_ref, *refs, has_side, w_is_t, copy_x):
    *o_refs, wb_ref = refs

    @pl.when(pl.program_id(1) == 0)
    def _():
        wb_ref[...] = w_ref[...].astype(BF16)

    def product(w):
        if w_is_t:
            return lax.dot_general(x, w, NT_DIMS, preferred_element_type=F32)
        return jnp.dot(x, w, preferred_element_type=F32)

    x = x_ref[...].astype(BF16)
    if copy_x:
        o_refs.pop()[...] = x
    if has_side:
        ws_ref = o_refs.pop(0)
        os_ref = o_refs.pop()
        os_ref[...] = product(ws_ref[...])
    acc = product(wb_ref[...])
    for o_ref in o_refs:
        o_ref[...] = acc.astype(o_ref.dtype)


def _matmul(x, w, col0, ncols, out_dtypes, w_side=None, w_is_t=False, copy_x=False):
    m, k = x.shape
    tm, tn = math.gcd(m, 2 * ROW_TILE), COL_TILE
    assert ncols % tn == 0 and col0 % tn == 0
    cb0 = col0 // tn
    if w_is_t:
        w_spec = pl.BlockSpec((tn, k), lambda j, i: (cb0 + j, 0))
    else:
        w_spec = pl.BlockSpec((k, tn), lambda j, i: (0, cb0 + j))
    in_specs = [pl.BlockSpec((tm, k), lambda j, i: (i, 0)), w_spec]
    out_specs = [pl.BlockSpec((tm, tn), lambda j, i: (i, j)) for _ in out_dtypes]
    out_shape = [jax.ShapeDtypeStruct((m, ncols), dt) for dt in out_dtypes]
    args = [x, w]
    if w_side is not None:
        assert ncols == tn
        ns = w_side.shape[0 if w_is_t else 1]
        in_specs.append(pl.BlockSpec(w_side.shape, lambda j, i: (0, 0)))
        out_specs.append(pl.BlockSpec((tm, ns), lambda j, i: (i, 0)))
        out_shape.append(jax.ShapeDtypeStruct((m, ns), F32))
        args.append(w_side)
    if copy_x:
        assert ncols == tn
        out_specs.append(pl.BlockSpec((tm, k), lambda j, i: (i, 0)))
        out_shape.append(jax.ShapeDtypeStruct((m, k), BF16))
    return pl.pallas_call(
        functools.partial(_mm_kernel, has_side=w_side is not None, w_is_t=w_is_t, copy_x=copy_x),
        grid=(ncols // tn, m // tm),
        in_specs=in_specs,
        out_specs=out_specs,
        out_shape=out_shape,
        scratch_shapes=[pltpu.VMEM(w_spec.block_shape, BF16)],
        compiler_params=_params("arbitrary", "arbitrary"),
        name=f"mm_c{col0}_n{ncols}_m{m}",
    )(*args)


def _mm_t_kernel(w_ref, x_ref, ob_ref, of_ref, wb_ref, *, first_kept):
    @pl.when(jnp.logical_and(pl.program_id(0) == 0, pl.program_id(1) == 0))
    def _():
        wb_ref[...] = w_ref[...].astype(BF16)

    acc = lax.dot_general(wb_ref[...], x_ref[0], NT_DIMS, preferred_element_type=F32)
    ob_ref[0] = acc.astype(ob_ref.dtype)

    def keep_f32():
        of_ref[0] = acc

    if first_kept == 0:
        keep_f32()
    else:
        pl.when(pl.program_id(1) >= first_kept)(keep_f32)


def _matmul_t(w_t, x, row0, nrows, keep):
    s, l, k = x.shape
    tm = math.gcd(math.gcd(l, keep), 2 * ROW_TILE if nrows <= COL_TILE else ROW_TILE)
    assert row0 % nrows == 0 and keep <= l
    first_kept = (l - keep) // tm
    return pl.pallas_call(
        functools.partial(_mm_t_kernel, first_kept=first_kept),
        grid=(s, l // tm),
        in_specs=[pl.BlockSpec((nrows, k), lambda si, i: (row0 // nrows, 0)),
                  pl.BlockSpec((1, tm, k), lambda si, i: (si, i, 0))],
        out_specs=[pl.BlockSpec((1, nrows, tm), lambda si, i: (si, 0, i)),
                   pl.BlockSpec((1, nrows, tm), lambda si, i: (si, 0, jnp.maximum(i - first_kept, 0)))],
        out_shape=[jax.ShapeDtypeStruct((s, nrows, l), BF16),
                   jax.ShapeDtypeStruct((s, nrows, keep), F32)],
        scratch_shapes=[pltpu.VMEM((nrows, k), BF16)],
        compiler_params=_params("arbitrary", "arbitrary"),
        name=f"mm_t_r{row0}_n{nrows}_l{l}",
    )(w_t, x)


def _proj_ln_kernel(*refs, widths):
    n = len(widths)
    o_refs = refs[:n]
    x_ref, w_ref, g_ref, b_ref, y_ref, yb_ref = refs[n:]
    mix = None
    row = 0
    for o_ref, wd in zip(o_refs, widths):
        part = jnp.dot(o_ref[...], w_ref[row:row + wd, :], preferred_element_type=F32)
        mix = part if mix is None else mix + part
        row += wd
    h = DEEPNORM_ALPHA * x_ref[...] + mix
    mu = jnp.mean(h, axis=-1, keepdims=True)
    hc = h - mu
    var = jnp.mean(hc * hc, axis=-1, keepdims=True)
    y = hc * lax.rsqrt(var + LN_EPS) * g_ref[...] + b_ref[...]
    y_ref[...] = y
    yb_ref[...] = y.astype(BF16)


def _proj_ln(o_parts, x, w_out, ln_g, ln_b):
    m = x.shape[0]
    widths = tuple(o.shape[1] for o in o_parts)
    ktot = sum(widths)
    tm = math.gcd(m, ROW_TILE)
    assert w_out.shape == (ktot, D_MODEL)
    in_specs = [pl.BlockSpec((tm, wd), lambda i: (i, 0)) for wd in widths]
    in_specs += [pl.BlockSpec((tm, D_MODEL), lambda i: (i, 0)),
                 pl.BlockSpec((ktot, D_MODEL), lambda i: (0, 0), pipeline_mode=pl.Buffered(1)),
                 pl.BlockSpec((1, D_MODEL), lambda i: (0, 0)),
                 pl.BlockSpec((1, D_MODEL), lambda i: (0, 0))]
    y, yb = pl.pallas_call(
        functools.partial(_proj_ln_kernel, widths=widths),
        grid=(m // tm,),
        in_specs=in_specs,
        out_specs=[pl.BlockSpec((tm, D_MODEL), lambda i: (i, 0)),
                   pl.BlockSpec((tm, D_MODEL), lambda i: (i, 0))],
        out_shape=[jax.ShapeDtypeStruct((m, D_MODEL), F32),
                   jax.ShapeDtypeStruct((m, D_MODEL), BF16)],
        compiler_params=_params("parallel"),
        name=f"proj_ln_k{ktot}_m{m}",
    )(*o_parts, x, w_out, ln_g.reshape(1, D_MODEL), ln_b.reshape(1, D_MODEL))
    return y, yb


def _head_masks(rows):
    lane = lax.broadcasted_iota(jnp.int32, (rows, LANES), 1)
    return lane < HEAD_DIM


SB_KEYS = 256
SB_STEP_LANES = 4 * LANES
SB_DEAD = 104.5
LOG2E = 1.4426950408889634


def _pair_queries(q2):
    head0 = _head_masks(q2.shape[0])
    qs = q2 * jnp.asarray(ATTN_SCALE, q2.dtype)
    zero = jnp.zeros_like(qs)
    return jnp.concatenate([jnp.where(head0, qs, zero), jnp.where(head0, zero, qs)], axis=0)


def _pair_outputs(stacked):
    tq = stacked.shape[0] // 2
    return jnp.where(_head_masks(tq), stacked[:tq], stacked[tq:])


def _sb_visit(qh, k_pairs, v_pairs, carries, accs, upper, diag_mask):
    zs = [jnp.dot(q2, kt, preferred_element_type=F32) for q2, kt in zip(qh, k_pairs)]
    sps = [jnp.maximum(z, 0.0) + jnp.log(1.0 + jnp.exp(-jnp.abs(z))) for z in zs]
    if diag_mask is not None:
        sps = [jnp.where(diag_mask, sp, 0.0) for sp in sps]
    masses = [jnp.dot(sp.astype(BF16), upper, preferred_element_type=F32) + c
              for sp, c in zip(sps, carries)]
    ws = [jnp.exp(z - mass) for z, mass in zip(zs, masses)]
    if diag_mask is not None:
        ws = [jnp.where(diag_mask, w, 0.0) for w in ws]
    accs = tuple(a + lax.dot_general(w.astype(BF16), vt, NT_DIMS, preferred_element_type=F32)
                 for a, w, vt in zip(accs, ws, v_pairs))
    return tuple(mass[:, 0:1] for mass in masses), accs


def _sb_upper(n):
    r = lax.broadcasted_iota(jnp.int32, (n, n), 0)
    c = lax.broadcasted_iota(jnp.int32, (n, n), 1)
    return (r >= c).astype(BF16)


def _sb_alive(carries):
    return jnp.min(functools.reduce(jnp.minimum, carries)) < SB_DEAD


def _sb_queries(q_ref):
    return [_pair_queries(q_ref[0, :, p * LANES:(p + 1) * LANES]) for p in range(q_ref.shape[2] // LANES)]


def _sb_strict(tq, keys):
    r = lax.broadcasted_iota(jnp.int32, (2 * tq, keys), 0)
    c = lax.broadcasted_iota(jnp.int32, (2 * tq, keys), 1)
    return c < jnp.where(r >= tq, r - tq, r)


def _sb_finish(accs, g_ref, o_ref):
    for p, acc in enumerate(accs):
        lanes = slice(p * LANES, (p + 1) * LANES)
        o_ref[0, :, lanes] = (_pair_outputs(acc) * _silu(g_ref[0, :, lanes])).astype(o_ref.dtype)


def _pair_slabs(ref, lead, n_pair, cols):
    return [ref[lead, p * LANES:(p + 1) * LANES, cols].astype(BF16) for p in range(n_pair)]


def _sb_prompt_kernel(q_ref, k_ref, v_ref, g_ref, o_ref):
    qi = pl.program_id(2)
    tq = q_ref.shape[1]
    assert tq == SB_KEYS
    qh = _sb_queries(q_ref)
    n_pair = len(qh)
    upper = _sb_upper(SB_KEYS)
    strict = _sb_strict(tq, SB_KEYS)

    def visit(kb, carries, accs, mask):
        cols = pl.ds(pl.multiple_of(kb * SB_KEYS, SB_KEYS), SB_KEYS)
        return _sb_visit(qh, _pair_slabs(k_ref, 0, n_pair, cols), _pair_slabs(v_ref, 0, n_pair, cols),
                         carries, accs, upper, mask)

    zc = (jnp.zeros((2 * tq, 1), F32),) * n_pair
    za = (jnp.zeros((2 * tq, LANES), F32),) * n_pair
    carries, accs = visit(qi, zc, za, strict)

    def older(state):
        kb, _, carries, accs = state
        carries, accs = visit(kb, carries, accs, None)
        return kb - 1, _sb_alive(carries), carries, accs

    state = lax.while_loop(lambda s: jnp.logical_and(s[0] >= 0, s[1]), older,
                           (qi - 1, _sb_alive(carries), carries, accs))
    _sb_finish(state[3], g_ref, o_ref)


def _sb_prompt(q, kt, vt, g):
    b, l, wdt = q.shape
    tq = SB_KEYS
    assert l % tq == 0 and wdt % SB_STEP_LANES == 0
    blk = pl.BlockSpec((1, tq, SB_STEP_LANES), lambda bi, hp, qi: (bi, qi, hp))
    full = pl.BlockSpec((1, SB_STEP_LANES, l), lambda bi, hp, qi: (bi, hp, 0))
    return pl.pallas_call(
        _sb_prompt_kernel,
        grid=(b, wdt // SB_STEP_LANES, l // tq),
        in_specs=[blk, full, full, blk],
        out_specs=blk,
        out_shape=jax.ShapeDtypeStruct((b, l, wdt), BF16),
        compiler_params=_params("parallel", "parallel", "arbitrary"),
        name="sb_prompt",
    )(q, kt, vt, g)


def _sb_sample_kernel(q_ref, kn_ref, vn_ref, g_ref, kc_hbm, vc_hbm, o_ref, kbuf, vbuf, sem, *, n_cached):
    seq = pl.program_id(0)
    tq = q_ref.shape[1]
    qh = _sb_queries(q_ref)
    n_pair = len(qh)
    upper = _sb_upper(SB_KEYS)
    strict = _sb_strict(tq, LANES)
    zc = (jnp.zeros((2 * tq, 1), F32),) * n_pair
    za = (jnp.zeros((2 * tq, LANES), F32),) * n_pair

    def block_copies(kb, slot):
        cols = pl.ds(pl.multiple_of(kb * SB_KEYS, SB_KEYS), SB_KEYS)
        return (pltpu.make_async_copy(kc_hbm.at[seq, :, cols], kbuf.at[slot], sem.at[0, slot]),
                pltpu.make_async_copy(vc_hbm.at[seq, :, cols], vbuf.at[slot], sem.at[1, slot]))

    def start_block(kb):
        for cp in block_copies(kb, kb % 2):
            cp.start()

    def wait_block(kb):
        for cp in block_copies(kb, kb % 2):
            cp.wait()

    start_block(n_cached - 1)
    everything = slice(None)

    def new_pairs(ref):
        return [_own_columns(ref[p * LANES:(p + 1) * LANES, :], seq, tq).astype(BF16) for p in range(n_pair)]

    carries, accs = _sb_visit(qh, new_pairs(kn_ref), new_pairs(vn_ref), zc, za, _sb_upper(LANES), strict)

    def older(state):
        kb, _, carries, accs = state
        wait_block(kb)
        pl.when(kb > 0)(lambda: start_block(kb - 1))
        slot = kb % 2
        carries, accs = _sb_visit(qh, _pair_slabs(kbuf, slot, n_pair, everything),
                                  _pair_slabs(vbuf, slot, n_pair, everything), carries, accs, upper, None)
        return kb - 1, _sb_alive(carries), carries, accs

    state = lax.while_loop(lambda s: jnp.logical_and(s[0] >= 0, s[1]), older,
                           (jnp.int32(n_cached - 1), _sb_alive(carries), carries, accs))
    pl.when(state[0] >= 0)(lambda: wait_block(state[0]))
    _sb_finish(state[3], g_ref, o_ref)


def _own_columns(block, seq, t):
    assert 2 * t == LANES
    return jnp.where(seq % 2 == 1, pltpu.roll(block, t, axis=1), block)


def _sb_sample(q, knt, vnt, g, kct, vct):
    b, t, wdt = q.shape
    p = kct.shape[2]
    assert p % SB_KEYS == 0 and knt.shape == (wdt, b * t)
    tok = pl.BlockSpec((1, t, wdt), lambda bi: (bi, 0, 0))
    new = pl.BlockSpec((wdt, LANES), lambda bi: (0, bi * t // LANES))
    hbm = pl.BlockSpec(memory_space=pl.ANY)
    return pl.pallas_call(
        functools.partial(_sb_sample_kernel, n_cached=p // SB_KEYS),
        grid=(b,),
        in_specs=[tok, new, new, tok, hbm, hbm],
        out_specs=tok,
        out_shape=jax.ShapeDtypeStruct((b, t, wdt), BF16),
        scratch_shapes=[pltpu.VMEM((2, wdt, SB_KEYS), F32),
                        pltpu.VMEM((2, wdt, SB_KEYS), F32),
                        pltpu.SemaphoreType.DMA((2, 2))],
        compiler_params=_params("arbitrary"),
        name="sb_sample",
    )(q, knt, vnt, g, kct, vct)


CONV_PAD = 8
CONV_SLAB = 512


def _ssd_kernel(xbc_ref, z_ref, dt_ref, cs_ref, h0_ref, cw_ref, cb_ref, dtb_ref, alog_ref, drow_ref,
                nw_ref, o_ref, h_ref, xp_ref, xa_ref, y_ref):
    ci = pl.program_id(1)
    q = xbc_ref.shape[1]
    taps = SSD_CONV - 1

    @pl.when(ci == 0)
    def _():
        xp_ref[CONV_PAD - taps:CONV_PAD, :] = cs_ref[0]
        h_ref[0] = h0_ref[0]

    @pl.when(ci > 0)
    def _():
        xp_ref[CONV_PAD - taps:CONV_PAD, :] = xp_ref[CONV_PAD + q - taps:CONV_PAD + q, :]

    xp_ref[CONV_PAD:CONV_PAD + q, :] = xbc_ref[0]
    for cs in range(0, SSD_CONV_DIM, CONV_SLAB):
        acc = cb_ref[:, cs:cs + CONV_SLAB] + cw_ref[0:1, cs:cs + CONV_SLAB] * xp_ref[
            CONV_PAD - taps:CONV_PAD - taps + q, cs:cs + CONV_SLAB]
        for t in range(1, SSD_CONV):
            acc = acc + cw_ref[t:t + 1, cs:cs + CONV_SLAB] * xp_ref[
                CONV_PAD - taps + t:CONV_PAD - taps + t + q, cs:cs + CONV_SLAB]
        xa_ref[:, cs:cs + CONV_SLAB] = _silu(acc)

    dtv = dt_ref[0] + dtb_ref[...]
    dtv = jnp.maximum(dtv, 0.0) + jnp.log1p(jnp.exp(-jnp.abs(dtv)))
    da = dtv * (-jnp.exp(alog_ref[...]))
    r = lax.broadcasted_iota(jnp.int32, (q, q), 0)
    c = lax.broadcasted_iota(jnp.int32, (q, q), 1)
    causal = c <= r
    tril = causal.astype(BF16)
    cum = sum(jnp.dot(tril, p, preferred_element_type=F32) for p in _split3(da))
    er = lax.broadcasted_iota(jnp.int32, (LANES, LANES), 0)
    ec = lax.broadcasted_iota(jnp.int32, (LANES, LANES), 1)
    eye = (er == ec).astype(BF16)

    def transpose(x):
        return sum(lax.dot_general(eye, p, NT_DIMS, preferred_element_type=F32) for p in _split3(x))

    cum2 = cum * LOG2E
    cum2_t = transpose(cum) * LOG2E
    dt_t = transpose(dtv)
    cum_last = cum[q - 1:q, :]
    to_end = jnp.exp(cum_last - cum) * dtv
    exp_cum = jnp.exp(cum)
    chunk_decay = jnp.broadcast_to(jnp.exp(cum_last), (LANES, LANES))
    head0 = _head_masks(q)
    top = lax.broadcasted_iota(jnp.int32, (LANES, LANES), 0) < HEAD_DIM

    for g in range(SSD_GROUPS):
        b0 = SSD_WIDTH + g * SSD_STATE
        c0 = SSD_WIDTH + SSD_GROUPS * SSD_STATE + g * SSD_STATE
        bm = xa_ref[:, b0:b0 + SSD_STATE].astype(BF16)
        cm = xa_ref[:, c0:c0 + SSD_STATE].astype(BF16)
        cb = lax.dot_general(cm, bm, NT_DIMS, preferred_element_type=F32)
        for j in range(SSD_HEADS // SSD_GROUPS // 2):
            pr = g * (SSD_HEADS // SSD_GROUPS // 2) + j
            h0, h1 = 2 * pr, 2 * pr + 1
            lo = pr * LANES
            x2 = xa_ref[:, lo:lo + LANES]
            x2b = x2.astype(BF16)
            yd = []
            for h in (h0, h1):
                seg = cum2[:, h:h + 1] - cum2_t[h:h + 1, :]
                wgt = cb * jnp.exp2(jnp.where(causal, seg, -jnp.inf)) * dt_t[h:h + 1, :]
                yd.append(jnp.dot(wgt.astype(BF16), x2b, preferred_element_type=F32))
            y = jnp.where(head0, yd[0], yd[1])
            hs = h_ref[0, lo:lo + LANES, :]
            yoff = lax.dot_general(cm, hs.astype(BF16), NT_DIMS, preferred_element_type=F32)
            y = y + yoff * jnp.where(head0, exp_cum[:, h0:h0 + 1], exp_cum[:, h1:h1 + 1])
            y_ref[:, lo:lo + LANES] = y + drow_ref[:, lo:lo + LANES] * x2
            xw = x2 * jnp.where(head0, to_end[:, h0:h0 + 1], to_end[:, h1:h1 + 1])
            st = lax.dot_general(xw.astype(BF16), bm, TN_DIMS, preferred_element_type=F32)
            dec = jnp.where(top, chunk_decay[:, h0:h0 + 1], chunk_decay[:, h1:h1 + 1])
            h_ref[0, lo:lo + LANES, :] = dec * hs + st

    gw = SSD_WIDTH // SSD_GROUPS
    for g in range(SSD_GROUPS):
        yg = y_ref[:, g * gw:(g + 1) * gw] * _silu(z_ref[0, :, g * gw:(g + 1) * gw])
        ms = jnp.mean(yg * yg, axis=-1, keepdims=True)
        o_ref[0, :, g * gw:(g + 1) * gw] = (yg * lax.rsqrt(ms + RMS_EPS)
                                            * nw_ref[:, g * gw:(g + 1) * gw]).astype(o_ref.dtype)


def _ssd(xbc, z, dt_raw, conv_state, h0, conv_w, conv_b, dt_bias, a_log, d_skip, norm_w, q):
    b, l, _ = xbc.shape
    assert l % q == 0
    pad = LANES - SSD_HEADS
    dtb = jnp.pad(dt_bias, (0, pad)).reshape(1, LANES)
    alog = jnp.pad(a_log, (0, pad)).reshape(1, LANES)
    drow = jnp.repeat(d_skip, HEAD_DIM).reshape(1, SSD_WIDTH)

    def tok(wd):
        return pl.BlockSpec((1, q, wd), lambda bi, ci: (bi, ci, 0))

    def per_seq(shape):
        return pl.BlockSpec((1,) + shape, lambda bi, ci: (bi, 0, 0))

    def const(shape):
        return pl.BlockSpec(shape, lambda bi, ci: (0, 0))

    o, h = pl.pallas_call(
        _ssd_kernel,
        grid=(b, l // q),
        in_specs=[tok(SSD_CONV_DIM), tok(SSD_WIDTH), tok(LANES),
                  per_seq((SSD_CONV - 1, SSD_CONV_DIM)), per_seq((SSD_WIDTH, SSD_STATE)),
                  const((SSD_CONV, SSD_CONV_DIM)), const((1, SSD_CONV_DIM)), const((1, LANES)),
                  const((1, LANES)), const((1, SSD_WIDTH)), const((1, SSD_WIDTH))],
        out_specs=[tok(SSD_WIDTH), per_seq((SSD_WIDTH, SSD_STATE))],
        out_shape=[jax.ShapeDtypeStruct((b, l, SSD_WIDTH), BF16),
                   jax.ShapeDtypeStruct((b, SSD_WIDTH, SSD_STATE), F32)],
        scratch_shapes=[pltpu.VMEM((CONV_PAD + q, SSD_CONV_DIM), F32),
                        pltpu.VMEM((q, SSD_CONV_DIM), F32),
                        pltpu.VMEM((q, SSD_WIDTH), F32)],
        compiler_params=_params("parallel", "arbitrary"),
        name=f"ssd_q{q}",
    )(xbc, z, dt_raw, conv_state, h0, conv_w, conv_b.reshape(1, SSD_CONV_DIM), dtb, alog, drow,
      norm_w.reshape(1, SSD_WIDTH))
    return o, h


BAND_BLOCKS = CA_LEFT // LANES + 1
BAND_PROMPT_PAIRS = 2


def _band_attend(jobs):
    scores = [jnp.dot(_pair_queries(q2), kt, preferred_element_type=F32) + bias2
              for q2, kt, _, bias2, _ in jobs]
    probs = [jnp.exp(s - jnp.max(s, axis=-1, keepdims=True)) for s in scores]
    dens = [jnp.sum(e, axis=-1, keepdims=True) for e in probs]
    nums = [lax.dot_general(e.astype(BF16), job[2], NT_DIMS, preferred_element_type=F32)
            for e, job in zip(probs, jobs)]
    return [_pair_outputs(num / den) * _silu(job[4]) for num, den, job in zip(nums, dens, jobs)]


def _band_prompt_kernel(q_ref, k_ref, v_ref, g_ref, bias_ref, o_ref):
    step = pl.program_id(2)
    n_sub = q_ref.shape[1] // LANES
    n_pair = q_ref.shape[2] // LANES
    band = BAND_BLOCKS * LANES

    def tile(n):
        return slice(n * LANES, (n + 1) * LANES)

    def sub_tile(p, s, i):
        if isinstance(i, int):
            keys = min(i + 1, BAND_BLOCKS) * LANES
            cols = pl.ds((i + 1) * LANES - keys, keys)
        else:
            keys = band
            cols = pl.ds(pl.multiple_of((i + 1) * LANES - band, LANES), band)
        bias2 = jnp.concatenate([bias_ref[2 * p, :, band - keys:], bias_ref[2 * p + 1, :, band - keys:]],
                                axis=0)
        return (q_ref[0, tile(s), tile(p)], k_ref[0, tile(p), cols], v_ref[0, tile(p), cols], bias2,
                g_ref[0, tile(s), tile(p)])

    def step_tiles(first):
        for p in range(n_pair):
            outs = _band_attend([sub_tile(p, s, first + s) for s in range(n_sub)])
            for s, o in enumerate(outs):
                o_ref[0, tile(s), tile(p)] = o.astype(o_ref.dtype)

    pl.when(step == 0)(lambda: step_tiles(0))
    pl.when(step > 0)(lambda: step_tiles(step * n_sub))


def _bias_lookup(rel_bias, t_minus_s):
    rel = jnp.clip(t_minus_s, -CA_REL_CLIP, CA_REL_CLIP) + CA_REL_CLIP
    onehot = (rel[..., None] == jnp.arange(2 * CA_REL_CLIP + 1)).astype(F32)
    return jnp.einsum("rck,hk->hrc", onehot, rel_bias, precision=lax.Precision.HIGHEST)


def _band_bias(rel_bias):
    r = jnp.arange(LANES)[:, None]
    u = jnp.arange(LANES)[None, :]

    def tile(dist):
        return _bias_lookup(rel_bias, dist * LANES + r - u)

    t0 = jnp.where((r < CHUNK) & (u >= CHUNK), NEG_BIG, tile(0))
    t1 = tile(1)
    t2 = jnp.broadcast_to(rel_bias[:, -1][:, None, None], t1.shape)
    t4 = jnp.where((r >= CHUNK) & (u < CHUNK), NEG_BIG, t2)
    return jnp.concatenate([t4, t2, t2, t1, t0], axis=-1).astype(F32)


def _band_prompt(q, kt, vt, g, band_bias):
    b, l, wdt = q.shape
    n_sub = math.gcd(l // LANES, BAND_BLOCKS - 1)
    assert n_sub == BAND_BLOCKS - 1 or l // LANES == n_sub
    tq = n_sub * LANES
    step_lanes = BAND_PROMPT_PAIRS * LANES
    assert wdt % step_lanes == 0
    blk = pl.BlockSpec((1, tq, step_lanes), lambda bi, hp, i: (bi, i, hp))
    full = pl.BlockSpec((1, step_lanes, l), lambda bi, hp, i: (bi, hp, 0))
    bias = pl.BlockSpec((2 * BAND_PROMPT_PAIRS, LANES, BAND_BLOCKS * LANES), lambda bi, hp, i: (hp, 0, 0))
    return pl.pallas_call(
        _band_prompt_kernel,
        grid=(b, wdt // step_lanes, l // tq),
        in_specs=[blk, full, full, blk, bias],
        out_specs=blk,
        out_shape=jax.ShapeDtypeStruct((b, l, wdt), BF16),
        compiler_params=_params("parallel", "parallel", "arbitrary"),
        name="band_prompt",
    )(q, kt, vt, g, band_bias)


BAND_SAMPLE_PAIRS = 4


def _band_sample_kernel(q_ref, kn_ref, vn_ref, kc_ref, vc_ref, g_ref, bias_ref, o_ref, ok_ref, ov_ref):
    seq = pl.program_id(0)
    t = q_ref.shape[1]
    p_len = kc_ref.shape[2]
    lane = lax.broadcasted_iota(jnp.int32, (kc_ref.shape[1], LANES), 1)
    k_new = _own_columns(kn_ref[...], seq, t)
    v_new = _own_columns(vn_ref[...], seq, t)
    for cache_ref, new, out_ref in ((kc_ref, k_new, ok_ref), (vc_ref, v_new, ov_ref)):
        shifted = pltpu.roll(cache_ref[0], p_len - t, axis=1)
        moved = pltpu.roll(new, LANES - t, axis=1)
        out_ref[0, :, :p_len - LANES] = shifted[:, :p_len - LANES]
        out_ref[0, :, p_len - LANES:] = jnp.where(lane < LANES - t, shifted[:, p_len - LANES:], moved)

    def lanes(p):
        return slice(p * LANES, (p + 1) * LANES)

    def pair(p):
        kt = jnp.concatenate([kc_ref[0, lanes(p), :], k_new[lanes(p)]], axis=1).astype(BF16)
        vt = jnp.concatenate([vc_ref[0, lanes(p), :], v_new[lanes(p)]], axis=1).astype(BF16)
        bias2 = jnp.concatenate([bias_ref[2 * p], bias_ref[2 * p + 1]], axis=0)
        return q_ref[0, :, lanes(p)], kt, vt, bias2, g_ref[0, :, lanes(p)]

    for p, o in enumerate(_band_attend([pair(p) for p in range(q_ref.shape[2] // LANES)])):
        o_ref[0, :, lanes(p)] = o.astype(o_ref.dtype)


def _band_sample_bias(rel_bias, tq):
    qi = jnp.arange(tq)[:, None]
    kj = jnp.arange(CA_LEFT + LANES)[None, :]
    return jnp.where(kj < CA_LEFT + tq, _bias_lookup(rel_bias, qi - kj + CA_LEFT), NEG_BIG).astype(F32)


def _band_sample(q, knt, vnt, kct, vct, g, bias):
    b, t, wdt = q.shape
    p = kct.shape[2]
    step_lanes = BAND_SAMPLE_PAIRS * LANES
    assert p == CA_LEFT and t == CHUNK and wdt % step_lanes == 0 and knt.shape == (wdt, b * t)
    tok = pl.BlockSpec((1, t, step_lanes), lambda bi, hp: (bi, 0, hp))
    new = pl.BlockSpec((step_lanes, LANES), lambda bi, hp: (hp, bi * t // LANES))
    old = pl.BlockSpec((1, step_lanes, p), lambda bi, hp: (bi, hp, 0))
    bspec = pl.BlockSpec((2 * BAND_SAMPLE_PAIRS, t, CA_LEFT + LANES), lambda bi, hp: (hp, 0, 0))
    return pl.pallas_call(
        _band_sample_kernel,
        grid=(b, wdt // step_lanes),
        in_specs=[tok, new, new, old, old, tok, bspec],
        out_specs=[tok, old, old],
        out_shape=[jax.ShapeDtypeStruct((b, t, wdt), BF16), jax.ShapeDtypeStruct(kct.shape, F32),
                   jax.ShapeDtypeStruct(vct.shape, F32)],
        compiler_params=_params("parallel", "parallel"),
        name="band_sample",
    )(q, knt, vnt, kct, vct, g, bias)


def _feature_major(cache):
    b, p, h, d = cache.shape
    return jnp.transpose(cache, (0, 2, 3, 1)).reshape(b, h * d, p)


def _token_major(kt, heads):
    b, _, t = kt.shape
    return jnp.transpose(kt.reshape(b, heads, HEAD_DIM, t), (0, 3, 1, 2))


def _new_keys(w_t, xb, row0, nrows, has_cache, keep):
    b, l, d = xb.shape
    if not has_cache:
        return _matmul_t(w_t, xb, row0, nrows, keep)
    _, kt = _matmul_t(w_t, xb.reshape(1, b * l, d), row0, nrows, b * l)
    return None, kt[0]


def _sample_token_major(kt, b, heads):
    t = kt.shape[1] // b
    return jnp.transpose(kt.reshape(heads, HEAD_DIM, b, t), (2, 3, 0, 1))


def _even_layer(x, past_k, past_v, conv_state, h0, wts, ssd_chunk):
    w_in_t, w_dt_t, conv_w, conv_b, dt_bias, a_log, d_skip, norm_w, w_out, ln_g, ln_b = wts
    b, l, _ = x.shape
    m = b * l
    heads = SB_WIDTH // HEAD_DIM
    x2 = x.reshape(m, D_MODEL)
    qb, xb2 = _matmul(x2, w_in_t, L0_Q, SB_WIDTH, (BF16,), w_is_t=True, copy_x=True)
    xb = xb2.reshape(b, l, D_MODEL)
    kt_b, kt = _new_keys(w_in_t, xb, L0_K, SB_WIDTH, past_k is not None, l)
    vt_b, vt = _new_keys(w_in_t, xb, L0_V, SB_WIDTH, past_k is not None, l)
    ga, dt_raw = _matmul(xb2, w_in_t, L0_G, SB_WIDTH, (F32,), w_side=w_dt_t, w_is_t=True)
    (z,) = _matmul(xb2, w_in_t, L0_Z, SSD_WIDTH, (F32,), w_is_t=True)
    (xbc,) = _matmul(xb2, w_in_t, L0_XBC, SSD_CONV_DIM, (F32,), w_is_t=True)

    def seq(a):
        return a.reshape(b, l, a.shape[-1])

    if past_k is None:
        o_a = _sb_prompt(seq(qb), kt_b, vt_b, seq(ga))
        k_new, v_new = _token_major(kt, heads), _token_major(vt, heads)
    else:
        o_a = _sb_sample(seq(qb), kt, vt, seq(ga), _feature_major(past_k), _feature_major(past_v))
        k_new, v_new = _sample_token_major(kt, b, heads), _sample_token_major(vt, b, heads)
    xbc3 = seq(xbc)
    o_b, h_new = _ssd(xbc3, seq(z), seq(dt_raw), conv_state, h0, conv_w, conv_b, dt_bias, a_log,
                      d_skip, norm_w, ssd_chunk)
    y, yb = _proj_ln([o_a.reshape(m, SB_WIDTH), o_b.reshape(m, SSD_WIDTH)], x2, w_out, ln_g, ln_b)
    tail = jnp.concatenate([conv_state, xbc3], axis=1)[:, -(SSD_CONV - 1):]
    return seq(y), seq(yb), k_new, v_new, tail, h_new.reshape(b, SSD_HEADS, HEAD_DIM, SSD_STATE)


def _odd_layer(x, xb, past_k, past_v, wts):
    w_in, w_kv_t, bias, w_out, ln_g, ln_b = wts
    b, l, _ = x.shape
    m = b * l
    x2 = x.reshape(m, D_MODEL)
    xb2 = xb.reshape(m, D_MODEL)
    keep = min(CA_LEFT, l)
    (qb,) = _matmul(xb2, w_in, 0, CA_WIDTH, (BF16,))
    kt_b, kt = _new_keys(w_kv_t, xb, 0, CA_WIDTH, past_k is not None, keep)
    vt_b, vt = _new_keys(w_kv_t, xb, CA_WIDTH, CA_WIDTH, past_k is not None, keep)
    (g,) = _matmul(xb2, w_in, 3 * CA_WIDTH, CA_WIDTH, (F32,))

    def seq(a):
        return a.reshape(b, l, a.shape[-1])

    if past_k is None:
        o = _band_prompt(seq(qb), kt_b, vt_b, seq(g), bias)
    else:
        o, kt, vt = _band_sample(seq(qb), kt, vt, _feature_major(past_k), _feature_major(past_v), seq(g),
                                 bias)
    y, _ = _proj_ln([o.reshape(m, CA_WIDTH)], x2, w_out, ln_g, ln_b)
    heads = CA_WIDTH // HEAD_DIM
    return seq(y), _token_major(kt, heads), _token_major(vt, heads)


def kernel(x_prompt, x_sample, cache_sb_k, cache_sb_v, state_ssm, state_conv, cache_band_k,
           cache_band_v, even_w_in, even_conv_w, even_conv_b, even_dt_bias, even_a_log, even_d_skip,
           even_norm_w, even_w_out, even_ln_g, even_ln_b, odd_w_in, odd_rel_bias, odd_w_out,
           odd_ln_g, odd_ln_b):
    bp, lp, _ = x_prompt.shape
    bs, ls, _ = x_sample.shape
    w_in_t0 = jnp.transpose(even_w_in[0])
    w_dt_t0 = jnp.pad(w_in_t0[L0_DT:].astype(BF16), ((0, LANES - SSD_HEADS), (0, 0)))
    wts0 = (w_in_t0, w_dt_t0, even_conv_w[0], even_conv_b[0], even_dt_bias[0], even_a_log[0],
            even_d_skip[0], even_norm_w[0], even_w_out[0].astype(BF16), even_ln_g[0], even_ln_b[0])
    zero_conv = jnp.zeros((bp, SSD_CONV - 1, SSD_CONV_DIM), F32)
    zero_h = jnp.zeros((bp, SSD_WIDTH, SSD_STATE), F32)
    yp, ypb, p_sb_k, p_sb_v, p_conv, p_ssm = _even_layer(
        x_prompt, None, None, zero_conv, zero_h, wts0, 2 * CHUNK)
    ys, ysb, s_sb_k, s_sb_v, s_conv, s_ssm = _even_layer(
        x_sample, cache_sb_k[0], cache_sb_v[0], state_conv[0],
        state_ssm[0].reshape(bs, SSD_WIDTH, SSD_STATE), wts0, CHUNK)
    w_in1 = odd_w_in[0]
    w_kv_t1 = jnp.transpose(odd_w_in[0][:, CA_WIDTH:3 * CA_WIDTH]).astype(BF16)
    w_out1 = odd_w_out[0].astype(BF16)
    wts1p = (w_in1, w_kv_t1, _band_bias(odd_rel_bias[0]), w_out1, odd_ln_g[0], odd_ln_b[0])
    wts1s = (w_in1, w_kv_t1, _band_sample_bias(odd_rel_bias[0], ls), w_out1, odd_ln_g[0], odd_ln_b[0])
    yp, p_band_k, p_band_v = _odd_layer(yp, ypb, None, None, wts1p)
    ys, s_band_k, s_band_v = _odd_layer(ys, ysb, cache_band_k[0], cache_band_v[0], wts1s)

    def lead(a):
        return a[None]

    return (yp, ys, lead(p_sb_k), lead(p_sb_v), lead(p_ssm), lead(p_conv), lead(p_band_k),
            lead(p_band_v), lead(s_sb_k), lead(s_sb_v), lead(s_ssm), lead(s_conv), lead(s_band_k),
            lead(s_band_v))
```

```python
import functools
import math

import jax
import jax.numpy as jnp
from jax import lax
from jax.experimental import pallas as pl
from jax.experimental.pallas import tpu as pltpu

F32 = jnp.float32
BF16 = jnp.bfloat16

D_MODEL = 2048
CHUNK = 64
HEAD_DIM = 64
LANES = 128
SB_WIDTH = 1024
SSD_WIDTH = 2048
SSD_HEADS = 32
SSD_GROUPS = 4
SSD_STATE = 128
SSD_CONV = 4
SSD_CONV_DIM = SSD_WIDTH + 2 * SSD_GROUPS * SSD_STATE
CA_WIDTH = 2048
CA_LEFT = 8 * CHUNK
CA_REL_CLIP = 128
DEPTH = 2
DEEPNORM_ALPHA = (2 * DEPTH) ** 0.25
LN_EPS = 1e-5
RMS_EPS = 1e-5
ATTN_SCALE = HEAD_DIM ** -0.5
NEG_BIG = -1e30
L0_Q, L0_K, L0_V, L0_G, L0_Z, L0_XBC, L0_DT = 0, 1024, 2048, 3072, 4096, 6144, 9216
VMEM_LIMIT = 56 * 1024 * 1024
ROW_TILE = 512
COL_TILE = 1024

NT_DIMS = (((1,), (1,)), ((), ()))
TN_DIMS = (((0,), (0,)), ((), ()))


def _params(*sem):
    return pltpu.CompilerParams(dimension_semantics=sem, vmem_limit_bytes=VMEM_LIMIT)


def _silu(x):
    return x * (1.0 / (1.0 + jnp.exp(-x)))


def _split3(x):
    p0 = x.astype(BF16)
    r = x - p0.astype(F32)
    p1 = r.astype(BF16)
    p2 = (r - p1.astype(F32)).astype(BF16)
    return p0, p1, p2


def _mm_kernel(x_ref, w_ref, *refs, has_side, w_is_t, copy_x):
    *o_refs, wb_ref = refs

    @pl.when(pl.program_id(1) == 0)
    def _():
        wb_ref[...] = w_ref[...].astype(BF16)

    def product(w):
        if w_is_t:
            return lax.dot_general(x, w, NT_DIMS, preferred_element_type=F32)
        return jnp.dot(x, w, preferred_element_type=F32)

    x = x_ref[...].astype(BF16)
    if copy_x:
        o_refs.pop()[...] = x
    if has_side:
        ws_ref = o_refs.pop(0)
        os_ref = o_refs.pop()
        os_ref[...] = product(ws_ref[...])
    acc = product(wb_ref[...])
    for o_ref in o_refs:
        o_ref[...] = acc.astype(o_ref.dtype)


def _matmul(x, w, col0, ncols, out_dtypes, w_side=None, w_is_t=False, copy_x=False):
    m, k = x.shape
    tm, tn = math.gcd(m, 2 * ROW_TILE), COL_TILE
    assert ncols % tn == 0 and col0 % tn == 0
    cb0 = col0 // tn
    if w_is_t:
        w_spec = pl.BlockSpec((tn, k), lambda j, i: (cb0 + j, 0))
    else:
        w_spec = pl.BlockSpec((k, tn), lambda j, i: (0, cb0 + j))
    in_specs = [pl.BlockSpec((tm, k), lambda j, i: (i, 0)), w_spec]
    out_specs = [pl.BlockSpec((tm, tn), lambda j, i: (i, j)) for _ in out_dtypes]
    out_shape = [jax.ShapeDtypeStruct((m, ncols), dt) for dt in out_dtypes]
    args = [x, w]
    if w_side is not None:
        assert ncols == tn
        ns = w_side.shape[0 if w_is_t else 1]
        in_specs.append(pl.BlockSpec(w_side.shape, lambda j, i: (0, 0)))
        out_specs.append(pl.BlockSpec((tm, ns), lambda j, i: (i, 0)))
        out_shape.append(jax.ShapeDtypeStruct((m, ns), F32))
        args.append(w_side)
    if copy_x:
        assert ncols == tn
        out_specs.append(pl.BlockSpec((tm, k), lambda j, i: (i, 0)))
        out_shape.append(jax.ShapeDtypeStruct((m, k), BF16))
    return pl.pallas_call(
        functools.partial(_mm_kernel, has_side=w_side is not None, w_is_t=w_is_t, copy_x=copy_x),
        grid=(ncols // tn, m // tm),
        in_specs=in_specs,
        out_specs=out_specs,
        out_shape=out_shape,
        scratch_shapes=[pltpu.VMEM(w_spec.block_shape, BF16)],
        compiler_params=_params("arbitrary", "arbitrary"),
        name=f"mm_c{col0}_n{ncols}_m{m}",
    )(*args)


def _mm_t_kernel(w_ref, x_ref, ob_ref, of_ref, *scratch, first_kept):
    if scratch:
        (wb_ref,) = scratch

        @pl.when(jnp.logical_and(pl.program_id(0) == 0, pl.program_id(1) == 0))
        def _():
            wb_ref[...] = w_ref[...].astype(BF16)

        w = wb_ref[...]
    else:
        w = w_ref[...]
    acc = lax.dot_general(w, x_ref[0], NT_DIMS, preferred_element_type=F32)
    ob_ref[0] = acc.astype(ob_ref.dtype)
    kept_cols = of_ref.shape[2]

    def keep_f32():
        of_ref[0] = acc[:, acc.shape[1] - kept_cols:]

    if first_kept == 0:
        keep_f32()
    else:
        pl.when(pl.program_id(1) >= first_kept)(keep_f32)


def _matmul_t(w_t, x, row0, nrows, keep):
    s, l, k = x.shape
    tm = math.gcd(l, 2 * ROW_TILE)
    kept_cols = min(keep, tm)
    assert row0 % nrows == 0 and keep <= l and keep % kept_cols == 0 and tm % kept_cols == 0
    first_kept = (l - keep) // tm
    return pl.pallas_call(
        functools.partial(_mm_t_kernel, first_kept=first_kept),
        grid=(s, l // tm),
        in_specs=[pl.BlockSpec((nrows, k), lambda si, i: (row0 // nrows, 0), pipeline_mode=pl.Buffered(1)),
                  pl.BlockSpec((1, tm, k), lambda si, i: (si, i, 0))],
        out_specs=[pl.BlockSpec((1, nrows, tm), lambda si, i: (si, 0, i)),
                   pl.BlockSpec((1, nrows, kept_cols),
                                lambda si, i: (si, 0, jnp.maximum(i - first_kept, 0)))],
        out_shape=[jax.ShapeDtypeStruct((s, nrows, l), BF16),
                   jax.ShapeDtypeStruct((s, nrows, keep), F32)],
        scratch_shapes=[] if w_t.dtype == BF16 else [pltpu.VMEM((nrows, k), BF16)],
        compiler_params=_params("arbitrary", "arbitrary"),
        name=f"mm_t_r{row0}_n{nrows}_l{l}",
    )(w_t, x)


def _proj_ln_kernel(*refs, widths):
    n = len(widths)
    o_refs = refs[:n]
    x_ref, w_ref, g_ref, b_ref, y_ref, yb_ref = refs[n:]
    mix = None
    row = 0
    for o_ref, wd in zip(o_refs, widths):
        part = jnp.dot(o_ref[...], w_ref[row:row + wd, :], preferred_element_type=F32)
        mix = part if mix is None else mix + part
        row += wd
    h = DEEPNORM_ALPHA * x_ref[...] + mix
    mu = jnp.mean(h, axis=-1, keepdims=True)
    hc = h - mu
    var = jnp.mean(hc * hc, axis=-1, keepdims=True)
    y = hc * lax.rsqrt(var + LN_EPS) * g_ref[...] + b_ref[...]
    y_ref[...] = y
    yb_ref[...] = y.astype(BF16)


def _proj_ln(o_parts, x, w_out, ln_g, ln_b):
    m = x.shape[0]
    widths = tuple(o.shape[1] for o in o_parts)
    ktot = sum(widths)
    tm = math.gcd(m, ROW_TILE)
    assert w_out.shape == (ktot, D_MODEL)
    in_specs = [pl.BlockSpec((tm, wd), lambda i: (i, 0)) for wd in widths]
    in_specs += [pl.BlockSpec((tm, D_MODEL), lambda i: (i, 0)),
                 pl.BlockSpec((ktot, D_MODEL), lambda i: (0, 0), pipeline_mode=pl.Buffered(1)),
                 pl.BlockSpec((1, D_MODEL), lambda i: (0, 0)),
                 pl.BlockSpec((1, D_MODEL), lambda i: (0, 0))]
    y, yb = pl.pallas_call(
        functools.partial(_proj_ln_kernel, widths=widths),
        grid=(m // tm,),
        in_specs=in_specs,
        out_specs=[pl.BlockSpec((tm, D_MODEL), lambda i: (i, 0)),
                   pl.BlockSpec((tm, D_MODEL), lambda i: (i, 0))],
        out_shape=[jax.ShapeDtypeStruct((m, D_MODEL), F32),
                   jax.ShapeDtypeStruct((m, D_MODEL), BF16)],
        compiler_params=_params("parallel"),
        name=f"proj_ln_k{ktot}_m{m}",
    )(*o_parts, x, w_out, ln_g.reshape(1, D_MODEL), ln_b.reshape(1, D_MODEL))
    return y, yb


def _head_masks(rows):
    lane = lax.broadcasted_iota(jnp.int32, (rows, LANES), 1)
    return lane < HEAD_DIM


SB_KEYS = 256
SB_STEP_LANES = 4 * LANES
SB_DEAD = 104.5
LOG2E = 1.4426950408889634


def _pair_queries(q2):
    head0 = _head_masks(q2.shape[0])
    qs = q2 * jnp.asarray(ATTN_SCALE, q2.dtype)
    zero = jnp.zeros_like(qs)
    return jnp.concatenate([jnp.where(head0, qs, zero), jnp.where(head0, zero, qs)], axis=0)


def _pair_outputs(stacked):
    tq = stacked.shape[0] // 2
    return jnp.where(_head_masks(tq), stacked[:tq], stacked[tq:])


def _sb_visit(qh, k_pairs, v_pairs, carries, accs, upper, diag_mask):
    zs = [jnp.dot(q2, kt, preferred_element_type=F32) for q2, kt in zip(qh, k_pairs)]
    sps = [jnp.maximum(z, 0.0) + jnp.log(1.0 + jnp.exp(-jnp.abs(z))) for z in zs]
    if diag_mask is not None:
        sps = [jnp.where(diag_mask, sp, 0.0) for sp in sps]
    masses = [jnp.dot(sp.astype(BF16), upper, preferred_element_type=F32) + c
              for sp, c in zip(sps, carries)]
    ws = [jnp.exp(z - mass) for z, mass in zip(zs, masses)]
    if diag_mask is not None:
        ws = [jnp.where(diag_mask, w, 0.0) for w in ws]
    accs = tuple(a + lax.dot_general(w.astype(BF16), vt, NT_DIMS, preferred_element_type=F32)
                 for a, w, vt in zip(accs, ws, v_pairs))
    return tuple(mass[:, 0:1] for mass in masses), accs


def _sb_upper(n):
    r = lax.broadcasted_iota(jnp.int32, (n, n), 0)
    c = lax.broadcasted_iota(jnp.int32, (n, n), 1)
    return (r >= c).astype(BF16)


def _sb_alive(carries):
    return jnp.min(functools.reduce(jnp.minimum, carries)) < SB_DEAD


def _sb_queries(q_ref):
    return [_pair_queries(q_ref[0, :, p * LANES:(p + 1) * LANES]) for p in range(q_ref.shape[2] // LANES)]


def _sb_strict(tq, keys):
    r = lax.broadcasted_iota(jnp.int32, (2 * tq, keys), 0)
    c = lax.broadcasted_iota(jnp.int32, (2 * tq, keys), 1)
    return c < jnp.where(r >= tq, r - tq, r)


def _sb_finish(accs, g_ref, o_ref):
    for p, acc in enumerate(accs):
        lanes = slice(p * LANES, (p + 1) * LANES)
        o_ref[0, :, lanes] = (_pair_outputs(acc) * _silu(g_ref[0, :, lanes])).astype(o_ref.dtype)


def _pair_slabs(ref, lead, n_pair, cols):
    return [ref[lead, p * LANES:(p + 1) * LANES, cols].astype(BF16) for p in range(n_pair)]


def _sb_prompt_kernel(q_ref, k_ref, v_ref, g_ref, o_ref):
    qi = pl.program_id(2)
    tq = q_ref.shape[1]
    assert tq == SB_KEYS
    qh = _sb_queries(q_ref)
    n_pair = len(qh)
    upper = _sb_upper(SB_KEYS)
    strict = _sb_strict(tq, SB_KEYS)

    def visit(kb, carries, accs, mask):
        cols = pl.ds(pl.multiple_of(kb * SB_KEYS, SB_KEYS), SB_KEYS)
        return _sb_visit(qh, _pair_slabs(k_ref, 0, n_pair, cols), _pair_slabs(v_ref, 0, n_pair, cols),
                         carries, accs, upper, mask)

    zc = (jnp.zeros((2 * tq, 1), F32),) * n_pair
    za = (jnp.zeros((2 * tq, LANES), F32),) * n_pair
    carries, accs = visit(qi, zc, za, strict)

    def older(state):
        kb, _, carries, accs = state
        carries, accs = visit(kb, carries, accs, None)
        return kb - 1, _sb_alive(carries), carries, accs

    state = lax.while_loop(lambda s: jnp.logical_and(s[0] >= 0, s[1]), older,
                           (qi - 1, _sb_alive(carries), carries, accs))
    _sb_finish(state[3], g_ref, o_ref)


def _sb_prompt(q, kt, vt, g):
    b, l, wdt = q.shape
    tq = SB_KEYS
    assert l % tq == 0 and wdt % SB_STEP_LANES == 0
    blk = pl.BlockSpec((1, tq, SB_STEP_LANES), lambda bi, hp, qi: (bi, qi, hp))
    full = pl.BlockSpec((1, SB_STEP_LANES, l), lambda bi, hp, qi: (bi, hp, 0))
    return pl.pallas_call(
        _sb_prompt_kernel,
        grid=(b, wdt // SB_STEP_LANES, l // tq),
        in_specs=[blk, full, full, blk],
        out_specs=blk,
        out_shape=jax.ShapeDtypeStruct((b, l, wdt), BF16),
        compiler_params=_params("parallel", "parallel", "arbitrary"),
        name="sb_prompt",
    )(q, kt, vt, g)


def _sb_sample_kernel(q_ref, kn_ref, vn_ref, g_ref, kc_hbm, vc_hbm, o_ref, kbuf, vbuf, sem, *, n_cached):
    seq = pl.program_id(0)
    tq = q_ref.shape[1]
    qh = _sb_queries(q_ref)
    n_pair = len(qh)
    upper = _sb_upper(SB_KEYS)
    strict = _sb_strict(tq, LANES)
    zc = (jnp.zeros((2 * tq, 1), F32),) * n_pair
    za = (jnp.zeros((2 * tq, LANES), F32),) * n_pair

    def block_copies(kb, slot):
        cols = pl.ds(pl.multiple_of(kb * SB_KEYS, SB_KEYS), SB_KEYS)
        return (pltpu.make_async_copy(kc_hbm.at[seq, :, cols], kbuf.at[slot], sem.at[0, slot]),
                pltpu.make_async_copy(vc_hbm.at[seq, :, cols], vbuf.at[slot], sem.at[1, slot]))

    def start_block(kb):
        for cp in block_copies(kb, kb % 2):
            cp.start()

    def wait_block(kb):
        for cp in block_copies(kb, kb % 2):
            cp.wait()

    start_block(n_cached - 1)
    everything = slice(None)

    def new_pairs(ref):
        return [_own_columns(ref[p * LANES:(p + 1) * LANES, :], seq, tq).astype(BF16) for p in range(n_pair)]

    carries, accs = _sb_visit(qh, new_pairs(kn_ref), new_pairs(vn_ref), zc, za, _sb_upper(LANES), strict)

    def older(state):
        kb, _, carries, accs = state
        wait_block(kb)
        pl.when(kb > 0)(lambda: start_block(kb - 1))
        slot = kb % 2
        carries, accs = _sb_visit(qh, _pair_slabs(kbuf, slot, n_pair, everything),
                                  _pair_slabs(vbuf, slot, n_pair, everything), carries, accs, upper, None)
        return kb - 1, _sb_alive(carries), carries, accs

    state = lax.while_loop(lambda s: jnp.logical_and(s[0] >= 0, s[1]), older,
                           (jnp.int32(n_cached - 1), _sb_alive(carries), carries, accs))
    pl.when(state[0] >= 0)(lambda: wait_block(state[0]))
    _sb_finish(state[3], g_ref, o_ref)


def _own_columns(block, seq, t):
    assert 2 * t == LANES
    return jnp.where(seq % 2 == 1, pltpu.roll(block, t, axis=1), block)


def _sb_sample(q, knt, vnt, g, kct, vct):
    b, t, wdt = q.shape
    p = kct.shape[2]
    assert p % SB_KEYS == 0 and knt.shape == (wdt, b * t)
    tok = pl.BlockSpec((1, t, wdt), lambda bi: (bi, 0, 0))
    new = pl.BlockSpec((wdt, LANES), lambda bi: (0, bi * t // LANES))
    hbm = pl.BlockSpec(memory_space=pl.ANY)
    return pl.pallas_call(
        functools.partial(_sb_sample_kernel, n_cached=p // SB_KEYS),
        grid=(b,),
        in_specs=[tok, new, new, tok, hbm, hbm],
        out_specs=tok,
        out_shape=jax.ShapeDtypeStruct((b, t, wdt), BF16),
        scratch_shapes=[pltpu.VMEM((2, wdt, SB_KEYS), F32),
                        pltpu.VMEM((2, wdt, SB_KEYS), F32),
                        pltpu.SemaphoreType.DMA((2, 2))],
        compiler_params=_params("arbitrary"),
        name="sb_sample",
    )(q, knt, vnt, g, kct, vct)


CONV_PAD = 8
CONV_SLAB = 512


def _ssd_kernel(xbc_ref, z_ref, dt_ref, cs_ref, h0_ref, cw_ref, cb_ref, dtb_ref, alog_ref, drow_ref,
                nw_ref, o_ref, h_ref, xp_ref, xa_ref, y_ref):
    ci = pl.program_id(1)
    q = xbc_ref.shape[1]
    taps = SSD_CONV - 1

    @pl.when(ci == 0)
    def _():
        xp_ref[CONV_PAD - taps:CONV_PAD, :] = cs_ref[0]
        h_ref[0] = h0_ref[0]

    @pl.when(ci > 0)
    def _():
        xp_ref[CONV_PAD - taps:CONV_PAD, :] = xp_ref[CONV_PAD + q - taps:CONV_PAD + q, :]

    xp_ref[CONV_PAD:CONV_PAD + q, :] = xbc_ref[0]
    for cs in range(0, SSD_CONV_DIM, CONV_SLAB):
        acc = cb_ref[:, cs:cs + CONV_SLAB] + cw_ref[0:1, cs:cs + CONV_SLAB] * xp_ref[
            CONV_PAD - taps:CONV_PAD - taps + q, cs:cs + CONV_SLAB]
        for t in range(1, SSD_CONV):
            acc = acc + cw_ref[t:t + 1, cs:cs + CONV_SLAB] * xp_ref[
                CONV_PAD - taps + t:CONV_PAD - taps + t + q, cs:cs + CONV_SLAB]
        xa_ref[:, cs:cs + CONV_SLAB] = _silu(acc)

    dtv = dt_ref[0] + dtb_ref[...]
    dtv = jnp.maximum(dtv, 0.0) + jnp.log1p(jnp.exp(-jnp.abs(dtv)))
    da = dtv * (-jnp.exp(alog_ref[...]))
    r = lax.broadcasted_iota(jnp.int32, (q, q), 0)
    c = lax.broadcasted_iota(jnp.int32, (q, q), 1)
    causal = c <= r
    tril = causal.astype(BF16)
    cum = sum(jnp.dot(tril, p, preferred_element_type=F32) for p in _split3(da))
    er = lax.broadcasted_iota(jnp.int32, (LANES, LANES), 0)
    ec = lax.broadcasted_iota(jnp.int32, (LANES, LANES), 1)
    eye = (er == ec).astype(BF16)

    def transpose(x):
        return sum(lax.dot_general(eye, p, NT_DIMS, preferred_element_type=F32) for p in _split3(x))

    cum2 = cum * LOG2E
    cum2_t = transpose(cum) * LOG2E
    dt_t = transpose(dtv)
    cum_last = cum[q - 1:q, :]
    to_end = jnp.exp(cum_last - cum) * dtv
    exp_cum = jnp.exp(cum)
    chunk_decay = jnp.broadcast_to(jnp.exp(cum_last), (LANES, LANES))
    head0 = _head_masks(q)
    top = lax.broadcasted_iota(jnp.int32, (LANES, LANES), 0) < HEAD_DIM

    for g in range(SSD_GROUPS):
        b0 = SSD_WIDTH + g * SSD_STATE
        c0 = SSD_WIDTH + SSD_GROUPS * SSD_STATE + g * SSD_STATE
        bm = xa_ref[:, b0:b0 + SSD_STATE].astype(BF16)
        cm = xa_ref[:, c0:c0 + SSD_STATE].astype(BF16)
        cb = lax.dot_general(cm, bm, NT_DIMS, preferred_element_type=F32)
        for j in range(SSD_HEADS // SSD_GROUPS // 2):
            pr = g * (SSD_HEADS // SSD_GROUPS // 2) + j
            h0, h1 = 2 * pr, 2 * pr + 1
            lo = pr * LANES
            x2 = xa_ref[:, lo:lo + LANES]
            x2b = x2.astype(BF16)
            yd = []
            for h in (h0, h1):
                seg = cum2[:, h:h + 1] - cum2_t[h:h + 1, :]
                wgt = cb * jnp.exp2(jnp.where(causal, seg, -jnp.inf)) * dt_t[h:h + 1, :]
                yd.append(jnp.dot(wgt.astype(BF16), x2b, preferred_element_type=F32))
            y = jnp.where(head0, yd[0], yd[1])
            hs = h_ref[0, lo:lo + LANES, :]
            yoff = lax.dot_general(cm, hs.astype(BF16), NT_DIMS, preferred_element_type=F32)
            y = y + yoff * jnp.where(head0, exp_cum[:, h0:h0 + 1], exp_cum[:, h1:h1 + 1])
            y_ref[:, lo:lo + LANES] = y + drow_ref[:, lo:lo + LANES] * x2
            xw = x2 * jnp.where(head0, to_end[:, h0:h0 + 1], to_end[:, h1:h1 + 1])
            st = lax.dot_general(xw.astype(BF16), bm, TN_DIMS, preferred_element_type=F32)
            dec = jnp.where(top, chunk_decay[:, h0:h0 + 1], chunk_decay[:, h1:h1 + 1])
            h_ref[0, lo:lo + LANES, :] = dec * hs + st

    gw = SSD_WIDTH // SSD_GROUPS
    for g in range(SSD_GROUPS):
        yg = y_ref[:, g * gw:(g + 1) * gw] * _silu(z_ref[0, :, g * gw:(g + 1) * gw])
        ms = jnp.mean(yg * yg, axis=-1, keepdims=True)
        o_ref[0, :, g * gw:(g + 1) * gw] = (yg * lax.rsqrt(ms + RMS_EPS)
                                            * nw_ref[:, g * gw:(g + 1) * gw]).astype(o_ref.dtype)


def _ssd(xbc, z, dt_raw, conv_state, h0, conv_w, conv_b, dt_bias, a_log, d_skip, norm_w, q):
    b, l, _ = xbc.shape
    assert l % q == 0
    pad = LANES - SSD_HEADS
    dtb = jnp.pad(dt_bias, (0, pad)).reshape(1, LANES)
    alog = jnp.pad(a_log, (0, pad)).reshape(1, LANES)
    drow = jnp.repeat(d_skip, HEAD_DIM).reshape(1, SSD_WIDTH)

    def tok(wd):
        return pl.BlockSpec((1, q, wd), lambda bi, ci: (bi, ci, 0))

    def per_seq(shape):
        return pl.BlockSpec((1,) + shape, lambda bi, ci: (bi, 0, 0))

    def const(shape):
        return pl.BlockSpec(shape, lambda bi, ci: (0, 0))

    o, h = pl.pallas_call(
        _ssd_kernel,
        grid=(b, l // q),
        in_specs=[tok(SSD_CONV_DIM), tok(SSD_WIDTH), tok(LANES),
                  per_seq((SSD_CONV - 1, SSD_CONV_DIM)), per_seq((SSD_WIDTH, SSD_STATE)),
                  const((SSD_CONV, SSD_CONV_DIM)), const((1, SSD_CONV_DIM)), const((1, LANES)),
                  const((1, LANES)), const((1, SSD_WIDTH)), const((1, SSD_WIDTH))],
        out_specs=[tok(SSD_WIDTH), per_seq((SSD_WIDTH, SSD_STATE))],
        out_shape=[jax.ShapeDtypeStruct((b, l, SSD_WIDTH), BF16),
                   jax.ShapeDtypeStruct((b, SSD_WIDTH, SSD_STATE), F32)],
        scratch_shapes=[pltpu.VMEM((CONV_PAD + q, SSD_CONV_DIM), F32),
                        pltpu.VMEM((q, SSD_CONV_DIM), F32),
                        pltpu.VMEM((q, SSD_WIDTH), F32)],
        compiler_params=_params("parallel", "arbitrary"),
        name=f"ssd_q{q}",
    )(xbc, z, dt_raw, conv_state, h0, conv_w, conv_b.reshape(1, SSD_CONV_DIM), dtb, alog, drow,
      norm_w.reshape(1, SSD_WIDTH))
    return o, h


BAND_BLOCKS = CA_LEFT // LANES + 1
BAND_PROMPT_PAIRS = 4


def _band_attend(jobs):
    scores = [jnp.dot(_pair_queries(q2), kt, preferred_element_type=F32) + bias2
              for q2, kt, _, bias2, _ in jobs]
    probs = [jnp.exp(s - jnp.max(s, axis=-1, keepdims=True)) for s in scores]
    dens = [jnp.sum(e, axis=-1, keepdims=True) for e in probs]
    nums = [lax.dot_general(e.astype(BF16), job[2], NT_DIMS, preferred_element_type=F32)
            for e, job in zip(probs, jobs)]
    return [_pair_outputs(num / den) * _silu(job[4]) for num, den, job in zip(nums, dens, jobs)]


def _band_prompt_kernel(q_ref, k_ref, v_ref, g_ref, bias_ref, o_ref):
    step = pl.program_id(2)
    n_sub = q_ref.shape[1] // LANES
    n_pair = q_ref.shape[2] // LANES
    band = BAND_BLOCKS * LANES

    def tile(n):
        return slice(n * LANES, (n + 1) * LANES)

    def sub_tile(p, s, i):
        if isinstance(i, int):
            keys = min(i + 1, BAND_BLOCKS) * LANES
            cols = pl.ds((i + 1) * LANES - keys, keys)
        else:
            keys = band
            cols = pl.ds(pl.multiple_of((i + 1) * LANES - band, LANES), band)
        bias2 = jnp.concatenate([bias_ref[2 * p, :, band - keys:], bias_ref[2 * p + 1, :, band - keys:]],
                                axis=0)
        return (q_ref[0, tile(s), tile(p)], k_ref[0, tile(p), cols], v_ref[0, tile(p), cols], bias2,
                g_ref[0, tile(s), tile(p)])

    def step_tiles(first):
        for p in range(n_pair):
            outs = _band_attend([sub_tile(p, s, first + s) for s in range(n_sub)])
            for s, o in enumerate(outs):
                o_ref[0, tile(s), tile(p)] = o.astype(o_ref.dtype)

    pl.when(step == 0)(lambda: step_tiles(0))
    pl.when(step > 0)(lambda: step_tiles(step * n_sub))


def _bias_lookup(rel_bias, t_minus_s):
    rel = jnp.clip(t_minus_s, -CA_REL_CLIP, CA_REL_CLIP) + CA_REL_CLIP
    onehot = (rel[..., None] == jnp.arange(2 * CA_REL_CLIP + 1)).astype(F32)
    return jnp.einsum("rck,hk->hrc", onehot, rel_bias, precision=lax.Precision.HIGHEST)


def _band_bias(rel_bias):
    r = jnp.arange(LANES)[:, None]
    u = jnp.arange(LANES)[None, :]

    def tile(dist):
        return _bias_lookup(rel_bias, dist * LANES + r - u)

    t0 = jnp.where((r < CHUNK) & (u >= CHUNK), NEG_BIG, tile(0))
    t1 = tile(1)
    t2 = jnp.broadcast_to(rel_bias[:, -1][:, None, None], t1.shape)
    t4 = jnp.where((r >= CHUNK) & (u < CHUNK), NEG_BIG, t2)
    return jnp.concatenate([t4, t2, t2, t1, t0], axis=-1).astype(F32)


def _band_prompt(q, kt, vt, g, band_bias):
    b, l, wdt = q.shape
    n_sub = math.gcd(l // LANES, BAND_BLOCKS - 1)
    assert n_sub == BAND_BLOCKS - 1 or l // LANES == n_sub
    tq = n_sub * LANES
    step_lanes = BAND_PROMPT_PAIRS * LANES
    assert wdt % step_lanes == 0
    blk = pl.BlockSpec((1, tq, step_lanes), lambda bi, hp, i: (bi, i, hp))
    full = pl.BlockSpec((1, step_lanes, l), lambda bi, hp, i: (bi, hp, 0))
    bias = pl.BlockSpec((2 * BAND_PROMPT_PAIRS, LANES, BAND_BLOCKS * LANES), lambda bi, hp, i: (hp, 0, 0))
    return pl.pallas_call(
        _band_prompt_kernel,
        grid=(b, wdt // step_lanes, l // tq),
        in_specs=[blk, full, full, blk, bias],
        out_specs=blk,
        out_shape=jax.ShapeDtypeStruct((b, l, wdt), BF16),
        compiler_params=_params("parallel", "parallel", "arbitrary"),
        name="band_prompt",
    )(q, kt, vt, g, band_bias)


BAND_SAMPLE_PAIRS = 4


def _band_sample_kernel(q_ref, kn_ref, vn_ref, kc_ref, vc_ref, g_ref, bias_ref, o_ref, ok_ref, ov_ref):
    seq = pl.program_id(0)
    t = q_ref.shape[1]
    p_len = kc_ref.shape[2]
    lane = lax.broadcasted_iota(jnp.int32, (kc_ref.shape[1], LANES), 1)
    k_new = _own_columns(kn_ref[...], seq, t)
    v_new = _own_columns(vn_ref[...], seq, t)
    for cache_ref, new, out_ref in ((kc_ref, k_new, ok_ref), (vc_ref, v_new, ov_ref)):
        shifted = pltpu.roll(cache_ref[0], p_len - t, axis=1)
        moved = pltpu.roll(new, LANES - t, axis=1)
        out_ref[0, :, :p_len - LANES] = shifted[:, :p_len - LANES]
        out_ref[0, :, p_len - LANES:] = jnp.where(lane < LANES - t, shifted[:, p_len - LANES:], moved)

    def lanes(p):
        return slice(p * LANES, (p + 1) * LANES)

    def pair(p):
        kt = jnp.concatenate([kc_ref[0, lanes(p), :], k_new[lanes(p)]], axis=1).astype(BF16)
        vt = jnp.concatenate([vc_ref[0, lanes(p), :], v_new[lanes(p)]], axis=1).astype(BF16)
        bias2 = jnp.concatenate([bias_ref[2 * p], bias_ref[2 * p + 1]], axis=0)
        return q_ref[0, :, lanes(p)], kt, vt, bias2, g_ref[0, :, lanes(p)]

    for p, o in enumerate(_band_attend([pair(p) for p in range(q_ref.shape[2] // LANES)])):
        o_ref[0, :, lanes(p)] = o.astype(o_ref.dtype)


def _band_sample_bias(rel_bias, tq):
    qi = jnp.arange(tq)[:, None]
    kj = jnp.arange(CA_LEFT + LANES)[None, :]
    return jnp.where(kj < CA_LEFT + tq, _bias_lookup(rel_bias, qi - kj + CA_LEFT), NEG_BIG).astype(F32)


def _band_sample(q, knt, vnt, kct, vct, g, bias):
    b, t, wdt = q.shape
    p = kct.shape[2]
    step_lanes = BAND_SAMPLE_PAIRS * LANES
    assert p == CA_LEFT and t == CHUNK and wdt % step_lanes == 0 and knt.shape == (wdt, b * t)
    tok = pl.BlockSpec((1, t, step_lanes), lambda bi, hp: (bi, 0, hp))
    new = pl.BlockSpec((step_lanes, LANES), lambda bi, hp: (hp, bi * t // LANES))
    old = pl.BlockSpec((1, step_lanes, p), lambda bi, hp: (bi, hp, 0))
    bspec = pl.BlockSpec((2 * BAND_SAMPLE_PAIRS, t, CA_LEFT + LANES), lambda bi, hp: (hp, 0, 0))
    return pl.pallas_call(
        _band_sample_kernel,
        grid=(b, wdt // step_lanes),
        in_specs=[tok, new, new, old, old, tok, bspec],
        out_specs=[tok, old, old],
        out_shape=[jax.ShapeDtypeStruct((b, t, wdt), BF16), jax.ShapeDtypeStruct(kct.shape, F32),
                   jax.ShapeDtypeStruct(vct.shape, F32)],
        compiler_params=_params("parallel", "parallel"),
        name="band_sample",
    )(q, knt, vnt, kct, vct, g, bias)


def _feature_major(cache):
    b, p, h, d = cache.shape
    return jnp.transpose(cache, (0, 2, 3, 1)).reshape(b, h * d, p)


def _token_major(kt, heads):
    b, _, t = kt.shape
    return jnp.transpose(kt.reshape(b, heads, HEAD_DIM, t), (0, 3, 1, 2))


def _new_keys(w_t, xb, row0, nrows, has_cache, keep):
    b, l, d = xb.shape
    if not has_cache:
        return _matmul_t(w_t, xb, row0, nrows, keep)
    _, kt = _matmul_t(w_t, xb.reshape(1, b * l, d), row0, nrows, b * l)
    return None, kt[0]


def _sample_token_major(kt, b, heads):
    t = kt.shape[1] // b
    return jnp.transpose(kt.reshape(heads, HEAD_DIM, b, t), (2, 3, 0, 1))


def _even_layer(x, past_k, past_v, conv_state, h0, wts, ssd_chunk):
    w_in_t, w_dt_t, conv_w, conv_b, dt_bias, a_log, d_skip, norm_w, w_out, ln_g, ln_b = wts
    b, l, _ = x.shape
    m = b * l
    heads = SB_WIDTH // HEAD_DIM
    x2 = x.reshape(m, D_MODEL)
    qb, xb2 = _matmul(x2, w_in_t, L0_Q, SB_WIDTH, (BF16,), w_is_t=True, copy_x=True)
    xb = xb2.reshape(b, l, D_MODEL)
    kt_b, kt = _new_keys(w_in_t, xb, L0_K, SB_WIDTH, past_k is not None, l)
    vt_b, vt = _new_keys(w_in_t, xb, L0_V, SB_WIDTH, past_k is not None, l)
    ga, dt_raw = _matmul(xb2, w_in_t, L0_G, SB_WIDTH, (F32,), w_side=w_dt_t, w_is_t=True)
    (z,) = _matmul(xb2, w_in_t, L0_Z, SSD_WIDTH, (F32,), w_is_t=True)
    (xbc,) = _matmul(xb2, w_in_t, L0_XBC, SSD_CONV_DIM, (F32,), w_is_t=True)

    def seq(a):
        return a.reshape(b, l, a.shape[-1])

    if past_k is None:
        o_a = _sb_prompt(seq(qb), kt_b, vt_b, seq(ga))
        k_new, v_new = _token_major(kt, heads), _token_major(vt, heads)
    else:
        o_a = _sb_sample(seq(qb), kt, vt, seq(ga), _feature_major(past_k), _feature_major(past_v))
        k_new, v_new = _sample_token_major(kt, b, heads), _sample_token_major(vt, b, heads)
    xbc3 = seq(xbc)
    o_b, h_new = _ssd(xbc3, seq(z), seq(dt_raw), conv_state, h0, conv_w, conv_b, dt_bias, a_log,
                      d_skip, norm_w, ssd_chunk)
    y, yb = _proj_ln([o_a.reshape(m, SB_WIDTH), o_b.reshape(m, SSD_WIDTH)], x2, w_out, ln_g, ln_b)
    tail = jnp.concatenate([conv_state, xbc3], axis=1)[:, -(SSD_CONV - 1):]
    return seq(y), seq(yb), k_new, v_new, tail, h_new.reshape(b, SSD_HEADS, HEAD_DIM, SSD_STATE)


def _odd_layer(x, xb, past_k, past_v, wts):
    w_in, w_kv_t, bias, w_out, ln_g, ln_b = wts
    b, l, _ = x.shape
    m = b * l
    x2 = x.reshape(m, D_MODEL)
    xb2 = xb.reshape(m, D_MODEL)
    keep = min(CA_LEFT, l)
    (qb,) = _matmul(xb2, w_in, 0, CA_WIDTH, (BF16,))
    kt_b, kt = _new_keys(w_kv_t, xb, 0, CA_WIDTH, past_k is not None, keep)
    vt_b, vt = _new_keys(w_kv_t, xb, CA_WIDTH, CA_WIDTH, past_k is not None, keep)
    (g,) = _matmul(xb2, w_in, 3 * CA_WIDTH, CA_WIDTH, (F32,))

    def seq(a):
        return a.reshape(b, l, a.shape[-1])

    if past_k is None:
        o = _band_prompt(seq(qb), kt_b, vt_b, seq(g), bias)
    else:
        o, kt, vt = _band_sample(seq(qb), kt, vt, _feature_major(past_k), _feature_major(past_v), seq(g),
                                 bias)
    y, _ = _proj_ln([o.reshape(m, CA_WIDTH)], x2, w_out, ln_g, ln_b)
    heads = CA_WIDTH // HEAD_DIM
    return seq(y), _token_major(kt, heads), _token_major(vt, heads)


def kernel(x_prompt, x_sample, cache_sb_k, cache_sb_v, state_ssm, state_conv, cache_band_k,
           cache_band_v, even_w_in, even_conv_w, even_conv_b, even_dt_bias, even_a_log, even_d_skip,
           even_norm_w, even_w_out, even_ln_g, even_ln_b, odd_w_in, odd_rel_bias, odd_w_out,
           odd_ln_g, odd_ln_b):
    bp, lp, _ = x_prompt.shape
    bs, ls, _ = x_sample.shape
    w_in_t0 = jnp.transpose(even_w_in[0])
    w_dt_t0 = jnp.pad(w_in_t0[L0_DT:].astype(BF16), ((0, LANES - SSD_HEADS), (0, 0)))
    wts0 = (w_in_t0, w_dt_t0, even_conv_w[0], even_conv_b[0], even_dt_bias[0], even_a_log[0],
            even_d_skip[0], even_norm_w[0], even_w_out[0].astype(BF16), even_ln_g[0], even_ln_b[0])
    zero_conv = jnp.zeros((bp, SSD_CONV - 1, SSD_CONV_DIM), F32)
    zero_h = jnp.zeros((bp, SSD_WIDTH, SSD_STATE), F32)
    yp, ypb, p_sb_k, p_sb_v, p_conv, p_ssm = _even_layer(
        x_prompt, None, None, zero_conv, zero_h, wts0, 2 * CHUNK)
    ys, ysb, s_sb_k, s_sb_v, s_conv, s_ssm = _even_layer(
        x_sample, cache_sb_k[0], cache_sb_v[0], state_conv[0],
        state_ssm[0].reshape(bs, SSD_WIDTH, SSD_STATE), wts0, CHUNK)
    w_in1 = odd_w_in[0]
    w_kv_t1 = jnp.transpose(odd_w_in[0][:, CA_WIDTH:3 * CA_WIDTH]).astype(BF16)
    w_out1 = odd_w_out[0].astype(BF16)
    wts1p = (w_in1, w_kv_t1, _band_bias(odd_rel_bias[0]), w_out1, odd_ln_g[0], odd_ln_b[0])
    wts1s = (w_in1, w_kv_t1, _band_sample_bias(odd_rel_bias[0], ls), w_out1, odd_ln_g[0], odd_ln_b[0])
    yp, p_band_k, p_band_v = _odd_layer(yp, ypb, None, None, wts1p)
    ys, s_band_k, s_band_v = _odd_layer(ys, ysb, cache_band_k[0], cache_band_v[0], wts1s)

    def lead(a):
        return a[None]

    return (yp, ys, lead(p_sb_k), lead(p_sb_v), lead(p_ssm), lead(p_conv), lead(p_band_k),
            lead(p_band_v), lead(s_sb_k), lead(s_sb_v), lead(s_ssm), lead(s_conv), lead(s_band_k),
            lead(s_band_v))
```

```python
import functools
import math

import jax
import jax.numpy as jnp
from jax import lax
from jax.experimental import pallas as pl
from jax.experimental.pallas import tpu as pltpu

F32 = jnp.float32
BF16 = jnp.bfloat16

D_MODEL = 2048
CHUNK = 64
HEAD_DIM = 64
LANES = 128
SB_WIDTH = 1024
SSD_WIDTH = 2048
SSD_HEADS = 32
SSD_GROUPS = 4
SSD_STATE = 128
SSD_CONV = 4
SSD_CONV_DIM = SSD_WIDTH + 2 * SSD_GROUPS * SSD_STATE
CA_WIDTH = 2048
CA_LEFT = 8 * CHUNK
CA_REL_CLIP = 128
DEPTH = 2
DEEPNORM_ALPHA = (2 * DEPTH) ** 0.25
LN_EPS = 1e-5
RMS_EPS = 1e-5
ATTN_SCALE = HEAD_DIM ** -0.5
NEG_BIG = -1e30
L0_Q, L0_K, L0_V, L0_G, L0_Z, L0_XBC, L0_DT = 0, 1024, 2048, 3072, 4096, 6144, 9216
VMEM_LIMIT = 56 * 1024 * 1024
ROW_TILE = 512
COL_TILE = 1024

NT_DIMS = (((1,), (1,)), ((), ()))
TN_DIMS = (((0,), (0,)), ((), ()))


def _params(*sem):
    return pltpu.CompilerParams(dimension_semantics=sem, vmem_limit_bytes=VMEM_LIMIT)


def _silu(x):
    return x * (1.0 / (1.0 + jnp.exp(-x)))


def _split3(x):
    p0 = x.astype(BF16)
    r = x - p0.astype(F32)
    p1 = r.astype(BF16)
    p2 = (r - p1.astype(F32)).astype(BF16)
    return p0, p1, p2


def _mm_kernel(x_ref, w_ref, *refs, has_side, w_is_t, copy_x):
    *o_refs, wb_ref = refs

    @pl.when(pl.program_id(1) == 0)
    def _():
        wb_ref[...] = w_ref[...].astype(BF16)

    def product(w):
        if w_is_t:
            return lax.dot_general(x, w, NT_DIMS, preferred_element_type=F32)
        return jnp.dot(x, w, preferred_element_type=F32)

    x = x_ref[...].astype(BF16)
    if copy_x:
        o_refs.pop()[...] = x
    if has_side:
        ws_ref = o_refs.pop(0)
        os_ref = o_refs.pop()
        os_ref[...] = product(ws_ref[...])
    acc = product(wb_ref[...])
    for o_ref in o_refs:
        o_ref[...] = acc.astype(o_ref.dtype)


def _matmul(x, w, col0, ncols, out_dtypes, w_side=None, w_is_t=False, copy_x=False):
    m, k = x.shape
    tm, tn = math.gcd(m, 2 * ROW_TILE), COL_TILE
    assert ncols % tn == 0 and col0 % tn == 0
    cb0 = col0 // tn
    if w_is_t:
        w_spec = pl.BlockSpec((tn, k), lambda j, i: (cb0 + j, 0))
    else:
        w_spec = pl.BlockSpec((k, tn), lambda j, i: (0, cb0 + j))
    in_specs = [pl.BlockSpec((tm, k), lambda j, i: (i, 0)), w_spec]
    out_specs = [pl.BlockSpec((tm, tn), lambda j, i: (i, j)) for _ in out_dtypes]
    out_shape = [jax.ShapeDtypeStruct((m, ncols), dt) for dt in out_dtypes]
    args = [x, w]
    if w_side is not None:
        assert ncols == tn
        ns = w_side.shape[0 if w_is_t else 1]
        in_specs.append(pl.BlockSpec(w_side.shape, lambda j, i: (0, 0)))
        out_specs.append(pl.BlockSpec((tm, ns), lambda j, i: (i, 0)))
        out_shape.append(jax.ShapeDtypeStruct((m, ns), F32))
        args.append(w_side)
    if copy_x:
        assert ncols == tn
        out_specs.append(pl.BlockSpec((tm, k), lambda j, i: (i, 0)))
        out_shape.append(jax.ShapeDtypeStruct((m, k), BF16))
    return pl.pallas_call(
        functools.partial(_mm_kernel, has_side=w_side is not None, w_is_t=w_is_t, copy_x=copy_x),
        grid=(ncols // tn, m // tm),
        in_specs=in_specs,
        out_specs=out_specs,
        out_shape=out_shape,
        scratch_shapes=[pltpu.VMEM(w_spec.block_shape, BF16)],
        compiler_params=_params("arbitrary", "arbitrary"),
        name=f"mm_c{col0}_n{ncols}_m{m}",
    )(*args)


def _mm_t_kernel(w_ref, x_ref, ob_ref, of_ref, *scratch, first_kept):
    if scratch:
        (wb_ref,) = scratch

        @pl.when(jnp.logical_and(pl.program_id(0) == 0, pl.program_id(1) == 0))
        def _():
            wb_ref[...] = w_ref[...].astype(BF16)

        w = wb_ref[...]
    else:
        w = w_ref[...]
    acc = lax.dot_general(w, x_ref[0], NT_DIMS, preferred_element_type=F32)
    ob_ref[0] = acc.astype(ob_ref.dtype)
    kept_cols = of_ref.shape[2]

    def keep_f32():
        of_ref[0] = acc[:, acc.shape[1] - kept_cols:]

    if first_kept == 0:
        keep_f32()
    else:
        pl.when(pl.program_id(1) >= first_kept)(keep_f32)


def _matmul_t(w_t, x, row0, nrows, keep):
    s, l, k = x.shape
    tm = math.gcd(l, 2 * ROW_TILE)
    kept_cols = min(keep, tm)
    assert row0 % nrows == 0 and keep <= l and keep % kept_cols == 0 and tm % kept_cols == 0
    first_kept = (l - keep) // tm
    return pl.pallas_call(
        functools.partial(_mm_t_kernel, first_kept=first_kept),
        grid=(s, l // tm),
        in_specs=[pl.BlockSpec((nrows, k), lambda si, i: (row0 // nrows, 0), pipeline_mode=pl.Buffered(1)),
                  pl.BlockSpec((1, tm, k), lambda si, i: (si, i, 0))],
        out_specs=[pl.BlockSpec((1, nrows, tm), lambda si, i: (si, 0, i)),
                   pl.BlockSpec((1, nrows, kept_cols),
                                lambda si, i: (si, 0, jnp.maximum(i - first_kept, 0)))],
        out_shape=[jax.ShapeDtypeStruct((s, nrows, l), BF16),
                   jax.ShapeDtypeStruct((s, nrows, keep), F32)],
        scratch_shapes=[] if w_t.dtype == BF16 else [pltpu.VMEM((nrows, k), BF16)],
        compiler_params=_params("arbitrary", "arbitrary"),
        name=f"mm_t_r{row0}_n{nrows}_l{l}",
    )(w_t, x)


def _proj_ln_kernel(*refs, widths):
    n = len(widths)
    o_refs = refs[:n]
    x_ref, w_ref, g_ref, b_ref, *y_refs = refs[n:]
    mix = None
    row = 0
    for o_ref, wd in zip(o_refs, widths):
        part = jnp.dot(o_ref[...], w_ref[row:row + wd, :], preferred_element_type=F32)
        mix = part if mix is None else mix + part
        row += wd
    h = DEEPNORM_ALPHA * x_ref[...] + mix
    mu = jnp.mean(h, axis=-1, keepdims=True)
    hc = h - mu
    var = jnp.mean(hc * hc, axis=-1, keepdims=True)
    y = hc * lax.rsqrt(var + LN_EPS) * g_ref[...] + b_ref[...]
    for y_ref in y_refs:
        y_ref[...] = y.astype(y_ref.dtype)


def _proj_ln(o_parts, x, w_out, ln_g, ln_b, out_dtypes):
    m = x.shape[0]
    widths = tuple(o.shape[1] for o in o_parts)
    ktot = sum(widths)
    tm = math.gcd(m, ROW_TILE)
    assert w_out.shape == (ktot, D_MODEL)
    in_specs = [pl.BlockSpec((tm, wd), lambda i: (i, 0)) for wd in widths]
    in_specs += [pl.BlockSpec((tm, D_MODEL), lambda i: (i, 0)),
                 pl.BlockSpec((ktot, D_MODEL), lambda i: (0, 0), pipeline_mode=pl.Buffered(1)),
                 pl.BlockSpec((1, D_MODEL), lambda i: (0, 0)),
                 pl.BlockSpec((1, D_MODEL), lambda i: (0, 0))]
    return pl.pallas_call(
        functools.partial(_proj_ln_kernel, widths=widths),
        grid=(m // tm,),
        in_specs=in_specs,
        out_specs=[pl.BlockSpec((tm, D_MODEL), lambda i: (i, 0)) for _ in out_dtypes],
        out_shape=[jax.ShapeDtypeStruct((m, D_MODEL), dt) for dt in out_dtypes],
        compiler_params=_params("parallel"),
        name=f"proj_ln_k{ktot}_m{m}",
    )(*o_parts, x, w_out, ln_g.reshape(1, D_MODEL), ln_b.reshape(1, D_MODEL))


def _head_masks(rows):
    lane = lax.broadcasted_iota(jnp.int32, (rows, LANES), 1)
    return lane < HEAD_DIM


SB_KEYS = 256
SB_STEP_LANES = 4 * LANES
SB_DEAD = 104.5
LOG2E = 1.4426950408889634


def _pair_queries(q2):
    head0 = _head_masks(q2.shape[0])
    qs = q2 * jnp.asarray(ATTN_SCALE, q2.dtype)
    zero = jnp.zeros_like(qs)
    return jnp.concatenate([jnp.where(head0, qs, zero), jnp.where(head0, zero, qs)], axis=0)


def _pair_outputs(stacked):
    tq = stacked.shape[0] // 2
    return jnp.where(_head_masks(tq), stacked[:tq], stacked[tq:])


def _sb_visit(qh, k_pairs, v_pairs, carries, accs, upper, diag_mask):
    zs = [jnp.dot(q2, kt, preferred_element_type=F32) for q2, kt in zip(qh, k_pairs)]
    sps = [jnp.maximum(z, 0.0) + jnp.log(1.0 + jnp.exp(-jnp.abs(z))) for z in zs]
    if diag_mask is not None:
        sps = [jnp.where(diag_mask, sp, 0.0) for sp in sps]
    masses = [jnp.dot(sp.astype(BF16), upper, preferred_element_type=F32) + c
              for sp, c in zip(sps, carries)]
    ws = [jnp.exp(z - mass) for z, mass in zip(zs, masses)]
    if diag_mask is not None:
        ws = [jnp.where(diag_mask, w, 0.0) for w in ws]
    accs = tuple(a + lax.dot_general(w.astype(BF16), vt, NT_DIMS, preferred_element_type=F32)
                 for a, w, vt in zip(accs, ws, v_pairs))
    return tuple(mass[:, 0:1] for mass in masses), accs


def _sb_upper(n):
    r = lax.broadcasted_iota(jnp.int32, (n, n), 0)
    c = lax.broadcasted_iota(jnp.int32, (n, n), 1)
    return (r >= c).astype(BF16)


def _sb_alive(carries):
    return jnp.min(functools.reduce(jnp.minimum, carries)) < SB_DEAD


def _sb_queries(q_ref):
    return [_pair_queries(q_ref[0, :, p * LANES:(p + 1) * LANES]) for p in range(q_ref.shape[2] // LANES)]


def _sb_strict(tq, keys):
    r = lax.broadcasted_iota(jnp.int32, (2 * tq, keys), 0)
    c = lax.broadcasted_iota(jnp.int32, (2 * tq, keys), 1)
    return c < jnp.where(r >= tq, r - tq, r)


def _sb_finish(accs, g_ref, o_ref):
    for p, acc in enumerate(accs):
        lanes = slice(p * LANES, (p + 1) * LANES)
        o_ref[0, :, lanes] = (_pair_outputs(acc) * _silu(g_ref[0, :, lanes])).astype(o_ref.dtype)


def _pair_slabs(ref, lead, n_pair, cols):
    return [ref[lead, p * LANES:(p + 1) * LANES, cols].astype(BF16) for p in range(n_pair)]


def _sb_prompt_kernel(q_ref, k_ref, v_ref, g_ref, o_ref):
    qi = pl.program_id(2)
    tq = q_ref.shape[1]
    assert tq == SB_KEYS
    qh = _sb_queries(q_ref)
    n_pair = len(qh)
    upper = _sb_upper(SB_KEYS)
    strict = _sb_strict(tq, SB_KEYS)

    def visit(kb, carries, accs, mask):
        cols = pl.ds(pl.multiple_of(kb * SB_KEYS, SB_KEYS), SB_KEYS)
        return _sb_visit(qh, _pair_slabs(k_ref, 0, n_pair, cols), _pair_slabs(v_ref, 0, n_pair, cols),
                         carries, accs, upper, mask)

    zc = (jnp.zeros((2 * tq, 1), F32),) * n_pair
    za = (jnp.zeros((2 * tq, LANES), F32),) * n_pair
    carries, accs = visit(qi, zc, za, strict)

    def older(state):
        kb, _, carries, accs = state
        carries, accs = visit(kb, carries, accs, None)
        return kb - 1, _sb_alive(carries), carries, accs

    state = lax.while_loop(lambda s: jnp.logical_and(s[0] >= 0, s[1]), older,
                           (qi - 1, _sb_alive(carries), carries, accs))
    _sb_finish(state[3], g_ref, o_ref)


def _sb_prompt(q, kt, vt, g):
    b, l, wdt = q.shape
    tq = SB_KEYS
    assert l % tq == 0 and wdt % SB_STEP_LANES == 0
    blk = pl.BlockSpec((1, tq, SB_STEP_LANES), lambda bi, hp, qi: (bi, qi, hp))
    full = pl.BlockSpec((1, SB_STEP_LANES, l), lambda bi, hp, qi: (bi, hp, 0))
    return pl.pallas_call(
        _sb_prompt_kernel,
        grid=(b, wdt // SB_STEP_LANES, l // tq),
        in_specs=[blk, full, full, blk],
        out_specs=blk,
        out_shape=jax.ShapeDtypeStruct((b, l, wdt), BF16),
        compiler_params=_params("parallel", "parallel", "arbitrary"),
        name="sb_prompt",
    )(q, kt, vt, g)


def _sb_sample_kernel(q_ref, kn_ref, vn_ref, g_ref, kc_hbm, vc_hbm, o_ref, kbuf, vbuf, sem, *, n_cached):
    seq = pl.program_id(0)
    tq = q_ref.shape[1]
    qh = _sb_queries(q_ref)
    n_pair = len(qh)
    upper = _sb_upper(SB_KEYS)
    strict = _sb_strict(tq, LANES)
    zc = (jnp.zeros((2 * tq, 1), F32),) * n_pair
    za = (jnp.zeros((2 * tq, LANES), F32),) * n_pair

    def block_copies(kb, slot):
        cols = pl.ds(pl.multiple_of(kb * SB_KEYS, SB_KEYS), SB_KEYS)
        return (pltpu.make_async_copy(kc_hbm.at[seq, :, cols], kbuf.at[slot], sem.at[0, slot]),
                pltpu.make_async_copy(vc_hbm.at[seq, :, cols], vbuf.at[slot], sem.at[1, slot]))

    def start_block(kb):
        for cp in block_copies(kb, kb % 2):
            cp.start()

    def wait_block(kb):
        for cp in block_copies(kb, kb % 2):
            cp.wait()

    start_block(n_cached - 1)
    everything = slice(None)

    def new_pairs(ref):
        return [_own_columns(ref[p * LANES:(p + 1) * LANES, :], seq, tq).astype(BF16) for p in range(n_pair)]

    carries, accs = _sb_visit(qh, new_pairs(kn_ref), new_pairs(vn_ref), zc, za, _sb_upper(LANES), strict)

    def older(state):
        kb, _, carries, accs = state
        wait_block(kb)
        pl.when(kb > 0)(lambda: start_block(kb - 1))
        slot = kb % 2
        carries, accs = _sb_visit(qh, _pair_slabs(kbuf, slot, n_pair, everything),
                                  _pair_slabs(vbuf, slot, n_pair, everything), carries, accs, upper, None)
        return kb - 1, _sb_alive(carries), carries, accs

    state = lax.while_loop(lambda s: jnp.logical_and(s[0] >= 0, s[1]), older,
                           (jnp.int32(n_cached - 1), _sb_alive(carries), carries, accs))
    pl.when(state[0] >= 0)(lambda: wait_block(state[0]))
    _sb_finish(state[3], g_ref, o_ref)


def _own_columns(block, seq, t):
    assert 2 * t == LANES
    return jnp.where(seq % 2 == 1, pltpu.roll(block, t, axis=1), block)


def _sb_sample(q, knt, vnt, g, kct, vct):
    b, t, wdt = q.shape
    p = kct.shape[2]
    assert p % SB_KEYS == 0 and knt.shape == (wdt, b * t)
    tok = pl.BlockSpec((1, t, wdt), lambda bi: (bi, 0, 0))
    new = pl.BlockSpec((wdt, LANES), lambda bi: (0, bi * t // LANES))
    hbm = pl.BlockSpec(memory_space=pl.ANY)
    return pl.pallas_call(
        functools.partial(_sb_sample_kernel, n_cached=p // SB_KEYS),
        grid=(b,),
        in_specs=[tok, new, new, tok, hbm, hbm],
        out_specs=tok,
        out_shape=jax.ShapeDtypeStruct((b, t, wdt), BF16),
        scratch_shapes=[pltpu.VMEM((2, wdt, SB_KEYS), F32),
                        pltpu.VMEM((2, wdt, SB_KEYS), F32),
                        pltpu.SemaphoreType.DMA((2, 2))],
        compiler_params=_params("arbitrary"),
        name="sb_sample",
    )(q, knt, vnt, g, kct, vct)


CONV_PAD = 8
CONV_SLAB = 512


def _ssd_kernel(xbc_ref, z_ref, dt_ref, cs_ref, h0_ref, cw_ref, cb_ref, dtb_ref, alog_ref, drow_ref,
                nw_ref, o_ref, h_ref, xp_ref, xa_ref, y_ref):
    ci = pl.program_id(1)
    q = xbc_ref.shape[1]
    taps = SSD_CONV - 1

    @pl.when(ci == 0)
    def _():
        xp_ref[0:CONV_PAD, :] = jnp.zeros((CONV_PAD, SSD_CONV_DIM), F32)
        xp_ref[CONV_PAD - taps:CONV_PAD, :] = cs_ref[0]
        h_ref[0] = h0_ref[0]

    @pl.when(ci > 0)
    def _():
        xp_ref[0:CONV_PAD, :] = xp_ref[q:CONV_PAD + q, :]

    xp_ref[CONV_PAD:CONV_PAD + q, :] = xbc_ref[0]
    for cs in range(0, SSD_CONV_DIM, CONV_SLAB):
        xe = xp_ref[:, cs:cs + CONV_SLAB]
        acc = cw_ref[0:1, cs:cs + CONV_SLAB] * xe
        for t in range(1, SSD_CONV):
            acc = cw_ref[t:t + 1, cs:cs + CONV_SLAB] * xe + pltpu.roll(acc, 1, axis=0)
        xa_ref[:, cs:cs + CONV_SLAB] = _silu(acc[CONV_PAD:] + cb_ref[:, cs:cs + CONV_SLAB])

    dtv = dt_ref[0] + dtb_ref[...]
    dtv = jnp.maximum(dtv, 0.0) + jnp.log1p(jnp.exp(-jnp.abs(dtv)))
    da = dtv * (-jnp.exp(alog_ref[...]))
    r = lax.broadcasted_iota(jnp.int32, (q, q), 0)
    c = lax.broadcasted_iota(jnp.int32, (q, q), 1)
    causal = c <= r
    tril = causal.astype(BF16)
    cum = sum(jnp.dot(tril, p, preferred_element_type=F32) for p in _split3(da))
    er = lax.broadcasted_iota(jnp.int32, (LANES, LANES), 0)
    ec = lax.broadcasted_iota(jnp.int32, (LANES, LANES), 1)
    eye = (er == ec).astype(BF16)

    def transpose(x):
        return sum(lax.dot_general(eye, p, NT_DIMS, preferred_element_type=F32) for p in _split3(x))

    cum2 = cum * LOG2E
    cum2_t = transpose(cum) * LOG2E
    dt_t = transpose(dtv)
    cum_last = cum[q - 1:q, :]
    to_end = jnp.exp(cum_last - cum) * dtv
    exp_cum = jnp.exp(cum)
    chunk_decay = jnp.broadcast_to(jnp.exp(cum_last), (LANES, LANES))
    head0 = _head_masks(q)
    top = lax.broadcasted_iota(jnp.int32, (LANES, LANES), 0) < HEAD_DIM

    for g in range(SSD_GROUPS):
        b0 = SSD_WIDTH + g * SSD_STATE
        c0 = SSD_WIDTH + SSD_GROUPS * SSD_STATE + g * SSD_STATE
        bm = xa_ref[:, b0:b0 + SSD_STATE].astype(BF16)
        cm = xa_ref[:, c0:c0 + SSD_STATE].astype(BF16)
        cb = lax.dot_general(cm, bm, NT_DIMS, preferred_element_type=F32)
        for j in range(SSD_HEADS // SSD_GROUPS // 2):
            pr = g * (SSD_HEADS // SSD_GROUPS // 2) + j
            h0, h1 = 2 * pr, 2 * pr + 1
            lo = pr * LANES
            x2 = xa_ref[:, lo:lo + LANES]
            x2b = x2.astype(BF16)
            yd = []
            for h in (h0, h1):
                seg = cum2[:, h:h + 1] - cum2_t[h:h + 1, :]
                wgt = cb * jnp.exp2(jnp.where(causal, seg, -jnp.inf)) * dt_t[h:h + 1, :]
                yd.append(jnp.dot(wgt.astype(BF16), x2b, preferred_element_type=F32))
            y = jnp.where(head0, yd[0], yd[1])
            hs = h_ref[0, lo:lo + LANES, :]
            yoff = lax.dot_general(cm, hs.astype(BF16), NT_DIMS, preferred_element_type=F32)
            y = y + yoff * jnp.where(head0, exp_cum[:, h0:h0 + 1], exp_cum[:, h1:h1 + 1])
            y_ref[:, lo:lo + LANES] = y + drow_ref[:, lo:lo + LANES] * x2
            xw = x2 * jnp.where(head0, to_end[:, h0:h0 + 1], to_end[:, h1:h1 + 1])
            st = lax.dot_general(xw.astype(BF16), bm, TN_DIMS, preferred_element_type=F32)
            dec = jnp.where(top, chunk_decay[:, h0:h0 + 1], chunk_decay[:, h1:h1 + 1])
            h_ref[0, lo:lo + LANES, :] = dec * hs + st

    gw = SSD_WIDTH // SSD_GROUPS
    for g in range(SSD_GROUPS):
        yg = y_ref[:, g * gw:(g + 1) * gw] * _silu(z_ref[0, :, g * gw:(g + 1) * gw])
        ms = jnp.mean(yg * yg, axis=-1, keepdims=True)
        o_ref[0, :, g * gw:(g + 1) * gw] = (yg * lax.rsqrt(ms + RMS_EPS)
                                            * nw_ref[:, g * gw:(g + 1) * gw]).astype(o_ref.dtype)


def _ssd(xbc, z, dt_raw, conv_state, h0, conv_w, conv_b, dt_bias, a_log, d_skip, norm_w, q):
    b, l, _ = xbc.shape
    assert l % q == 0
    pad = LANES - SSD_HEADS
    dtb = jnp.pad(dt_bias, (0, pad)).reshape(1, LANES)
    alog = jnp.pad(a_log, (0, pad)).reshape(1, LANES)
    drow = jnp.repeat(d_skip, HEAD_DIM).reshape(1, SSD_WIDTH)

    def tok(wd):
        return pl.BlockSpec((1, q, wd), lambda bi, ci: (bi, ci, 0))

    def per_seq(shape):
        return pl.BlockSpec((1,) + shape, lambda bi, ci: (bi, 0, 0))

    def const(shape):
        return pl.BlockSpec(shape, lambda bi, ci: (0, 0))

    o, h = pl.pallas_call(
        _ssd_kernel,
        grid=(b, l // q),
        in_specs=[tok(SSD_CONV_DIM), tok(SSD_WIDTH), tok(LANES),
                  per_seq((SSD_CONV - 1, SSD_CONV_DIM)), per_seq((SSD_WIDTH, SSD_STATE)),
                  const((SSD_CONV, SSD_CONV_DIM)), const((1, SSD_CONV_DIM)), const((1, LANES)),
                  const((1, LANES)), const((1, SSD_WIDTH)), const((1, SSD_WIDTH))],
        out_specs=[tok(SSD_WIDTH), per_seq((SSD_WIDTH, SSD_STATE))],
        out_shape=[jax.ShapeDtypeStruct((b, l, SSD_WIDTH), BF16),
                   jax.ShapeDtypeStruct((b, SSD_WIDTH, SSD_STATE), F32)],
        scratch_shapes=[pltpu.VMEM((CONV_PAD + q, SSD_CONV_DIM), F32),
                        pltpu.VMEM((q, SSD_CONV_DIM), F32),
                        pltpu.VMEM((q, SSD_WIDTH), F32)],
        compiler_params=_params("parallel", "arbitrary"),
        name=f"ssd_q{q}",
    )(xbc, z, dt_raw, conv_state, h0, conv_w, conv_b.reshape(1, SSD_CONV_DIM), dtb, alog, drow,
      norm_w.reshape(1, SSD_WIDTH))
    return o, h


BAND_BLOCKS = CA_LEFT // LANES + 1
BAND_PROMPT_PAIRS = 4


def _band_attend(jobs):
    scores = [jnp.dot(_pair_queries(q2), kt, preferred_element_type=F32) + bias2
              for q2, kt, _, bias2, _ in jobs]
    probs = [jnp.exp(s - jnp.max(s, axis=-1, keepdims=True)) for s in scores]
    dens = [jnp.sum(e, axis=-1, keepdims=True) for e in probs]
    nums = [lax.dot_general(e.astype(BF16), job[2], NT_DIMS, preferred_element_type=F32)
            for e, job in zip(probs, jobs)]
    return [_pair_outputs(num / den) * _silu(job[4]) for num, den, job in zip(nums, dens, jobs)]


def _band_prompt_kernel(q_ref, k_ref, v_ref, g_ref, bias_ref, o_ref):
    step = pl.program_id(2)
    n_sub = q_ref.shape[1] // LANES
    n_pair = q_ref.shape[2] // LANES
    band = BAND_BLOCKS * LANES

    def tile(n):
        return slice(n * LANES, (n + 1) * LANES)

    def sub_tile(p, s, i):
        if isinstance(i, int):
            keys = min(i + 1, BAND_BLOCKS) * LANES
            cols = pl.ds((i + 1) * LANES - keys, keys)
        else:
            keys = band
            cols = pl.ds(pl.multiple_of((i + 1) * LANES - band, LANES), band)
        bias2 = jnp.concatenate([bias_ref[2 * p, :, band - keys:], bias_ref[2 * p + 1, :, band - keys:]],
                                axis=0)
        return (q_ref[0, tile(s), tile(p)], k_ref[0, tile(p), cols], v_ref[0, tile(p), cols], bias2,
                g_ref[0, tile(s), tile(p)])

    def step_tiles(first):
        for p in range(n_pair):
            outs = _band_attend([sub_tile(p, s, first + s) for s in range(n_sub)])
            for s, o in enumerate(outs):
                o_ref[0, tile(s), tile(p)] = o.astype(o_ref.dtype)

    pl.when(step == 0)(lambda: step_tiles(0))
    pl.when(step > 0)(lambda: step_tiles(step * n_sub))


def _bias_lookup(rel_bias, t_minus_s):
    rel = jnp.clip(t_minus_s, -CA_REL_CLIP, CA_REL_CLIP) + CA_REL_CLIP
    onehot = (rel[..., None] == jnp.arange(2 * CA_REL_CLIP + 1)).astype(F32)
    return jnp.einsum("rck,hk->hrc", onehot, rel_bias, precision=lax.Precision.HIGHEST)


def _band_bias(rel_bias):
    r = jnp.arange(LANES)[:, None]
    u = jnp.arange(LANES)[None, :]

    def tile(dist):
        return _bias_lookup(rel_bias, dist * LANES + r - u)

    t0 = jnp.where((r < CHUNK) & (u >= CHUNK), NEG_BIG, tile(0))
    t1 = tile(1)
    t2 = jnp.broadcast_to(rel_bias[:, -1][:, None, None], t1.shape)
    t4 = jnp.where((r >= CHUNK) & (u < CHUNK), NEG_BIG, t2)
    return jnp.concatenate([t4, t2, t2, t1, t0], axis=-1).astype(F32)


def _band_prompt(q, kt, vt, g, band_bias):
    b, l, wdt = q.shape
    n_sub = math.gcd(l // LANES, BAND_BLOCKS - 1)
    assert n_sub == BAND_BLOCKS - 1 or l // LANES == n_sub
    tq = n_sub * LANES
    step_lanes = BAND_PROMPT_PAIRS * LANES
    assert wdt % step_lanes == 0
    blk = pl.BlockSpec((1, tq, step_lanes), lambda bi, hp, i: (bi, i, hp))
    full = pl.BlockSpec((1, step_lanes, l), lambda bi, hp, i: (bi, hp, 0))
    bias = pl.BlockSpec((2 * BAND_PROMPT_PAIRS, LANES, BAND_BLOCKS * LANES), lambda bi, hp, i: (hp, 0, 0))
    return pl.pallas_call(
        _band_prompt_kernel,
        grid=(b, wdt // step_lanes, l // tq),
        in_specs=[blk, full, full, blk, bias],
        out_specs=blk,
        out_shape=jax.ShapeDtypeStruct((b, l, wdt), BF16),
        compiler_params=_params("parallel", "parallel", "arbitrary"),
        name="band_prompt",
    )(q, kt, vt, g, band_bias)


BAND_SAMPLE_PAIRS = 4


def _band_sample_kernel(q_ref, kn_ref, vn_ref, kc_ref, vc_ref, g_ref, bias_ref, o_ref, ok_ref, ov_ref):
    seq = pl.program_id(0)
    t = q_ref.shape[1]
    p_len = kc_ref.shape[2]
    lane = lax.broadcasted_iota(jnp.int32, (kc_ref.shape[1], LANES), 1)
    k_new = _own_columns(kn_ref[...], seq, t)
    v_new = _own_columns(vn_ref[...], seq, t)
    for cache_ref, new, out_ref in ((kc_ref, k_new, ok_ref), (vc_ref, v_new, ov_ref)):
        shifted = pltpu.roll(cache_ref[0], p_len - t, axis=1)
        moved = pltpu.roll(new, LANES - t, axis=1)
        out_ref[0, :, :p_len - LANES] = shifted[:, :p_len - LANES]
        out_ref[0, :, p_len - LANES:] = jnp.where(lane < LANES - t, shifted[:, p_len - LANES:], moved)

    def lanes(p):
        return slice(p * LANES, (p + 1) * LANES)

    def pair(p):
        kt = jnp.concatenate([kc_ref[0, lanes(p), :], k_new[lanes(p)]], axis=1).astype(BF16)
        vt = jnp.concatenate([vc_ref[0, lanes(p), :], v_new[lanes(p)]], axis=1).astype(BF16)
        bias2 = jnp.concatenate([bias_ref[2 * p], bias_ref[2 * p + 1]], axis=0)
        return q_ref[0, :, lanes(p)], kt, vt, bias2, g_ref[0, :, lanes(p)]

    for p, o in enumerate(_band_attend([pair(p) for p in range(q_ref.shape[2] // LANES)])):
        o_ref[0, :, lanes(p)] = o.astype(o_ref.dtype)


def _band_sample_bias(rel_bias, tq):
    qi = jnp.arange(tq)[:, None]
    kj = jnp.arange(CA_LEFT + LANES)[None, :]
    return jnp.where(kj < CA_LEFT + tq, _bias_lookup(rel_bias, qi - kj + CA_LEFT), NEG_BIG).astype(F32)


def _band_sample(q, knt, vnt, kct, vct, g, bias):
    b, t, wdt = q.shape
    p = kct.shape[2]
    step_lanes = BAND_SAMPLE_PAIRS * LANES
    assert p == CA_LEFT and t == CHUNK and wdt % step_lanes == 0 and knt.shape == (wdt, b * t)
    tok = pl.BlockSpec((1, t, step_lanes), lambda bi, hp: (bi, 0, hp))
    new = pl.BlockSpec((step_lanes, LANES), lambda bi, hp: (hp, bi * t // LANES))
    old = pl.BlockSpec((1, step_lanes, p), lambda bi, hp: (bi, hp, 0))
    bspec = pl.BlockSpec((2 * BAND_SAMPLE_PAIRS, t, CA_LEFT + LANES), lambda bi, hp: (hp, 0, 0))
    return pl.pallas_call(
        _band_sample_kernel,
        grid=(b, wdt // step_lanes),
        in_specs=[tok, new, new, old, old, tok, bspec],
        out_specs=[tok, old, old],
        out_shape=[jax.ShapeDtypeStruct((b, t, wdt), BF16), jax.ShapeDtypeStruct(kct.shape, F32),
                   jax.ShapeDtypeStruct(vct.shape, F32)],
        compiler_params=_params("parallel", "parallel"),
        name="band_sample",
    )(q, knt, vnt, kct, vct, g, bias)


def _feature_major(cache):
    b, p, h, d = cache.shape
    return jnp.transpose(cache, (0, 2, 3, 1)).reshape(b, h * d, p)


def _token_major(kt, heads):
    b, _, t = kt.shape
    return jnp.transpose(kt.reshape(b, heads, HEAD_DIM, t), (0, 3, 1, 2))


def _new_keys(w_t, xb, row0, nrows, has_cache, keep):
    b, l, d = xb.shape
    if not has_cache:
        return _matmul_t(w_t, xb, row0, nrows, keep)
    _, kt = _matmul_t(w_t, xb.reshape(1, b * l, d), row0, nrows, b * l)
    return None, kt[0]


def _sample_token_major(kt, b, heads):
    t = kt.shape[1] // b
    return jnp.transpose(kt.reshape(heads, HEAD_DIM, b, t), (2, 3, 0, 1))


def _even_layer(x, past_k, past_v, conv_state, h0, wts, ssd_chunk):
    w_in_t, w_dt_t, conv_w, conv_b, dt_bias, a_log, d_skip, norm_w, w_out, ln_g, ln_b = wts
    b, l, _ = x.shape
    m = b * l
    heads = SB_WIDTH // HEAD_DIM
    x2 = x.reshape(m, D_MODEL)
    qb, xb2 = _matmul(x2, w_in_t, L0_Q, SB_WIDTH, (BF16,), w_is_t=True, copy_x=True)
    xb = xb2.reshape(b, l, D_MODEL)
    kt_b, kt = _new_keys(w_in_t, xb, L0_K, SB_WIDTH, past_k is not None, l)
    vt_b, vt = _new_keys(w_in_t, xb, L0_V, SB_WIDTH, past_k is not None, l)
    ga, dt_raw = _matmul(xb2, w_in_t, L0_G, SB_WIDTH, (F32,), w_side=w_dt_t, w_is_t=True)
    (z,) = _matmul(xb2, w_in_t, L0_Z, SSD_WIDTH, (F32,), w_is_t=True)
    (xbc,) = _matmul(xb2, w_in_t, L0_XBC, SSD_CONV_DIM, (F32,), w_is_t=True)

    def seq(a):
        return a.reshape(b, l, a.shape[-1])

    if past_k is None:
        o_a = _sb_prompt(seq(qb), kt_b, vt_b, seq(ga))
        k_new, v_new = _token_major(kt, heads), _token_major(vt, heads)
    else:
        o_a = _sb_sample(seq(qb), kt, vt, seq(ga), _feature_major(past_k), _feature_major(past_v))
        k_new, v_new = _sample_token_major(kt, b, heads), _sample_token_major(vt, b, heads)
    xbc3 = seq(xbc)
    o_b, h_new = _ssd(xbc3, seq(z), seq(dt_raw), conv_state, h0, conv_w, conv_b, dt_bias, a_log,
                      d_skip, norm_w, ssd_chunk)
    y, yb = _proj_ln([o_a.reshape(m, SB_WIDTH), o_b.reshape(m, SSD_WIDTH)], x2, w_out, ln_g, ln_b,
                     (F32, BF16))
    tail = jnp.concatenate([conv_state, xbc3], axis=1)[:, -(SSD_CONV - 1):]
    return seq(y), seq(yb), k_new, v_new, tail, h_new.reshape(b, SSD_HEADS, HEAD_DIM, SSD_STATE)


def _odd_layer(x, xb, past_k, past_v, wts):
    w_in, w_kv_t, bias, w_out, ln_g, ln_b = wts
    b, l, _ = x.shape
    m = b * l
    x2 = x.reshape(m, D_MODEL)
    xb2 = xb.reshape(m, D_MODEL)
    keep = min(CA_LEFT, l)
    (qb,) = _matmul(xb2, w_in, 0, CA_WIDTH, (BF16,))
    kt_b, kt = _new_keys(w_kv_t, xb, 0, CA_WIDTH, past_k is not None, keep)
    vt_b, vt = _new_keys(w_kv_t, xb, CA_WIDTH, CA_WIDTH, past_k is not None, keep)
    (g,) = _matmul(xb2, w_in, 3 * CA_WIDTH, CA_WIDTH, (F32,))

    def seq(a):
        return a.reshape(b, l, a.shape[-1])

    if past_k is None:
        o = _band_prompt(seq(qb), kt_b, vt_b, seq(g), bias)
    else:
        o, kt, vt = _band_sample(seq(qb), kt, vt, _feature_major(past_k), _feature_major(past_v), seq(g),
                                 bias)
    (y,) = _proj_ln([o.reshape(m, CA_WIDTH)], x2, w_out, ln_g, ln_b, (F32,))
    heads = CA_WIDTH // HEAD_DIM
    return seq(y), _token_major(kt, heads), _token_major(vt, heads)


def kernel(x_prompt, x_sample, cache_sb_k, cache_sb_v, state_ssm, state_conv, cache_band_k,
           cache_band_v, even_w_in, even_conv_w, even_conv_b, even_dt_bias, even_a_log, even_d_skip,
           even_norm_w, even_w_out, even_ln_g, even_ln_b, odd_w_in, odd_rel_bias, odd_w_out,
           odd_ln_g, odd_ln_b):
    bp, lp, _ = x_prompt.shape
    bs, ls, _ = x_sample.shape
    w_in_t0 = jnp.transpose(even_w_in[0])
    w_dt_t0 = jnp.pad(w_in_t0[L0_DT:].astype(BF16), ((0, LANES - SSD_HEADS), (0, 0)))
    wts0 = (w_in_t0, w_dt_t0, even_conv_w[0], even_conv_b[0], even_dt_bias[0], even_a_log[0],
            even_d_skip[0], even_norm_w[0], even_w_out[0].astype(BF16), even_ln_g[0], even_ln_b[0])
    zero_conv = jnp.zeros((bp, SSD_CONV - 1, SSD_CONV_DIM), F32)
    zero_h = jnp.zeros((bp, SSD_WIDTH, SSD_STATE), F32)
    yp, ypb, p_sb_k, p_sb_v, p_conv, p_ssm = _even_layer(
        x_prompt, None, None, zero_conv, zero_h, wts0, 2 * CHUNK)
    ys, ysb, s_sb_k, s_sb_v, s_conv, s_ssm = _even_layer(
        x_sample, cache_sb_k[0], cache_sb_v[0], state_conv[0],
        state_ssm[0].reshape(bs, SSD_WIDTH, SSD_STATE), wts0, CHUNK)
    w_in1 = odd_w_in[0]
    w_kv_t1 = jnp.transpose(odd_w_in[0][:, CA_WIDTH:3 * CA_WIDTH]).astype(BF16)
    w_out1 = odd_w_out[0].astype(BF16)
    wts1p = (w_in1, w_kv_t1, _band_bias(odd_rel_bias[0]), w_out1, odd_ln_g[0], odd_ln_b[0])
    wts1s = (w_in1, w_kv_t1, _band_sample_bias(odd_rel_bias[0], ls), w_out1, odd_ln_g[0], odd_ln_b[0])
    yp, p_band_k, p_band_v = _odd_layer(yp, ypb, None, None, wts1p)
    ys, s_band_k, s_band_v = _odd_layer(ys, ysb, cache_band_k[0], cache_band_v[0], wts1s)

    def lead(a):
        return a[None]

    return (yp, ys, lead(p_sb_k), lead(p_sb_v), lead(p_ssm), lead(p_conv), lead(p_band_k),
            lead(p_band_v), lead(s_sb_k), lead(s_sb_v), lead(s_ssm), lead(s_conv), lead(s_band_k),
            lead(s_band_v))
```

```python
import functools
import math

import jax
import jax.numpy as jnp
from jax import lax
from jax.experimental import pallas as pl
from jax.experimental.pallas import tpu as pltpu

F32 = jnp.float32
BF16 = jnp.bfloat16

D_MODEL = 2048
CHUNK = 64
HEAD_DIM = 64
LANES = 128
SB_WIDTH = 1024
SSD_WIDTH = 2048
SSD_HEADS = 32
SSD_GROUPS = 4
SSD_STATE = 128
SSD_CONV = 4
SSD_CONV_DIM = SSD_WIDTH + 2 * SSD_GROUPS * SSD_STATE
CA_WIDTH = 2048
CA_LEFT = 8 * CHUNK
CA_REL_CLIP = 128
DEPTH = 2
DEEPNORM_ALPHA = (2 * DEPTH) ** 0.25
LN_EPS = 1e-5
RMS_EPS = 1e-5
ATTN_SCALE = HEAD_DIM ** -0.5
NEG_BIG = -1e30
L0_Q, L0_K, L0_V, L0_G, L0_Z, L0_XBC, L0_DT = 0, 1024, 2048, 3072, 4096, 6144, 9216
VMEM_LIMIT = 56 * 1024 * 1024
ROW_TILE = 512
COL_TILE = 1024

NT_DIMS = (((1,), (1,)), ((), ()))
TN_DIMS = (((0,), (0,)), ((), ()))


def _params(*sem):
    return pltpu.CompilerParams(dimension_semantics=sem, vmem_limit_bytes=VMEM_LIMIT)


def _silu(x):
    return x * (1.0 / (1.0 + jnp.exp(-x)))


def _split3(x):
    p0 = x.astype(BF16)
    r = x - p0.astype(F32)
    p1 = r.astype(BF16)
    p2 = (r - p1.astype(F32)).astype(BF16)
    return p0, p1, p2


def _mm_kernel(x_ref, w_ref, *refs, has_side, w_is_t, copy_x):
    *o_refs, wb_ref = refs

    @pl.when(pl.program_id(1) == 0)
    def _():
        wb_ref[...] = w_ref[...].astype(BF16)

    def product(w):
        if w_is_t:
            return lax.dot_general(x, w, NT_DIMS, preferred_element_type=F32)
        return jnp.dot(x, w, preferred_element_type=F32)

    x = x_ref[...].astype(BF16)
    if copy_x:
        o_refs.pop()[...] = x
    if has_side:
        ws_ref = o_refs.pop(0)
        os_ref = o_refs.pop()
        os_ref[...] = product(ws_ref[...])
    acc = product(wb_ref[...])
    for o_ref in o_refs:
        o_ref[...] = acc.astype(o_ref.dtype)


def _matmul(x, w, col0, ncols, out_dtypes, w_side=None, w_is_t=False, copy_x=False):
    m, k = x.shape
    tm, tn = math.gcd(m, 2 * ROW_TILE), COL_TILE
    assert ncols % tn == 0 and col0 % tn == 0
    cb0 = col0 // tn
    if w_is_t:
        w_spec = pl.BlockSpec((tn, k), lambda j, i: (cb0 + j, 0))
    else:
        w_spec = pl.BlockSpec((k, tn), lambda j, i: (0, cb0 + j))
    in_specs = [pl.BlockSpec((tm, k), lambda j, i: (i, 0)), w_spec]
    out_specs = [pl.BlockSpec((tm, tn), lambda j, i: (i, j)) for _ in out_dtypes]
    out_shape = [jax.ShapeDtypeStruct((m, ncols), dt) for dt in out_dtypes]
    args = [x, w]
    if w_side is not None:
        assert ncols == tn
        ns = w_side.shape[0 if w_is_t else 1]
        in_specs.append(pl.BlockSpec(w_side.shape, lambda j, i: (0, 0)))
        out_specs.append(pl.BlockSpec((tm, ns), lambda j, i: (i, 0)))
        out_shape.append(jax.ShapeDtypeStruct((m, ns), F32))
        args.append(w_side)
    if copy_x:
        assert ncols == tn
        out_specs.append(pl.BlockSpec((tm, k), lambda j, i: (i, 0)))
        out_shape.append(jax.ShapeDtypeStruct((m, k), BF16))
    return pl.pallas_call(
        functools.partial(_mm_kernel, has_side=w_side is not None, w_is_t=w_is_t, copy_x=copy_x),
        grid=(ncols // tn, m // tm),
        in_specs=in_specs,
        out_specs=out_specs,
        out_shape=out_shape,
        scratch_shapes=[pltpu.VMEM(w_spec.block_shape, BF16)],
        compiler_params=_params("arbitrary", "arbitrary"),
        name=f"mm_c{col0}_n{ncols}_m{m}",
    )(*args)


def _mm_t_kernel(w_ref, x_ref, ob_ref, of_ref, *scratch, first_kept):
    if scratch:
        (wb_ref,) = scratch

        @pl.when(jnp.logical_and(pl.program_id(0) == 0, pl.program_id(1) == 0))
        def _():
            wb_ref[...] = w_ref[...].astype(BF16)

        w = wb_ref[...]
    else:
        w = w_ref[...]
    acc = lax.dot_general(w, x_ref[0], NT_DIMS, preferred_element_type=F32)
    ob_ref[0] = acc.astype(ob_ref.dtype)
    kept_cols = of_ref.shape[2]

    def keep_f32():
        of_ref[0] = acc[:, acc.shape[1] - kept_cols:]

    if first_kept == 0:
        keep_f32()
    else:
        pl.when(pl.program_id(1) >= first_kept)(keep_f32)


def _matmul_t(w_t, x, row0, nrows, keep):
    s, l, k = x.shape
    tm = math.gcd(l, 2 * ROW_TILE)
    kept_cols = min(keep, tm)
    assert row0 % nrows == 0 and keep <= l and keep % kept_cols == 0 and tm % kept_cols == 0
    first_kept = (l - keep) // tm
    return pl.pallas_call(
        functools.partial(_mm_t_kernel, first_kept=first_kept),
        grid=(s, l // tm),
        in_specs=[pl.BlockSpec((nrows, k), lambda si, i: (row0 // nrows, 0), pipeline_mode=pl.Buffered(1)),
                  pl.BlockSpec((1, tm, k), lambda si, i: (si, i, 0))],
        out_specs=[pl.BlockSpec((1, nrows, tm), lambda si, i: (si, 0, i)),
                   pl.BlockSpec((1, nrows, kept_cols),
                                lambda si, i: (si, 0, jnp.maximum(i - first_kept, 0)))],
        out_shape=[jax.ShapeDtypeStruct((s, nrows, l), BF16),
                   jax.ShapeDtypeStruct((s, nrows, keep), F32)],
        scratch_shapes=[] if w_t.dtype == BF16 else [pltpu.VMEM((nrows, k), BF16)],
        compiler_params=_params("arbitrary", "arbitrary"),
        name=f"mm_t_r{row0}_n{nrows}_l{l}",
    )(w_t, x)


def _proj_ln_kernel(*refs, widths):
    n = len(widths)
    o_refs = refs[:n]
    x_ref, w_ref, g_ref, b_ref, *y_refs = refs[n:]
    mix = None
    row = 0
    for o_ref, wd in zip(o_refs, widths):
        part = jnp.dot(o_ref[...], w_ref[row:row + wd, :], preferred_element_type=F32)
        mix = part if mix is None else mix + part
        row += wd
    h = DEEPNORM_ALPHA * x_ref[...] + mix
    mu = jnp.mean(h, axis=-1, keepdims=True)
    hc = h - mu
    var = jnp.mean(hc * hc, axis=-1, keepdims=True)
    y = hc * lax.rsqrt(var + LN_EPS) * g_ref[...] + b_ref[...]
    for y_ref in y_refs:
        y_ref[...] = y.astype(y_ref.dtype)


def _proj_ln(o_parts, x, w_out, ln_g, ln_b, out_dtypes):
    m = x.shape[0]
    widths = tuple(o.shape[1] for o in o_parts)
    ktot = sum(widths)
    tm = math.gcd(m, ROW_TILE)
    assert w_out.shape == (ktot, D_MODEL)
    in_specs = [pl.BlockSpec((tm, wd), lambda i: (i, 0)) for wd in widths]
    in_specs += [pl.BlockSpec((tm, D_MODEL), lambda i: (i, 0)),
                 pl.BlockSpec((ktot, D_MODEL), lambda i: (0, 0), pipeline_mode=pl.Buffered(1)),
                 pl.BlockSpec((1, D_MODEL), lambda i: (0, 0)),
                 pl.BlockSpec((1, D_MODEL), lambda i: (0, 0))]
    return pl.pallas_call(
        functools.partial(_proj_ln_kernel, widths=widths),
        grid=(m // tm,),
        in_specs=in_specs,
        out_specs=[pl.BlockSpec((tm, D_MODEL), lambda i: (i, 0)) for _ in out_dtypes],
        out_shape=[jax.ShapeDtypeStruct((m, D_MODEL), dt) for dt in out_dtypes],
        compiler_params=_params("parallel"),
        name=f"proj_ln_k{ktot}_m{m}",
    )(*o_parts, x, w_out, ln_g.reshape(1, D_MODEL), ln_b.reshape(1, D_MODEL))


def _head_masks(rows):
    lane = lax.broadcasted_iota(jnp.int32, (rows, LANES), 1)
    return lane < HEAD_DIM


SB_KEYS = 256
SB_STEP_LANES = 8 * LANES
SB_DEAD = 104.5
LOG2E = 1.4426950408889634


def _pair_queries(q2):
    head0 = _head_masks(q2.shape[0])
    qs = q2 * jnp.asarray(ATTN_SCALE, q2.dtype)
    zero = jnp.zeros_like(qs)
    return jnp.concatenate([jnp.where(head0, qs, zero), jnp.where(head0, zero, qs)], axis=0)


def _pair_outputs(stacked):
    tq = stacked.shape[0] // 2
    return jnp.where(_head_masks(tq), stacked[:tq], stacked[tq:])


def _sb_visit(qh, k_pairs, v_pairs, carries, accs, upper, diag_mask):
    zs = [jnp.dot(q2, kt, preferred_element_type=F32) for q2, kt in zip(qh, k_pairs)]
    sps = [jnp.maximum(z, 0.0) + jnp.log(1.0 + jnp.exp(-jnp.abs(z))) for z in zs]
    if diag_mask is not None:
        sps = [jnp.where(diag_mask, sp, 0.0) for sp in sps]
    masses = [jnp.dot(sp.astype(BF16), upper, preferred_element_type=F32) + c
              for sp, c in zip(sps, carries)]
    ws = [jnp.exp(z - mass) for z, mass in zip(zs, masses)]
    if diag_mask is not None:
        ws = [jnp.where(diag_mask, w, 0.0) for w in ws]
    accs = tuple(a + lax.dot_general(w.astype(BF16), vt, NT_DIMS, preferred_element_type=F32)
                 for a, w, vt in zip(accs, ws, v_pairs))
    return tuple(mass[:, 0:1] for mass in masses), accs


def _sb_upper(n):
    r = lax.broadcasted_iota(jnp.int32, (n, n), 0)
    c = lax.broadcasted_iota(jnp.int32, (n, n), 1)
    return (r >= c).astype(BF16)


def _sb_alive(carries):
    return jnp.min(functools.reduce(jnp.minimum, carries)) < SB_DEAD


def _sb_queries(q_ref):
    return [_pair_queries(q_ref[0, :, p * LANES:(p + 1) * LANES]) for p in range(q_ref.shape[2] // LANES)]


def _sb_strict(tq, keys):
    r = lax.broadcasted_iota(jnp.int32, (2 * tq, keys), 0)
    c = lax.broadcasted_iota(jnp.int32, (2 * tq, keys), 1)
    return c < jnp.where(r >= tq, r - tq, r)


def _sb_finish(accs, g_ref, o_ref):
    for p, acc in enumerate(accs):
        lanes = slice(p * LANES, (p + 1) * LANES)
        o_ref[0, :, lanes] = (_pair_outputs(acc) * _silu(g_ref[0, :, lanes])).astype(o_ref.dtype)


def _pair_slabs(ref, lead, n_pair, cols):
    return [ref[lead, p * LANES:(p + 1) * LANES, cols].astype(BF16) for p in range(n_pair)]


def _sb_prompt_kernel(q_ref, k_ref, v_ref, g_ref, o_ref):
    qi = pl.program_id(2)
    tq = q_ref.shape[1]
    assert tq == SB_KEYS
    qh = _sb_queries(q_ref)
    n_pair = len(qh)
    upper = _sb_upper(SB_KEYS)
    strict = _sb_strict(tq, SB_KEYS)

    def visit(kb, carries, accs, mask):
        cols = pl.ds(pl.multiple_of(kb * SB_KEYS, SB_KEYS), SB_KEYS)
        return _sb_visit(qh, _pair_slabs(k_ref, 0, n_pair, cols), _pair_slabs(v_ref, 0, n_pair, cols),
                         carries, accs, upper, mask)

    zc = (jnp.zeros((2 * tq, 1), F32),) * n_pair
    za = (jnp.zeros((2 * tq, LANES), F32),) * n_pair
    carries, accs = visit(qi, zc, za, strict)

    def older(state):
        kb, _, carries, accs = state
        carries, accs = visit(kb, carries, accs, None)
        return kb - 1, _sb_alive(carries), carries, accs

    state = lax.while_loop(lambda s: jnp.logical_and(s[0] >= 0, s[1]), older,
                           (qi - 1, _sb_alive(carries), carries, accs))
    _sb_finish(state[3], g_ref, o_ref)


def _sb_prompt(q, kt, vt, g):
    b, l, wdt = q.shape
    tq = SB_KEYS
    assert l % tq == 0 and wdt % SB_STEP_LANES == 0
    blk = pl.BlockSpec((1, tq, SB_STEP_LANES), lambda bi, hp, qi: (bi, qi, hp))
    full = pl.BlockSpec((1, SB_STEP_LANES, l), lambda bi, hp, qi: (bi, hp, 0))
    return pl.pallas_call(
        _sb_prompt_kernel,
        grid=(b, wdt // SB_STEP_LANES, l // tq),
        in_specs=[blk, full, full, blk],
        out_specs=blk,
        out_shape=jax.ShapeDtypeStruct((b, l, wdt), BF16),
        compiler_params=_params("parallel", "parallel", "arbitrary"),
        name="sb_prompt",
    )(q, kt, vt, g)


def _sb_sample_kernel(q_ref, kn_ref, vn_ref, g_ref, kc_hbm, vc_hbm, o_ref, kbuf, vbuf, sem, *, n_cached):
    seq = pl.program_id(0)
    tq = q_ref.shape[1]
    qh = _sb_queries(q_ref)
    n_pair = len(qh)
    upper = _sb_upper(SB_KEYS)
    strict = _sb_strict(tq, LANES)
    zc = (jnp.zeros((2 * tq, 1), F32),) * n_pair
    za = (jnp.zeros((2 * tq, LANES), F32),) * n_pair

    def block_copies(kb, slot):
        cols = pl.ds(pl.multiple_of(kb * SB_KEYS, SB_KEYS), SB_KEYS)
        return (pltpu.make_async_copy(kc_hbm.at[seq, :, cols], kbuf.at[slot], sem.at[0, slot]),
                pltpu.make_async_copy(vc_hbm.at[seq, :, cols], vbuf.at[slot], sem.at[1, slot]))

    def start_block(kb):
        for cp in block_copies(kb, kb % 2):
            cp.start()

    def wait_block(kb):
        for cp in block_copies(kb, kb % 2):
            cp.wait()

    start_block(n_cached - 1)
    everything = slice(None)

    def new_pairs(ref):
        return [_own_columns(ref[p * LANES:(p + 1) * LANES, :], seq, tq).astype(BF16) for p in range(n_pair)]

    carries, accs = _sb_visit(qh, new_pairs(kn_ref), new_pairs(vn_ref), zc, za, _sb_upper(LANES), strict)

    def older(state):
        kb, _, carries, accs = state
        wait_block(kb)
        pl.when(kb > 0)(lambda: start_block(kb - 1))
        slot = kb % 2
        carries, accs = _sb_visit(qh, _pair_slabs(kbuf, slot, n_pair, everything),
                                  _pair_slabs(vbuf, slot, n_pair, everything), carries, accs, upper, None)
        return kb - 1, _sb_alive(carries), carries, accs

    state = lax.while_loop(lambda s: jnp.logical_and(s[0] >= 0, s[1]), older,
                           (jnp.int32(n_cached - 1), _sb_alive(carries), carries, accs))
    pl.when(state[0] >= 0)(lambda: wait_block(state[0]))
    _sb_finish(state[3], g_ref, o_ref)


def _own_columns(block, seq, t):
    assert 2 * t == LANES
    return jnp.where(seq % 2 == 1, pltpu.roll(block, t, axis=1), block)


def _sb_sample(q, knt, vnt, g, kct, vct):
    b, t, wdt = q.shape
    p = kct.shape[2]
    assert p % SB_KEYS == 0 and knt.shape == (wdt, b * t)
    tok = pl.BlockSpec((1, t, wdt), lambda bi: (bi, 0, 0))
    new = pl.BlockSpec((wdt, LANES), lambda bi: (0, bi * t // LANES))
    hbm = pl.BlockSpec(memory_space=pl.ANY)
    return pl.pallas_call(
        functools.partial(_sb_sample_kernel, n_cached=p // SB_KEYS),
        grid=(b,),
        in_specs=[tok, new, new, tok, hbm, hbm],
        out_specs=tok,
        out_shape=jax.ShapeDtypeStruct((b, t, wdt), BF16),
        scratch_shapes=[pltpu.VMEM((2, wdt, SB_KEYS), F32),
                        pltpu.VMEM((2, wdt, SB_KEYS), F32),
                        pltpu.SemaphoreType.DMA((2, 2))],
        compiler_params=_params("arbitrary"),
        name="sb_sample",
    )(q, knt, vnt, g, kct, vct)


CONV_PAD = 8
CONV_SLAB = 512


def _ssd_kernel(xbc_ref, z_ref, dt_ref, cs_ref, h0_ref, cw_ref, cb_ref, dtb_ref, alog_ref, drow_ref,
                nw_ref, o_ref, h_ref, xp_ref, xa_ref, y_ref):
    ci = pl.program_id(1)
    q = xbc_ref.shape[1]
    taps = SSD_CONV - 1

    @pl.when(ci == 0)
    def _():
        xp_ref[0:CONV_PAD, :] = jnp.zeros((CONV_PAD, SSD_CONV_DIM), F32)
        xp_ref[CONV_PAD - taps:CONV_PAD, :] = cs_ref[0]
        h_ref[0] = h0_ref[0]

    @pl.when(ci > 0)
    def _():
        xp_ref[0:CONV_PAD, :] = xp_ref[q:CONV_PAD + q, :]

    xp_ref[CONV_PAD:CONV_PAD + q, :] = xbc_ref[0]
    for cs in range(0, SSD_CONV_DIM, CONV_SLAB):
        xe = xp_ref[:, cs:cs + CONV_SLAB]
        acc = cw_ref[0:1, cs:cs + CONV_SLAB] * xe
        for t in range(1, SSD_CONV):
            acc = cw_ref[t:t + 1, cs:cs + CONV_SLAB] * xe + pltpu.roll(acc, 1, axis=0)
        xa_ref[:, cs:cs + CONV_SLAB] = _silu(acc[CONV_PAD:] + cb_ref[:, cs:cs + CONV_SLAB])

    dtv = dt_ref[0] + dtb_ref[...]
    dtv = jnp.maximum(dtv, 0.0) + jnp.log1p(jnp.exp(-jnp.abs(dtv)))
    da = dtv * (-jnp.exp(alog_ref[...]))
    r = lax.broadcasted_iota(jnp.int32, (q, q), 0)
    c = lax.broadcasted_iota(jnp.int32, (q, q), 1)
    causal = c <= r
    tril = causal.astype(BF16)
    cum = sum(jnp.dot(tril, p, preferred_element_type=F32) for p in _split3(da))
    er = lax.broadcasted_iota(jnp.int32, (LANES, LANES), 0)
    ec = lax.broadcasted_iota(jnp.int32, (LANES, LANES), 1)
    eye = (er == ec).astype(BF16)

    def transpose(x):
        return sum(lax.dot_general(eye, p, NT_DIMS, preferred_element_type=F32) for p in _split3(x))

    cum2 = cum * LOG2E
    cum2_t = transpose(cum) * LOG2E
    dt_t = transpose(dtv)
    cum_last = cum[q - 1:q, :]
    to_end = jnp.exp(cum_last - cum) * dtv
    exp_cum = jnp.exp(cum)
    chunk_decay = jnp.broadcast_to(jnp.exp(cum_last), (LANES, LANES))
    head0 = _head_masks(q)
    top = lax.broadcasted_iota(jnp.int32, (LANES, LANES), 0) < HEAD_DIM

    for g in range(SSD_GROUPS):
        b0 = SSD_WIDTH + g * SSD_STATE
        c0 = SSD_WIDTH + SSD_GROUPS * SSD_STATE + g * SSD_STATE
        bm = xa_ref[:, b0:b0 + SSD_STATE].astype(BF16)
        cm = xa_ref[:, c0:c0 + SSD_STATE].astype(BF16)
        cb = lax.dot_general(cm, bm, NT_DIMS, preferred_element_type=F32)
        for j in range(SSD_HEADS // SSD_GROUPS // 2):
            pr = g * (SSD_HEADS // SSD_GROUPS // 2) + j
            h0, h1 = 2 * pr, 2 * pr + 1
            lo = pr * LANES
            x2 = xa_ref[:, lo:lo + LANES]
            x2b = x2.astype(BF16)
            yd = []
            for h in (h0, h1):
                seg = cum2[:, h:h + 1] - cum2_t[h:h + 1, :]
                wgt = cb * jnp.exp2(jnp.where(causal, seg, -jnp.inf)) * dt_t[h:h + 1, :]
                yd.append(jnp.dot(wgt.astype(BF16), x2b, preferred_element_type=F32))
            y = jnp.where(head0, yd[0], yd[1])
            hs = h_ref[0, lo:lo + LANES, :]
            yoff = lax.dot_general(cm, hs.astype(BF16), NT_DIMS, preferred_element_type=F32)
            y = y + yoff * jnp.where(head0, exp_cum[:, h0:h0 + 1], exp_cum[:, h1:h1 + 1])
            y_ref[:, lo:lo + LANES] = y + drow_ref[:, lo:lo + LANES] * x2
            xw = x2 * jnp.where(head0, to_end[:, h0:h0 + 1], to_end[:, h1:h1 + 1])
            st = lax.dot_general(xw.astype(BF16), bm, TN_DIMS, preferred_element_type=F32)
            dec = jnp.where(top, chunk_decay[:, h0:h0 + 1], chunk_decay[:, h1:h1 + 1])
            h_ref[0, lo:lo + LANES, :] = dec * hs + st

    gw = SSD_WIDTH // SSD_GROUPS
    for g in range(SSD_GROUPS):
        yg = y_ref[:, g * gw:(g + 1) * gw] * _silu(z_ref[0, :, g * gw:(g + 1) * gw])
        ms = jnp.mean(yg * yg, axis=-1, keepdims=True)
        o_ref[0, :, g * gw:(g + 1) * gw] = (yg * lax.rsqrt(ms + RMS_EPS)
                                            * nw_ref[:, g * gw:(g + 1) * gw]).astype(o_ref.dtype)


def _ssd(xbc, z, dt_raw, conv_state, h0, conv_w, conv_b, dt_bias, a_log, d_skip, norm_w, q):
    b, l, _ = xbc.shape
    assert l % q == 0
    pad = LANES - SSD_HEADS
    dtb = jnp.pad(dt_bias, (0, pad)).reshape(1, LANES)
    alog = jnp.pad(a_log, (0, pad)).reshape(1, LANES)
    drow = jnp.repeat(d_skip, HEAD_DIM).reshape(1, SSD_WIDTH)

    def tok(wd):
        return pl.BlockSpec((1, q, wd), lambda bi, ci: (bi, ci, 0))

    def per_seq(shape):
        return pl.BlockSpec((1,) + shape, lambda bi, ci: (bi, 0, 0))

    def const(shape):
        return pl.BlockSpec(shape, lambda bi, ci: (0, 0))

    o, h = pl.pallas_call(
        _ssd_kernel,
        grid=(b, l // q),
        in_specs=[tok(SSD_CONV_DIM), tok(SSD_WIDTH), tok(LANES),
                  per_seq((SSD_CONV - 1, SSD_CONV_DIM)), per_seq((SSD_WIDTH, SSD_STATE)),
                  const((SSD_CONV, SSD_CONV_DIM)), const((1, SSD_CONV_DIM)), const((1, LANES)),
                  const((1, LANES)), const((1, SSD_WIDTH)), const((1, SSD_WIDTH))],
        out_specs=[tok(SSD_WIDTH), per_seq((SSD_WIDTH, SSD_STATE))],
        out_shape=[jax.ShapeDtypeStruct((b, l, SSD_WIDTH), BF16),
                   jax.ShapeDtypeStruct((b, SSD_WIDTH, SSD_STATE), F32)],
        scratch_shapes=[pltpu.VMEM((CONV_PAD + q, SSD_CONV_DIM), F32),
                        pltpu.VMEM((q, SSD_CONV_DIM), F32),
                        pltpu.VMEM((q, SSD_WIDTH), F32)],
        compiler_params=_params("parallel", "arbitrary"),
        name=f"ssd_q{q}",
    )(xbc, z, dt_raw, conv_state, h0, conv_w, conv_b.reshape(1, SSD_CONV_DIM), dtb, alog, drow,
      norm_w.reshape(1, SSD_WIDTH))
    return o, h


BAND_BLOCKS = CA_LEFT // LANES + 1
BAND_PROMPT_PAIRS = 8


def _band_attend(jobs):
    scores = [jnp.dot(_pair_queries(q2), kt, preferred_element_type=F32) + bias2
              for q2, kt, _, bias2, _ in jobs]
    probs = [jnp.exp(s - jnp.max(s, axis=-1, keepdims=True)) for s in scores]
    dens = [jnp.sum(e, axis=-1, keepdims=True) for e in probs]
    nums = [lax.dot_general(e.astype(BF16), job[2], NT_DIMS, preferred_element_type=F32)
            for e, job in zip(probs, jobs)]
    return [_pair_outputs(num / den) * _silu(job[4]) for num, den, job in zip(nums, dens, jobs)]


def _band_prompt_kernel(q_ref, k_ref, v_ref, g_ref, bias_ref, o_ref):
    step = pl.program_id(2)
    n_sub = q_ref.shape[1] // LANES
    n_pair = q_ref.shape[2] // LANES
    band = BAND_BLOCKS * LANES

    def tile(n):
        return slice(n * LANES, (n + 1) * LANES)

    def sub_tile(p, s, i):
        if isinstance(i, int):
            keys = min(i + 1, BAND_BLOCKS) * LANES
            cols = pl.ds((i + 1) * LANES - keys, keys)
        else:
            keys = band
            cols = pl.ds(pl.multiple_of((i + 1) * LANES - band, LANES), band)
        bias2 = jnp.concatenate([bias_ref[2 * p, :, band - keys:], bias_ref[2 * p + 1, :, band - keys:]],
                                axis=0)
        return (q_ref[0, tile(s), tile(p)], k_ref[0, tile(p), cols], v_ref[0, tile(p), cols], bias2,
                g_ref[0, tile(s), tile(p)])

    def step_tiles(first):
        for p in range(n_pair):
            outs = _band_attend([sub_tile(p, s, first + s) for s in range(n_sub)])
            for s, o in enumerate(outs):
                o_ref[0, tile(s), tile(p)] = o.astype(o_ref.dtype)

    pl.when(step == 0)(lambda: step_tiles(0))
    pl.when(step > 0)(lambda: step_tiles(step * n_sub))


def _bias_lookup(rel_bias, t_minus_s):
    rel = jnp.clip(t_minus_s, -CA_REL_CLIP, CA_REL_CLIP) + CA_REL_CLIP
    onehot = (rel[..., None] == jnp.arange(2 * CA_REL_CLIP + 1)).astype(F32)
    return jnp.einsum("rck,hk->hrc", onehot, rel_bias, precision=lax.Precision.HIGHEST)


def _band_bias(rel_bias):
    r = jnp.arange(LANES)[:, None]
    u = jnp.arange(LANES)[None, :]

    def tile(dist):
        return _bias_lookup(rel_bias, dist * LANES + r - u)

    t0 = jnp.where((r < CHUNK) & (u >= CHUNK), NEG_BIG, tile(0))
    t1 = tile(1)
    t2 = jnp.broadcast_to(rel_bias[:, -1][:, None, None], t1.shape)
    t4 = jnp.where((r >= CHUNK) & (u < CHUNK), NEG_BIG, t2)
    return jnp.concatenate([t4, t2, t2, t1, t0], axis=-1).astype(F32)


def _band_prompt(q, kt, vt, g, band_bias):
    b, l, wdt = q.shape
    n_sub = math.gcd(l // LANES, BAND_BLOCKS - 1)
    assert n_sub == BAND_BLOCKS - 1 or l // LANES == n_sub
    tq = n_sub * LANES
    step_lanes = BAND_PROMPT_PAIRS * LANES
    assert wdt % step_lanes == 0
    blk = pl.BlockSpec((1, tq, step_lanes), lambda bi, hp, i: (bi, i, hp))
    full = pl.BlockSpec((1, step_lanes, l), lambda bi, hp, i: (bi, hp, 0))
    bias = pl.BlockSpec((2 * BAND_PROMPT_PAIRS, LANES, BAND_BLOCKS * LANES), lambda bi, hp, i: (hp, 0, 0))
    return pl.pallas_call(
        _band_prompt_kernel,
        grid=(b, wdt // step_lanes, l // tq),
        in_specs=[blk, full, full, blk, bias],
        out_specs=blk,
        out_shape=jax.ShapeDtypeStruct((b, l, wdt), BF16),
        compiler_params=_params("parallel", "parallel", "arbitrary"),
        name="band_prompt",
    )(q, kt, vt, g, band_bias)


BAND_SAMPLE_PAIRS = 4


def _band_sample_kernel(q_ref, kn_ref, vn_ref, kc_ref, vc_ref, g_ref, bias_ref, o_ref, ok_ref, ov_ref):
    seq = pl.program_id(0)
    t = q_ref.shape[1]
    p_len = kc_ref.shape[2]
    lane = lax.broadcasted_iota(jnp.int32, (kc_ref.shape[1], LANES), 1)
    k_new = _own_columns(kn_ref[...], seq, t)
    v_new = _own_columns(vn_ref[...], seq, t)
    for cache_ref, new, out_ref in ((kc_ref, k_new, ok_ref), (vc_ref, v_new, ov_ref)):
        shifted = pltpu.roll(cache_ref[0], p_len - t, axis=1)
        moved = pltpu.roll(new, LANES - t, axis=1)
        out_ref[0, :, :p_len - LANES] = shifted[:, :p_len - LANES]
        out_ref[0, :, p_len - LANES:] = jnp.where(lane < LANES - t, shifted[:, p_len - LANES:], moved)

    def lanes(p):
        return slice(p * LANES, (p + 1) * LANES)

    def pair(p):
        kt = jnp.concatenate([kc_ref[0, lanes(p), :], k_new[lanes(p)]], axis=1).astype(BF16)
        vt = jnp.concatenate([vc_ref[0, lanes(p), :], v_new[lanes(p)]], axis=1).astype(BF16)
        bias2 = jnp.concatenate([bias_ref[2 * p], bias_ref[2 * p + 1]], axis=0)
        return q_ref[0, :, lanes(p)], kt, vt, bias2, g_ref[0, :, lanes(p)]

    for p, o in enumerate(_band_attend([pair(p) for p in range(q_ref.shape[2] // LANES)])):
        o_ref[0, :, lanes(p)] = o.astype(o_ref.dtype)


def _band_sample_bias(rel_bias, tq):
    qi = jnp.arange(tq)[:, None]
    kj = jnp.arange(CA_LEFT + LANES)[None, :]
    return jnp.where(kj < CA_LEFT + tq, _bias_lookup(rel_bias, qi - kj + CA_LEFT), NEG_BIG).astype(F32)


def _band_sample(q, knt, vnt, kct, vct, g, bias):
    b, t, wdt = q.shape
    p = kct.shape[2]
    step_lanes = BAND_SAMPLE_PAIRS * LANES
    assert p == CA_LEFT and t == CHUNK and wdt % step_lanes == 0 and knt.shape == (wdt, b * t)
    tok = pl.BlockSpec((1, t, step_lanes), lambda bi, hp: (bi, 0, hp))
    new = pl.BlockSpec((step_lanes, LANES), lambda bi, hp: (hp, bi * t // LANES))
    old = pl.BlockSpec((1, step_lanes, p), lambda bi, hp: (bi, hp, 0))
    bspec = pl.BlockSpec((2 * BAND_SAMPLE_PAIRS, t, CA_LEFT + LANES), lambda bi, hp: (hp, 0, 0))
    return pl.pallas_call(
        _band_sample_kernel,
        grid=(b, wdt // step_lanes),
        in_specs=[tok, new, new, old, old, tok, bspec],
        out_specs=[tok, old, old],
        out_shape=[jax.ShapeDtypeStruct((b, t, wdt), BF16), jax.ShapeDtypeStruct(kct.shape, F32),
                   jax.ShapeDtypeStruct(vct.shape, F32)],
        compiler_params=_params("parallel", "parallel"),
        name="band_sample",
    )(q, knt, vnt, kct, vct, g, bias)


def _feature_major(cache):
    b, p, h, d = cache.shape
    return jnp.transpose(cache, (0, 2, 3, 1)).reshape(b, h * d, p)


def _token_major(kt, heads):
    b, _, t = kt.shape
    return jnp.transpose(kt.reshape(b, heads, HEAD_DIM, t), (0, 3, 1, 2))


def _new_keys(w_t, xb, row0, nrows, has_cache, keep):
    b, l, d = xb.shape
    if not has_cache:
        return _matmul_t(w_t, xb, row0, nrows, keep)
    _, kt = _matmul_t(w_t, xb.reshape(1, b * l, d), row0, nrows, b * l)
    return None, kt[0]


def _sample_token_major(kt, b, heads):
    t = kt.shape[1] // b
    return jnp.transpose(kt.reshape(heads, HEAD_DIM, b, t), (2, 3, 0, 1))


def _even_layer(x, past_k, past_v, conv_state, h0, wts, ssd_chunk):
    w_in_t, w_dt_t, conv_w, conv_b, dt_bias, a_log, d_skip, norm_w, w_out, ln_g, ln_b = wts
    b, l, _ = x.shape
    m = b * l
    heads = SB_WIDTH // HEAD_DIM
    x2 = x.reshape(m, D_MODEL)
    qb, xb2 = _matmul(x2, w_in_t, L0_Q, SB_WIDTH, (BF16,), w_is_t=True, copy_x=True)
    xb = xb2.reshape(b, l, D_MODEL)
    kt_b, kt = _new_keys(w_in_t, xb, L0_K, SB_WIDTH, past_k is not None, l)
    vt_b, vt = _new_keys(w_in_t, xb, L0_V, SB_WIDTH, past_k is not None, l)
    ga, dt_raw = _matmul(xb2, w_in_t, L0_G, SB_WIDTH, (F32,), w_side=w_dt_t, w_is_t=True)
    (z,) = _matmul(xb2, w_in_t, L0_Z, SSD_WIDTH, (F32,), w_is_t=True)
    (xbc,) = _matmul(xb2, w_in_t, L0_XBC, SSD_CONV_DIM, (F32,), w_is_t=True)

    def seq(a):
        return a.reshape(b, l, a.shape[-1])

    if past_k is None:
        o_a = _sb_prompt(seq(qb), kt_b, vt_b, seq(ga))
        k_new, v_new = _token_major(kt, heads), _token_major(vt, heads)
    else:
        o_a = _sb_sample(seq(qb), kt, vt, seq(ga), _feature_major(past_k), _feature_major(past_v))
        k_new, v_new = _sample_token_major(kt, b, heads), _sample_token_major(vt, b, heads)
    xbc3 = seq(xbc)
    o_b, h_new = _ssd(xbc3, seq(z), seq(dt_raw), conv_state, h0, conv_w, conv_b, dt_bias, a_log,
                      d_skip, norm_w, ssd_chunk)
    y, yb = _proj_ln([o_a.reshape(m, SB_WIDTH), o_b.reshape(m, SSD_WIDTH)], x2, w_out, ln_g, ln_b,
                     (F32, BF16))
    tail = jnp.concatenate([conv_state, xbc3], axis=1)[:, -(SSD_CONV - 1):]
    return seq(y), seq(yb), k_new, v_new, tail, h_new.reshape(b, SSD_HEADS, HEAD_DIM, SSD_STATE)


def _odd_layer(x, xb, past_k, past_v, wts):
    w_in, w_kv_t, bias, w_out, ln_g, ln_b = wts
    b, l, _ = x.shape
    m = b * l
    x2 = x.reshape(m, D_MODEL)
    xb2 = xb.reshape(m, D_MODEL)
    keep = min(CA_LEFT, l)
    (qb,) = _matmul(xb2, w_in, 0, CA_WIDTH, (BF16,))
    kt_b, kt = _new_keys(w_kv_t, xb, 0, CA_WIDTH, past_k is not None, keep)
    vt_b, vt = _new_keys(w_kv_t, xb, CA_WIDTH, CA_WIDTH, past_k is not None, keep)
    (g,) = _matmul(xb2, w_in, 3 * CA_WIDTH, CA_WIDTH, (F32,))

    def seq(a):
        return a.reshape(b, l, a.shape[-1])

    if past_k is None:
        o = _band_prompt(seq(qb), kt_b, vt_b, seq(g), bias)
    else:
        o, kt, vt = _band_sample(seq(qb), kt, vt, _feature_major(past_k), _feature_major(past_v), seq(g),
                                 bias)
    (y,) = _proj_ln([o.reshape(m, CA_WIDTH)], x2, w_out, ln_g, ln_b, (F32,))
    heads = CA_WIDTH // HEAD_DIM
    return seq(y), _token_major(kt, heads), _token_major(vt, heads)


def kernel(x_prompt, x_sample, cache_sb_k, cache_sb_v, state_ssm, state_conv, cache_band_k,
           cache_band_v, even_w_in, even_conv_w, even_conv_b, even_dt_bias, even_a_log, even_d_skip,
           even_norm_w, even_w_out, even_ln_g, even_ln_b, odd_w_in, odd_rel_bias, odd_w_out,
           odd_ln_g, odd_ln_b):
    bp, lp, _ = x_prompt.shape
    bs, ls, _ = x_sample.shape
    w_in_t0 = jnp.transpose(even_w_in[0])
    w_dt_t0 = jnp.pad(w_in_t0[L0_DT:].astype(BF16), ((0, LANES - SSD_HEADS), (0, 0)))
    wts0 = (w_in_t0, w_dt_t0, even_conv_w[0], even_conv_b[0], even_dt_bias[0], even_a_log[0],
            even_d_skip[0], even_norm_w[0], even_w_out[0].astype(BF16), even_ln_g[0], even_ln_b[0])
    zero_conv = jnp.zeros((bp, SSD_CONV - 1, SSD_CONV_DIM), F32)
    zero_h = jnp.zeros((bp, SSD_WIDTH, SSD_STATE), F32)
    yp, ypb, p_sb_k, p_sb_v, p_conv, p_ssm = _even_layer(
        x_prompt, None, None, zero_conv, zero_h, wts0, 2 * CHUNK)
    ys, ysb, s_sb_k, s_sb_v, s_conv, s_ssm = _even_layer(
        x_sample, cache_sb_k[0], cache_sb_v[0], state_conv[0],
        state_ssm[0].reshape(bs, SSD_WIDTH, SSD_STATE), wts0, CHUNK)
    w_in1 = odd_w_in[0]
    w_kv_t1 = jnp.transpose(odd_w_in[0][:, CA_WIDTH:3 * CA_WIDTH]).astype(BF16)
    w_out1 = odd_w_out[0].astype(BF16)
    wts1p = (w_in1, w_kv_t1, _band_bias(odd_rel_bias[0]), w_out1, odd_ln_g[0], odd_ln_b[0])
    wts1s = (w_in1, w_kv_t1, _band_sample_bias(odd_rel_bias[0], ls), w_out1, odd_ln_g[0], odd_ln_b[0])
    yp, p_band_k, p_band_v = _odd_layer(yp, ypb, None, None, wts1p)
    ys, s_band_k, s_band_v = _odd_layer(ys, ysb, cache_band_k[0], cache_band_v[0], wts1s)

    def lead(a):
        return a[None]

    return (yp, ys, lead(p_sb_k), lead(p_sb_v), lead(p_ssm), lead(p_conv), lead(p_band_k),
            lead(p_band_v), lead(s_sb_k), lead(s_sb_v), lead(s_ssm), lead(s_conv), lead(s_band_k),
            lead(s_band_v))
```

```python
import functools
import math

import jax
import jax.numpy as jnp
from jax import lax
from jax.experimental import pallas as pl
from jax.experimental.pallas import tpu as pltpu

F32 = jnp.float32
BF16 = jnp.bfloat16

D_MODEL = 2048
CHUNK = 64
HEAD_DIM = 64
LANES = 128
SB_WIDTH = 1024
SSD_WIDTH = 2048
SSD_HEADS = 32
SSD_GROUPS = 4
SSD_STATE = 128
SSD_CONV = 4
SSD_CONV_DIM = SSD_WIDTH + 2 * SSD_GROUPS * SSD_STATE
CA_WIDTH = 2048
CA_LEFT = 8 * CHUNK
CA_REL_CLIP = 128
DEPTH = 2
DEEPNORM_ALPHA = (2 * DEPTH) ** 0.25
LN_EPS = 1e-5
RMS_EPS = 1e-5
ATTN_SCALE = HEAD_DIM ** -0.5
NEG_BIG = -1e30
L0_Q, L0_K, L0_V, L0_G, L0_Z, L0_XBC, L0_DT = 0, 1024, 2048, 3072, 4096, 6144, 9216
VMEM_LIMIT = 56 * 1024 * 1024
ROW_TILE = 512
COL_TILE = 1024

NT_DIMS = (((1,), (1,)), ((), ()))
TN_DIMS = (((0,), (0,)), ((), ()))


def _params(*sem):
    return pltpu.CompilerParams(dimension_semantics=sem, vmem_limit_bytes=VMEM_LIMIT)


def _silu(x):
    return x * (1.0 / (1.0 + jnp.exp(-x)))


def _split3(x):
    p0 = x.astype(BF16)
    r = x - p0.astype(F32)
    p1 = r.astype(BF16)
    p2 = (r - p1.astype(F32)).astype(BF16)
    return p0, p1, p2


def _mm_kernel(x_ref, w_ref, *refs, has_side, w_is_t, copy_x):
    *o_refs, wb_ref = refs

    @pl.when(pl.program_id(1) == 0)
    def _():
        wb_ref[...] = w_ref[...].astype(BF16)

    def product(w):
        if w_is_t:
            return lax.dot_general(x, w, NT_DIMS, preferred_element_type=F32)
        return jnp.dot(x, w, preferred_element_type=F32)

    x = x_ref[...].astype(BF16)
    if copy_x:
        o_refs.pop()[...] = x
    if has_side:
        ws_ref = o_refs.pop(0)
        os_ref = o_refs.pop()
        os_ref[...] = product(ws_ref[...])
    acc = product(wb_ref[...])
    for o_ref in o_refs:
        o_ref[...] = acc.astype(o_ref.dtype)


def _matmul(x, w, col0, ncols, out_dtypes, w_side=None, w_is_t=False, copy_x=False):
    m, k = x.shape
    tm, tn = math.gcd(m, 2 * ROW_TILE), COL_TILE
    assert ncols % tn == 0 and col0 % tn == 0
    cb0 = col0 // tn
    if w_is_t:
        w_spec = pl.BlockSpec((tn, k), lambda j, i: (cb0 + j, 0))
    else:
        w_spec = pl.BlockSpec((k, tn), lambda j, i: (0, cb0 + j))
    in_specs = [pl.BlockSpec((tm, k), lambda j, i: (i, 0)), w_spec]
    out_specs = [pl.BlockSpec((tm, tn), lambda j, i: (i, j)) for _ in out_dtypes]
    out_shape = [jax.ShapeDtypeStruct((m, ncols), dt) for dt in out_dtypes]
    args = [x, w]
    if w_side is not None:
        assert ncols == tn
        ns = w_side.shape[0 if w_is_t else 1]
        in_specs.append(pl.BlockSpec(w_side.shape, lambda j, i: (0, 0)))
        out_specs.append(pl.BlockSpec((tm, ns), lambda j, i: (i, 0)))
        out_shape.append(jax.ShapeDtypeStruct((m, ns), F32))
        args.append(w_side)
    if copy_x:
        assert ncols == tn
        out_specs.append(pl.BlockSpec((tm, k), lambda j, i: (i, 0)))
        out_shape.append(jax.ShapeDtypeStruct((m, k), BF16))
    return pl.pallas_call(
        functools.partial(_mm_kernel, has_side=w_side is not None, w_is_t=w_is_t, copy_x=copy_x),
        grid=(ncols // tn, m // tm),
        in_specs=in_specs,
        out_specs=out_specs,
        out_shape=out_shape,
        scratch_shapes=[pltpu.VMEM(w_spec.block_shape, BF16)],
        compiler_params=_params("arbitrary", "arbitrary"),
        name=f"mm_c{col0}_n{ncols}_m{m}",
    )(*args)


def _mm_t_kernel(w_ref, x_ref, ob_ref, of_ref, *scratch, first_kept):
    if scratch:
        (wb_ref,) = scratch

        @pl.when(jnp.logical_and(pl.program_id(0) == 0, pl.program_id(1) == 0))
        def _():
            wb_ref[...] = w_ref[...].astype(BF16)

        w = wb_ref[...]
    else:
        w = w_ref[...]
    acc = lax.dot_general(w, x_ref[0], NT_DIMS, preferred_element_type=F32)
    ob_ref[0] = acc.astype(ob_ref.dtype)
    kept_cols = of_ref.shape[2]

    def keep_f32():
        of_ref[0] = acc[:, acc.shape[1] - kept_cols:]

    if first_kept == 0:
        keep_f32()
    else:
        pl.when(pl.program_id(1) >= first_kept)(keep_f32)


def _matmul_t(w_t, x, row0, nrows, keep):
    s, l, k = x.shape
    tm = math.gcd(l, 2 * ROW_TILE)
    kept_cols = min(keep, tm)
    assert row0 % nrows == 0 and keep <= l and keep % kept_cols == 0 and tm % kept_cols == 0
    first_kept = (l - keep) // tm
    return pl.pallas_call(
        functools.partial(_mm_t_kernel, first_kept=first_kept),
        grid=(s, l // tm),
        in_specs=[pl.BlockSpec((nrows, k), lambda si, i: (row0 // nrows, 0), pipeline_mode=pl.Buffered(1)),
                  pl.BlockSpec((1, tm, k), lambda si, i: (si, i, 0))],
        out_specs=[pl.BlockSpec((1, nrows, tm), lambda si, i: (si, 0, i)),
                   pl.BlockSpec((1, nrows, kept_cols),
                                lambda si, i: (si, 0, jnp.maximum(i - first_kept, 0)))],
        out_shape=[jax.ShapeDtypeStruct((s, nrows, l), BF16),
                   jax.ShapeDtypeStruct((s, nrows, keep), F32)],
        scratch_shapes=[] if w_t.dtype == BF16 else [pltpu.VMEM((nrows, k), BF16)],
        compiler_params=_params("arbitrary", "arbitrary"),
        name=f"mm_t_r{row0}_n{nrows}_l{l}",
    )(w_t, x)


LN_ROW_GROUP = 128


def _proj_ln_kernel(*refs, widths):
    n = len(widths)
    o_refs = refs[:n]
    x_ref, w_ref, g_ref, b_ref, *y_refs = refs[n:]
    tm = x_ref.shape[0]
    group = math.gcd(tm, LN_ROW_GROUP)
    for r0 in range(0, tm, group):
        rows = slice(r0, r0 + group)
        mix = None
        row = 0
        for o_ref, wd in zip(o_refs, widths):
            part = jnp.dot(o_ref[rows, :], w_ref[row:row + wd, :], preferred_element_type=F32)
            mix = part if mix is None else mix + part
            row += wd
        h = DEEPNORM_ALPHA * x_ref[rows, :] + mix
        mu = jnp.mean(h, axis=-1, keepdims=True)
        hc = h - mu
        var = jnp.mean(hc * hc, axis=-1, keepdims=True)
        y = hc * lax.rsqrt(var + LN_EPS) * g_ref[...] + b_ref[...]
        for y_ref in y_refs:
            y_ref[rows, :] = y.astype(y_ref.dtype)


def _proj_ln(o_parts, x, w_out, ln_g, ln_b, out_dtypes):
    m = x.shape[0]
    widths = tuple(o.shape[1] for o in o_parts)
    ktot = sum(widths)
    tm = math.gcd(m, ROW_TILE)
    assert w_out.shape == (ktot, D_MODEL)
    in_specs = [pl.BlockSpec((tm, wd), lambda i: (i, 0)) for wd in widths]
    in_specs += [pl.BlockSpec((tm, D_MODEL), lambda i: (i, 0)),
                 pl.BlockSpec((ktot, D_MODEL), lambda i: (0, 0), pipeline_mode=pl.Buffered(1)),
                 pl.BlockSpec((1, D_MODEL), lambda i: (0, 0)),
                 pl.BlockSpec((1, D_MODEL), lambda i: (0, 0))]
    return pl.pallas_call(
        functools.partial(_proj_ln_kernel, widths=widths),
        grid=(m // tm,),
        in_specs=in_specs,
        out_specs=[pl.BlockSpec((tm, D_MODEL), lambda i: (i, 0)) for _ in out_dtypes],
        out_shape=[jax.ShapeDtypeStruct((m, D_MODEL), dt) for dt in out_dtypes],
        compiler_params=_params("parallel"),
        name=f"proj_ln_k{ktot}_m{m}",
    )(*o_parts, x, w_out, ln_g.reshape(1, D_MODEL), ln_b.reshape(1, D_MODEL))


def _head_masks(rows):
    lane = lax.broadcasted_iota(jnp.int32, (rows, LANES), 1)
    return lane < HEAD_DIM


SB_KEYS = 256
SB_STEP_LANES = 8 * LANES
SB_DEAD = 104.5
LOG2E = 1.4426950408889634


def _pair_queries(q2):
    head0 = _head_masks(q2.shape[0])
    qs = q2 * jnp.asarray(ATTN_SCALE, q2.dtype)
    zero = jnp.zeros_like(qs)
    return jnp.concatenate([jnp.where(head0, qs, zero), jnp.where(head0, zero, qs)], axis=0)


def _pair_outputs(stacked):
    tq = stacked.shape[0] // 2
    return jnp.where(_head_masks(tq), stacked[:tq], stacked[tq:])


def _sb_visit(qh, k_pairs, v_pairs, carries, accs, upper, diag_mask):
    zs = [jnp.dot(q2, kt, preferred_element_type=F32) for q2, kt in zip(qh, k_pairs)]
    sps = [jnp.maximum(z, 0.0) + jnp.log(1.0 + jnp.exp(-jnp.abs(z))) for z in zs]
    if diag_mask is not None:
        sps = [jnp.where(diag_mask, sp, 0.0) for sp in sps]
    masses = [jnp.dot(sp.astype(BF16), upper, preferred_element_type=F32) + c
              for sp, c in zip(sps, carries)]
    ws = [jnp.exp(z - mass) for z, mass in zip(zs, masses)]
    if diag_mask is not None:
        ws = [jnp.where(diag_mask, w, 0.0) for w in ws]
    accs = tuple(a + lax.dot_general(w.astype(BF16), vt, NT_DIMS, preferred_element_type=F32)
                 for a, w, vt in zip(accs, ws, v_pairs))
    return tuple(mass[:, 0:1] for mass in masses), accs


def _sb_upper(n):
    r = lax.broadcasted_iota(jnp.int32, (n, n), 0)
    c = lax.broadcasted_iota(jnp.int32, (n, n), 1)
    return (r >= c).astype(BF16)


def _sb_alive(carries):
    return jnp.min(functools.reduce(jnp.minimum, carries)) < SB_DEAD


def _sb_queries(q_ref):
    return [_pair_queries(q_ref[0, :, p * LANES:(p + 1) * LANES]) for p in range(q_ref.shape[2] // LANES)]


def _sb_strict(tq, keys):
    r = lax.broadcasted_iota(jnp.int32, (2 * tq, keys), 0)
    c = lax.broadcasted_iota(jnp.int32, (2 * tq, keys), 1)
    return c < jnp.where(r >= tq, r - tq, r)


def _sb_finish(accs, g_ref, o_ref):
    for p, acc in enumerate(accs):
        lanes = slice(p * LANES, (p + 1) * LANES)
        o_ref[0, :, lanes] = (_pair_outputs(acc) * _silu(g_ref[0, :, lanes])).astype(o_ref.dtype)


def _pair_slabs(ref, lead, n_pair, cols):
    return [ref[lead, p * LANES:(p + 1) * LANES, cols].astype(BF16) for p in range(n_pair)]


def _sb_prompt_kernel(q_ref, k_ref, v_ref, g_ref, o_ref):
    qi = pl.program_id(2)
    tq = q_ref.shape[1]
    assert tq == SB_KEYS
    qh = _sb_queries(q_ref)
    n_pair = len(qh)
    upper = _sb_upper(SB_KEYS)
    strict = _sb_strict(tq, SB_KEYS)

    def visit(kb, carries, accs, mask):
        cols = pl.ds(pl.multiple_of(kb * SB_KEYS, SB_KEYS), SB_KEYS)
        return _sb_visit(qh, _pair_slabs(k_ref, 0, n_pair, cols), _pair_slabs(v_ref, 0, n_pair, cols),
                         carries, accs, upper, mask)

    zc = (jnp.zeros((2 * tq, 1), F32),) * n_pair
    za = (jnp.zeros((2 * tq, LANES), F32),) * n_pair
    carries, accs = visit(qi, zc, za, strict)

    def older(state):
        kb, _, carries, accs = state
        carries, accs = visit(kb, carries, accs, None)
        return kb - 1, _sb_alive(carries), carries, accs

    state = lax.while_loop(lambda s: jnp.logical_and(s[0] >= 0, s[1]), older,
                           (qi - 1, _sb_alive(carries), carries, accs))
    _sb_finish(state[3], g_ref, o_ref)


def _sb_prompt(q, kt, vt, g):
    b, l, wdt = q.shape
    tq = SB_KEYS
    assert l % tq == 0 and wdt % SB_STEP_LANES == 0
    blk = pl.BlockSpec((1, tq, SB_STEP_LANES), lambda bi, hp, qi: (bi, qi, hp))
    full = pl.BlockSpec((1, SB_STEP_LANES, l), lambda bi, hp, qi: (bi, hp, 0))
    return pl.pallas_call(
        _sb_prompt_kernel,
        grid=(b, wdt // SB_STEP_LANES, l // tq),
        in_specs=[blk, full, full, blk],
        out_specs=blk,
        out_shape=jax.ShapeDtypeStruct((b, l, wdt), BF16),
        compiler_params=_params("parallel", "parallel", "arbitrary"),
        name="sb_prompt",
    )(q, kt, vt, g)


def _sb_sample_kernel(q_ref, kn_ref, vn_ref, g_ref, kc_hbm, vc_hbm, o_ref, kbuf, vbuf, sem, *, n_cached):
    seq = pl.program_id(0)
    tq = q_ref.shape[1]
    qh = _sb_queries(q_ref)
    n_pair = len(qh)
    upper = _sb_upper(SB_KEYS)
    strict = _sb_strict(tq, LANES)
    zc = (jnp.zeros((2 * tq, 1), F32),) * n_pair
    za = (jnp.zeros((2 * tq, LANES), F32),) * n_pair

    def block_copies(kb, slot):
        cols = pl.ds(pl.multiple_of(kb * SB_KEYS, SB_KEYS), SB_KEYS)
        return (pltpu.make_async_copy(kc_hbm.at[seq, :, cols], kbuf.at[slot], sem.at[0, slot]),
                pltpu.make_async_copy(vc_hbm.at[seq, :, cols], vbuf.at[slot], sem.at[1, slot]))

    def start_block(kb):
        for cp in block_copies(kb, kb % 2):
            cp.start()

    def wait_block(kb):
        for cp in block_copies(kb, kb % 2):
            cp.wait()

    start_block(n_cached - 1)
    everything = slice(None)

    def new_pairs(ref):
        return [_own_columns(ref[p * LANES:(p + 1) * LANES, :], seq, tq).astype(BF16) for p in range(n_pair)]

    carries, accs = _sb_visit(qh, new_pairs(kn_ref), new_pairs(vn_ref), zc, za, _sb_upper(LANES), strict)

    def older(state):
        kb, _, carries, accs = state
        wait_block(kb)
        pl.when(kb > 0)(lambda: start_block(kb - 1))
        slot = kb % 2
        carries, accs = _sb_visit(qh, _pair_slabs(kbuf, slot, n_pair, everything),
                                  _pair_slabs(vbuf, slot, n_pair, everything), carries, accs, upper, None)
        return kb - 1, _sb_alive(carries), carries, accs

    state = lax.while_loop(lambda s: jnp.logical_and(s[0] >= 0, s[1]), older,
                           (jnp.int32(n_cached - 1), _sb_alive(carries), carries, accs))
    pl.when(state[0] >= 0)(lambda: wait_block(state[0]))
    _sb_finish(state[3], g_ref, o_ref)


def _own_columns(block, seq, t):
    assert 2 * t == LANES
    return jnp.where(seq % 2 == 1, pltpu.roll(block, t, axis=1), block)


def _sb_sample(q, knt, vnt, g, kct, vct):
    b, t, wdt = q.shape
    p = kct.shape[2]
    assert p % SB_KEYS == 0 and knt.shape == (wdt, b * t)
    tok = pl.BlockSpec((1, t, wdt), lambda bi: (bi, 0, 0))
    new = pl.BlockSpec((wdt, LANES), lambda bi: (0, bi * t // LANES))
    hbm = pl.BlockSpec(memory_space=pl.ANY)
    return pl.pallas_call(
        functools.partial(_sb_sample_kernel, n_cached=p // SB_KEYS),
        grid=(b,),
        in_specs=[tok, new, new, tok, hbm, hbm],
        out_specs=tok,
        out_shape=jax.ShapeDtypeStruct((b, t, wdt), BF16),
        scratch_shapes=[pltpu.VMEM((2, wdt, SB_KEYS), F32),
                        pltpu.VMEM((2, wdt, SB_KEYS), F32),
                        pltpu.SemaphoreType.DMA((2, 2))],
        compiler_params=_params("arbitrary"),
        name="sb_sample",
    )(q, knt, vnt, g, kct, vct)


CONV_PAD = 8
CONV_SLAB = 512


def _ssd_kernel(xbc_ref, z_ref, dt_ref, cs_ref, h0_ref, cw_ref, cb_ref, dtb_ref, alog_ref, drow_ref,
                nw_ref, o_ref, h_ref, xp_ref, xa_ref, y_ref):
    ci = pl.program_id(1)
    q = xbc_ref.shape[1]
    taps = SSD_CONV - 1

    @pl.when(ci == 0)
    def _():
        xp_ref[0:CONV_PAD, :] = jnp.zeros((CONV_PAD, SSD_CONV_DIM), F32)
        xp_ref[CONV_PAD - taps:CONV_PAD, :] = cs_ref[0]
        h_ref[0] = h0_ref[0]

    @pl.when(ci > 0)
    def _():
        xp_ref[0:CONV_PAD, :] = xp_ref[q:CONV_PAD + q, :]

    xp_ref[CONV_PAD:CONV_PAD + q, :] = xbc_ref[0]
    for cs in range(0, SSD_CONV_DIM, CONV_SLAB):
        xe = xp_ref[:, cs:cs + CONV_SLAB]
        acc = cw_ref[0:1, cs:cs + CONV_SLAB] * xe
        for t in range(1, SSD_CONV):
            acc = cw_ref[t:t + 1, cs:cs + CONV_SLAB] * xe + pltpu.roll(acc, 1, axis=0)
        xa_ref[:, cs:cs + CONV_SLAB] = _silu(acc[CONV_PAD:] + cb_ref[:, cs:cs + CONV_SLAB])

    dtv = dt_ref[0] + dtb_ref[...]
    dtv = jnp.maximum(dtv, 0.0) + jnp.log1p(jnp.exp(-jnp.abs(dtv)))
    da = dtv * (-jnp.exp(alog_ref[...]))
    r = lax.broadcasted_iota(jnp.int32, (q, q), 0)
    c = lax.broadcasted_iota(jnp.int32, (q, q), 1)
    causal = c <= r
    tril = causal.astype(BF16)
    cum = sum(jnp.dot(tril, p, preferred_element_type=F32) for p in _split3(da))
    er = lax.broadcasted_iota(jnp.int32, (LANES, LANES), 0)
    ec = lax.broadcasted_iota(jnp.int32, (LANES, LANES), 1)
    eye = (er == ec).astype(BF16)

    def transpose(x):
        return sum(lax.dot_general(eye, p, NT_DIMS, preferred_element_type=F32) for p in _split3(x))

    cum2 = cum * LOG2E
    cum2_t = transpose(cum) * LOG2E
    dt_t = transpose(dtv)
    cum_last = cum[q - 1:q, :]
    to_end = jnp.exp(cum_last - cum) * dtv
    exp_cum = jnp.exp(cum)
    chunk_decay = jnp.broadcast_to(jnp.exp(cum_last), (LANES, LANES))
    head0 = _head_masks(q)
    top = lax.broadcasted_iota(jnp.int32, (LANES, LANES), 0) < HEAD_DIM

    for g in range(SSD_GROUPS):
        b0 = SSD_WIDTH + g * SSD_STATE
        c0 = SSD_WIDTH + SSD_GROUPS * SSD_STATE + g * SSD_STATE
        bm = xa_ref[:, b0:b0 + SSD_STATE].astype(BF16)
        cm = xa_ref[:, c0:c0 + SSD_STATE].astype(BF16)
        cb = lax.dot_general(cm, bm, NT_DIMS, preferred_element_type=F32)
        for j in range(SSD_HEADS // SSD_GROUPS // 2):
            pr = g * (SSD_HEADS // SSD_GROUPS // 2) + j
            h0, h1 = 2 * pr, 2 * pr + 1
            lo = pr * LANES
            x2 = xa_ref[:, lo:lo + LANES]
            x2b = x2.astype(BF16)
            yd = []
            for h in (h0, h1):
                seg = cum2[:, h:h + 1] - cum2_t[h:h + 1, :]
                wgt = cb * jnp.exp2(jnp.where(causal, seg, -jnp.inf)) * dt_t[h:h + 1, :]
                yd.append(jnp.dot(wgt.astype(BF16), x2b, preferred_element_type=F32))
            y = jnp.where(head0, yd[0], yd[1])
            hs = h_ref[0, lo:lo + LANES, :]
            yoff = lax.dot_general(cm, hs.astype(BF16), NT_DIMS, preferred_element_type=F32)
            y = y + yoff * jnp.where(head0, exp_cum[:, h0:h0 + 1], exp_cum[:, h1:h1 + 1])
            y_ref[:, lo:lo + LANES] = y + drow_ref[:, lo:lo + LANES] * x2
            xw = x2 * jnp.where(head0, to_end[:, h0:h0 + 1], to_end[:, h1:h1 + 1])
            st = lax.dot_general(xw.astype(BF16), bm, TN_DIMS, preferred_element_type=F32)
            dec = jnp.where(top, chunk_decay[:, h0:h0 + 1], chunk_decay[:, h1:h1 + 1])
            h_ref[0, lo:lo + LANES, :] = dec * hs + st

    gw = SSD_WIDTH // SSD_GROUPS
    for g in range(SSD_GROUPS):
        yg = y_ref[:, g * gw:(g + 1) * gw] * _silu(z_ref[0, :, g * gw:(g + 1) * gw])
        ms = jnp.mean(yg * yg, axis=-1, keepdims=True)
        o_ref[0, :, g * gw:(g + 1) * gw] = (yg * lax.rsqrt(ms + RMS_EPS)
                                            * nw_ref[:, g * gw:(g + 1) * gw]).astype(o_ref.dtype)


def _ssd(xbc, z, dt_raw, conv_state, h0, conv_w, conv_b, dt_bias, a_log, d_skip, norm_w, q):
    b, l, _ = xbc.shape
    assert l % q == 0
    pad = LANES - SSD_HEADS
    dtb = jnp.pad(dt_bias, (0, pad)).reshape(1, LANES)
    alog = jnp.pad(a_log, (0, pad)).reshape(1, LANES)
    drow = jnp.repeat(d_skip, HEAD_DIM).reshape(1, SSD_WIDTH)

    def tok(wd):
        return pl.BlockSpec((1, q, wd), lambda bi, ci: (bi, ci, 0))

    def per_seq(shape):
        return pl.BlockSpec((1,) + shape, lambda bi, ci: (bi, 0, 0))

    def const(shape):
        return pl.BlockSpec(shape, lambda bi, ci: (0, 0))

    o, h = pl.pallas_call(
        _ssd_kernel,
        grid=(b, l // q),
        in_specs=[tok(SSD_CONV_DIM), tok(SSD_WIDTH), tok(LANES),
                  per_seq((SSD_CONV - 1, SSD_CONV_DIM)), per_seq((SSD_WIDTH, SSD_STATE)),
                  const((SSD_CONV, SSD_CONV_DIM)), const((1, SSD_CONV_DIM)), const((1, LANES)),
                  const((1, LANES)), const((1, SSD_WIDTH)), const((1, SSD_WIDTH))],
        out_specs=[tok(SSD_WIDTH), per_seq((SSD_WIDTH, SSD_STATE))],
        out_shape=[jax.ShapeDtypeStruct((b, l, SSD_WIDTH), BF16),
                   jax.ShapeDtypeStruct((b, SSD_WIDTH, SSD_STATE), F32)],
        scratch_shapes=[pltpu.VMEM((CONV_PAD + q, SSD_CONV_DIM), F32),
                        pltpu.VMEM((q, SSD_CONV_DIM), F32),
                        pltpu.VMEM((q, SSD_WIDTH), F32)],
        compiler_params=_params("parallel", "arbitrary"),
        name=f"ssd_q{q}",
    )(xbc, z, dt_raw, conv_state, h0, conv_w, conv_b.reshape(1, SSD_CONV_DIM), dtb, alog, drow,
      norm_w.reshape(1, SSD_WIDTH))
    return o, h


BAND_BLOCKS = CA_LEFT // LANES + 1
BAND_PROMPT_PAIRS = 8


def _band_attend(jobs):
    scores = [jnp.dot(_pair_queries(q2), kt, preferred_element_type=F32) + bias2
              for q2, kt, _, bias2, _ in jobs]
    probs = [jnp.exp(s - jnp.max(s, axis=-1, keepdims=True)) for s in scores]
    dens = [jnp.sum(e, axis=-1, keepdims=True) for e in probs]
    nums = [lax.dot_general(e.astype(BF16), job[2], NT_DIMS, preferred_element_type=F32)
            for e, job in zip(probs, jobs)]
    return [_pair_outputs(num / den) * _silu(job[4]) for num, den, job in zip(nums, dens, jobs)]


def _band_prompt_kernel(q_ref, k_ref, v_ref, g_ref, bias_ref, o_ref):
    step = pl.program_id(2)
    n_sub = q_ref.shape[1] // LANES
    n_pair = q_ref.shape[2] // LANES
    band = BAND_BLOCKS * LANES

    def tile(n):
        return slice(n * LANES, (n + 1) * LANES)

    def sub_tile(p, s, i):
        if isinstance(i, int):
            keys = min(i + 1, BAND_BLOCKS) * LANES
            cols = pl.ds((i + 1) * LANES - keys, keys)
        else:
            keys = band
            cols = pl.ds(pl.multiple_of((i + 1) * LANES - band, LANES), band)
        bias2 = jnp.concatenate([bias_ref[2 * p, :, band - keys:], bias_ref[2 * p + 1, :, band - keys:]],
                                axis=0)
        return (q_ref[0, tile(s), tile(p)], k_ref[0, tile(p), cols], v_ref[0, tile(p), cols], bias2,
                g_ref[0, tile(s), tile(p)])

    def step_tiles(first):
        for p in range(n_pair):
            outs = _band_attend([sub_tile(p, s, first + s) for s in range(n_sub)])
            for s, o in enumerate(outs):
                o_ref[0, tile(s), tile(p)] = o.astype(o_ref.dtype)

    pl.when(step == 0)(lambda: step_tiles(0))
    pl.when(step > 0)(lambda: step_tiles(step * n_sub))


def _bias_lookup(rel_bias, t_minus_s):
    rel = jnp.clip(t_minus_s, -CA_REL_CLIP, CA_REL_CLIP) + CA_REL_CLIP
    onehot = (rel[..., None] == jnp.arange(2 * CA_REL_CLIP + 1)).astype(F32)
    return jnp.einsum("rck,hk->hrc", onehot, rel_bias, precision=lax.Precision.HIGHEST)


def _band_bias(rel_bias):
    r = jnp.arange(LANES)[:, None]
    u = jnp.arange(LANES)[None, :]

    def tile(dist):
        return _bias_lookup(rel_bias, dist * LANES + r - u)

    t0 = jnp.where((r < CHUNK) & (u >= CHUNK), NEG_BIG, tile(0))
    t1 = tile(1)
    t2 = jnp.broadcast_to(rel_bias[:, -1][:, None, None], t1.shape)
    t4 = jnp.where((r >= CHUNK) & (u < CHUNK), NEG_BIG, t2)
    return jnp.concatenate([t4, t2, t2, t1, t0], axis=-1).astype(F32)


def _band_prompt(q, kt, vt, g, band_bias):
    b, l, wdt = q.shape
    n_sub = math.gcd(l // LANES, BAND_BLOCKS - 1)
    assert n_sub == BAND_BLOCKS - 1 or l // LANES == n_sub
    tq = n_sub * LANES
    step_lanes = BAND_PROMPT_PAIRS * LANES
    assert wdt % step_lanes == 0
    blk = pl.BlockSpec((1, tq, step_lanes), lambda bi, hp, i: (bi, i, hp))
    full = pl.BlockSpec((1, step_lanes, l), lambda bi, hp, i: (bi, hp, 0))
    bias = pl.BlockSpec((2 * BAND_PROMPT_PAIRS, LANES, BAND_BLOCKS * LANES), lambda bi, hp, i: (hp, 0, 0))
    return pl.pallas_call(
        _band_prompt_kernel,
        grid=(b, wdt // step_lanes, l // tq),
        in_specs=[blk, full, full, blk, bias],
        out_specs=blk,
        out_shape=jax.ShapeDtypeStruct((b, l, wdt), BF16),
        compiler_params=_params("parallel", "parallel", "arbitrary"),
        name="band_prompt",
    )(q, kt, vt, g, band_bias)


BAND_SAMPLE_PAIRS = 4


def _band_sample_kernel(q_ref, kn_ref, vn_ref, kc_ref, vc_ref, g_ref, bias_ref, o_ref, ok_ref, ov_ref):
    seq = pl.program_id(0)
    t = q_ref.shape[1]
    p_len = kc_ref.shape[2]
    lane = lax.broadcasted_iota(jnp.int32, (kc_ref.shape[1], LANES), 1)
    k_new = _own_columns(kn_ref[...], seq, t)
    v_new = _own_columns(vn_ref[...], seq, t)
    for cache_ref, new, out_ref in ((kc_ref, k_new, ok_ref), (vc_ref, v_new, ov_ref)):
        shifted = pltpu.roll(cache_ref[0], p_len - t, axis=1)
        moved = pltpu.roll(new, LANES - t, axis=1)
        out_ref[0, :, :p_len - LANES] = shifted[:, :p_len - LANES]
        out_ref[0, :, p_len - LANES:] = jnp.where(lane < LANES - t, shifted[:, p_len - LANES:], moved)

    def lanes(p):
        return slice(p * LANES, (p + 1) * LANES)

    def pair(p):
        kt = jnp.concatenate([kc_ref[0, lanes(p), :], k_new[lanes(p)]], axis=1).astype(BF16)
        vt = jnp.concatenate([vc_ref[0, lanes(p), :], v_new[lanes(p)]], axis=1).astype(BF16)
        bias2 = jnp.concatenate([bias_ref[2 * p], bias_ref[2 * p + 1]], axis=0)
        return q_ref[0, :, lanes(p)], kt, vt, bias2, g_ref[0, :, lanes(p)]

    for p, o in enumerate(_band_attend([pair(p) for p in range(q_ref.shape[2] // LANES)])):
        o_ref[0, :, lanes(p)] = o.astype(o_ref.dtype)


def _band_sample_bias(rel_bias, tq):
    qi = jnp.arange(tq)[:, None]
    kj = jnp.arange(CA_LEFT + LANES)[None, :]
    return jnp.where(kj < CA_LEFT + tq, _bias_lookup(rel_bias, qi - kj + CA_LEFT), NEG_BIG).astype(F32)


def _band_sample(q, knt, vnt, kct, vct, g, bias):
    b, t, wdt = q.shape
    p = kct.shape[2]
    step_lanes = BAND_SAMPLE_PAIRS * LANES
    assert p == CA_LEFT and t == CHUNK and wdt % step_lanes == 0 and knt.shape == (wdt, b * t)
    tok = pl.BlockSpec((1, t, step_lanes), lambda bi, hp: (bi, 0, hp))
    new = pl.BlockSpec((step_lanes, LANES), lambda bi, hp: (hp, bi * t // LANES))
    old = pl.BlockSpec((1, step_lanes, p), lambda bi, hp: (bi, hp, 0))
    bspec = pl.BlockSpec((2 * BAND_SAMPLE_PAIRS, t, CA_LEFT + LANES), lambda bi, hp: (hp, 0, 0))
    return pl.pallas_call(
        _band_sample_kernel,
        grid=(b, wdt // step_lanes),
        in_specs=[tok, new, new, old, old, tok, bspec],
        out_specs=[tok, old, old],
        out_shape=[jax.ShapeDtypeStruct((b, t, wdt), BF16), jax.ShapeDtypeStruct(kct.shape, F32),
                   jax.ShapeDtypeStruct(vct.shape, F32)],
        compiler_params=_params("parallel", "parallel"),
        name="band_sample",
    )(q, knt, vnt, kct, vct, g, bias)


def _feature_major(cache):
    b, p, h, d = cache.shape
    return jnp.transpose(cache, (0, 2, 3, 1)).reshape(b, h * d, p)


def _token_major(kt, heads):
    b, _, t = kt.shape
    return jnp.transpose(kt.reshape(b, heads, HEAD_DIM, t), (0, 3, 1, 2))


def _new_keys(w_t, xb, row0, nrows, has_cache, keep):
    b, l, d = xb.shape
    if not has_cache:
        return _matmul_t(w_t, xb, row0, nrows, keep)
    _, kt = _matmul_t(w_t, xb.reshape(1, b * l, d), row0, nrows, b * l)
    return None, kt[0]


def _sample_token_major(kt, b, heads):
    t = kt.shape[1] // b
    return jnp.transpose(kt.reshape(heads, HEAD_DIM, b, t), (2, 3, 0, 1))


def _even_layer(x, past_k, past_v, conv_state, h0, wts, ssd_chunk):
    w_in_t, w_dt_t, conv_w, conv_b, dt_bias, a_log, d_skip, norm_w, w_out, ln_g, ln_b = wts
    b, l, _ = x.shape
    m = b * l
    heads = SB_WIDTH // HEAD_DIM
    x2 = x.reshape(m, D_MODEL)
    qb, xb2 = _matmul(x2, w_in_t, L0_Q, SB_WIDTH, (BF16,), w_is_t=True, copy_x=True)
    xb = xb2.reshape(b, l, D_MODEL)
    kt_b, kt = _new_keys(w_in_t, xb, L0_K, SB_WIDTH, past_k is not None, l)
    vt_b, vt = _new_keys(w_in_t, xb, L0_V, SB_WIDTH, past_k is not None, l)
    ga, dt_raw = _matmul(xb2, w_in_t, L0_G, SB_WIDTH, (F32,), w_side=w_dt_t, w_is_t=True)
    (z,) = _matmul(xb2, w_in_t, L0_Z, SSD_WIDTH, (F32,), w_is_t=True)
    (xbc,) = _matmul(xb2, w_in_t, L0_XBC, SSD_CONV_DIM, (F32,), w_is_t=True)

    def seq(a):
        return a.reshape(b, l, a.shape[-1])

    if past_k is None:
        o_a = _sb_prompt(seq(qb), kt_b, vt_b, seq(ga))
        k_new, v_new = _token_major(kt, heads), _token_major(vt, heads)
    else:
        o_a = _sb_sample(seq(qb), kt, vt, seq(ga), _feature_major(past_k), _feature_major(past_v))
        k_new, v_new = _sample_token_major(kt, b, heads), _sample_token_major(vt, b, heads)
    xbc3 = seq(xbc)
    o_b, h_new = _ssd(xbc3, seq(z), seq(dt_raw), conv_state, h0, conv_w, conv_b, dt_bias, a_log,
                      d_skip, norm_w, ssd_chunk)
    y, yb = _proj_ln([o_a.reshape(m, SB_WIDTH), o_b.reshape(m, SSD_WIDTH)], x2, w_out, ln_g, ln_b,
                     (F32, BF16))
    tail = jnp.concatenate([conv_state, xbc3], axis=1)[:, -(SSD_CONV - 1):]
    return seq(y), seq(yb), k_new, v_new, tail, h_new.reshape(b, SSD_HEADS, HEAD_DIM, SSD_STATE)


def _odd_layer(x, xb, past_k, past_v, wts):
    w_in, w_kv_t, bias, w_out, ln_g, ln_b = wts
    b, l, _ = x.shape
    m = b * l
    x2 = x.reshape(m, D_MODEL)
    xb2 = xb.reshape(m, D_MODEL)
    keep = min(CA_LEFT, l)
    (qb,) = _matmul(xb2, w_in, 0, CA_WIDTH, (BF16,))
    kt_b, kt = _new_keys(w_kv_t, xb, 0, CA_WIDTH, past_k is not None, keep)
    vt_b, vt = _new_keys(w_kv_t, xb, CA_WIDTH, CA_WIDTH, past_k is not None, keep)
    (g,) = _matmul(xb2, w_in, 3 * CA_WIDTH, CA_WIDTH, (F32,))

    def seq(a):
        return a.reshape(b, l, a.shape[-1])

    if past_k is None:
        o = _band_prompt(seq(qb), kt_b, vt_b, seq(g), bias)
    else:
        o, kt, vt = _band_sample(seq(qb), kt, vt, _feature_major(past_k), _feature_major(past_v), seq(g),
                                 bias)
    (y,) = _proj_ln([o.reshape(m, CA_WIDTH)], x2, w_out, ln_g, ln_b, (F32,))
    heads = CA_WIDTH // HEAD_DIM
    return seq(y), _token_major(kt, heads), _token_major(vt, heads)


def kernel(x_prompt, x_sample, cache_sb_k, cache_sb_v, state_ssm, state_conv, cache_band_k,
           cache_band_v, even_w_in, even_conv_w, even_conv_b, even_dt_bias, even_a_log, even_d_skip,
           even_norm_w, even_w_out, even_ln_g, even_ln_b, odd_w_in, odd_rel_bias, odd_w_out,
           odd_ln_g, odd_ln_b):
    bp, lp, _ = x_prompt.shape
    bs, ls, _ = x_sample.shape
    w_in_t0 = jnp.transpose(even_w_in[0])
    w_dt_t0 = jnp.pad(w_in_t0[L0_DT:].astype(BF16), ((0, LANES - SSD_HEADS), (0, 0)))
    wts0 = (w_in_t0, w_dt_t0, even_conv_w[0], even_conv_b[0], even_dt_bias[0], even_a_log[0],
            even_d_skip[0], even_norm_w[0], even_w_out[0].astype(BF16), even_ln_g[0], even_ln_b[0])
    zero_conv = jnp.zeros((bp, SSD_CONV - 1, SSD_CONV_DIM), F32)
    zero_h = jnp.zeros((bp, SSD_WIDTH, SSD_STATE), F32)
    yp, ypb, p_sb_k, p_sb_v, p_conv, p_ssm = _even_layer(
        x_prompt, None, None, zero_conv, zero_h, wts0, 2 * CHUNK)
    ys, ysb, s_sb_k, s_sb_v, s_conv, s_ssm = _even_layer(
        x_sample, cache_sb_k[0], cache_sb_v[0], state_conv[0],
        state_ssm[0].reshape(bs, SSD_WIDTH, SSD_STATE), wts0, CHUNK)
    w_in1 = odd_w_in[0]
    w_kv_t1 = jnp.transpose(odd_w_in[0][:, CA_WIDTH:3 * CA_WIDTH]).astype(BF16)
    w_out1 = odd_w_out[0].astype(BF16)
    wts1p = (w_in1, w_kv_t1, _band_bias(odd_rel_bias[0]), w_out1, odd_ln_g[0], odd_ln_b[0])
    wts1s = (w_in1, w_kv_t1, _band_sample_bias(odd_rel_bias[0], ls), w_out1, odd_ln_g[0], odd_ln_b[0])
    yp, p_band_k, p_band_v = _odd_layer(yp, ypb, None, None, wts1p)
    ys, s_band_k, s_band_v = _odd_layer(ys, ysb, cache_band_k[0], cache_band_v[0], wts1s)

    def lead(a):
        return a[None]

    return (yp, ys, lead(p_sb_k), lead(p_sb_v), lead(p_ssm), lead(p_conv), lead(p_band_k),
            lead(p_band_v), lead(s_sb_k), lead(s_sb_v), lead(s_ssm), lead(s_conv), lead(s_band_k),
            lead(s_band_v))
```
